```python
import math
import jax, jax.numpy as jnp
from jax import lax
import numpy as np

D_MODEL = 2048
BATCH = 8
SEQ = 2048
DEPTH = 1
DEC_BATCH = 16
DEC_SEQ = 64
PAST_LEN = 1024

CHUNK = 64
HEAD_DIM = 64
N_Q = 3 * D_MODEL // (8 * HEAD_DIM)
N_KV = 4
GRP = N_Q // N_KV
WINDOW = 128
WIN_CHUNKS = WINDOW // CHUNK
NUM_BUCKETS = 32
MAX_DISTANCE = 128
SSM_W = 3 * D_MODEL // 8
SSM_GROUP_CH = 16
SSM_GROUPS = SSM_W // SSM_GROUP_CH
SSM_STATE = 64
N_MEM = 256
MEM_HEADS = 4
MEM_W = D_MODEL // 4
MEM_HEAD_DIM = MEM_W // MEM_HEADS
N_BRANCH = 3
Q_W = N_Q * HEAD_DIM
KV_W = N_KV * HEAD_DIM
MIX_W = Q_W + SSM_W + MEM_W
IN_COLS = Q_W + 2 * KV_W + SSM_W + MEM_W + N_BRANCH * D_MODEL
IN_SPLITS = [Q_W, Q_W + KV_W, Q_W + 2 * KV_W, Q_W + 2 * KV_W + SSM_W, Q_W + 2 * KV_W + SSM_W + MEM_W]
D_FF = ((8 * D_MODEL // 3 + 255) // 256) * 256
CONV_W = 3
EPS = 1e-6
NEG_INF = -1e30

kernel_name = 'hybrid_swa_s5_memxattn_convffn_step'


def rms_norm(x, g):
    xf = x.astype(jnp.float32)
    y = xf * lax.rsqrt(jnp.mean(xf * xf, axis=-1, keepdims=True) + EPS)
    return (y * g.astype(jnp.float32)).astype(x.dtype)


def t5_bucket(rel):
    half = NUM_BUCKETS // 2
    max_exact = half // 2
    n = jnp.abs(rel)
    large = max_exact + (jnp.log(jnp.maximum(n, 1).astype(jnp.float32) / max_exact)
                         / math.log(MAX_DISTANCE / max_exact) * (half - max_exact)).astype(jnp.int32)
    large = jnp.minimum(large, half - 1)
    return jnp.where(rel > 0, half, 0) + jnp.where(n < max_exact, n, large)


def rel_bias(n_q, n_k, k_offset, table):
    rel = jnp.arange(n_k)[None, :] - k_offset - jnp.arange(n_q)[:, None]
    return jnp.transpose(table[t5_bucket(rel)], (2, 0, 1)).astype(jnp.float32)


def band_attention(q, k, v, bias, mask, sinks):
    b, nb, lq, _, hd = q.shape
    lk = k.shape[2]
    qg = q.reshape(b, nb, lq, N_KV, GRP, hd)
    s = jnp.einsum('bnqhgd,bnkhd->bnhgqk', qg, k).astype(jnp.float32) * (hd ** -0.5)
    s = s + bias.reshape(N_KV, GRP, lq, lk)
    s = jnp.where(mask[None, :, None, None, None, :], s, NEG_INF)
    sink = sinks.astype(jnp.float32).reshape(N_KV, GRP, 1, 1)
    m = jnp.maximum(jnp.max(s, axis=-1, keepdims=True), sink)
    p = jnp.exp(s - m)
    p = p / (jnp.sum(p, axis=-1, keepdims=True) + jnp.exp(sink - m))
    o = jnp.einsum('bnhgqk,bnkhd->bnqhgd', p.astype(v.dtype), v)
    return o.reshape(b, nb * lq, N_Q * hd)


def local_attn_prompt(q, k, v, table, sinks):
    b, t = q.shape[:2]
    nc = t // CHUNK
    pad = ((0, 0), (WINDOW, 0), (0, 0), (0, 0))
    kp = jnp.pad(k, pad).reshape(b, nc + WIN_CHUNKS, CHUNK, N_KV, HEAD_DIM)
    vp = jnp.pad(v, pad).reshape(b, nc + WIN_CHUNKS, CHUNK, N_KV, HEAD_DIM)
    kb = jnp.concatenate([kp[:, i:i + nc] for i in range(WIN_CHUNKS + 1)], axis=2)
    vb = jnp.concatenate([vp[:, i:i + nc] for i in range(WIN_CHUNKS + 1)], axis=2)
    lk = WINDOW + CHUNK
    key_pos = jnp.arange(nc)[:, None] * CHUNK - WINDOW + jnp.arange(lk)[None, :]
    mask = key_pos >= 0
    bias = rel_bias(CHUNK, lk, WINDOW, table)
    o = band_attention(q.reshape(b, nc, CHUNK, N_Q, HEAD_DIM), kb, vb, bias, mask, sinks)
    return o, k[:, -WINDOW:], v[:, -WINDOW:]


def local_attn_sample(q, k, v, k_cache, v_cache, table, sinks):
    lc = k_cache.shape[1]
    tn = q.shape[1]
    kf = jnp.concatenate([k_cache.astype(k.dtype), k], axis=1)
    vf = jnp.concatenate([v_cache.astype(v.dtype), v], axis=1)
    bias = rel_bias(tn, lc + tn, lc, table)
    mask = jnp.ones((1, lc + tn), dtype=bool)
    o = band_attention(q[:, None], kf[:, None], vf[:, None], bias, mask, sinks)
    return o, kf[:, -lc:], vf[:, -lc:]


def _lin_combine(e1, e2):
    a1, b1 = e1
    a2, b2 = e2
    return a1 * a2, a2 * b1 + b2


def s5_branch(u, s0, lp):
    f32 = jnp.float32
    b, t, _ = u.shape
    lam = lax.complex(lp['ssm_a_re'].astype(f32), lp['ssm_a_im'].astype(f32))
    dt = jnp.exp(lp['ssm_log_dt'].astype(f32))[:, None]
    lam_dt = lam * dt
    a_bar = jnp.exp(lam_dt)
    b_mat = lax.complex(lp['ssm_b_re'].astype(f32), lp['ssm_b_im'].astype(f32))
    b_bar = ((a_bar - 1.0) / lam)[..., None] * b_mat
    c_mat = lax.complex(lp['ssm_c_re'].astype(f32), lp['ssm_c_im'].astype(f32))
    ug = u.astype(f32).reshape(b, t, SSM_GROUPS, SSM_GROUP_CH)
    bu = jnp.einsum('btgc,gpc->btgp', ug.astype(jnp.complex64), b_bar)
    a_seq = jnp.broadcast_to(a_bar, bu.shape)
    _, s = lax.associative_scan(_lin_combine, (a_seq, bu), axis=1)
    steps = jnp.arange(1, t + 1, dtype=f32)
    s = s + jnp.exp(steps[:, None, None] * lam_dt)[None] * s0[:, None]
    y = jnp.einsum('btgp,gcp->btgc', s, c_mat).real
    y = y + lp['ssm_d'].astype(f32).reshape(SSM_GROUPS, SSM_GROUP_CH) * ug
    y = jax.nn.gelu(y.reshape(b, t, SSM_W)).astype(u.dtype)
    out = y * jax.nn.sigmoid(y @ lp['w_glu'])
    return out, s[:, -1]


def memory_kv(mem, g_mem, w_mem_kv):
    b = mem.shape[0]
    kv = rms_norm(mem, g_mem) @ w_mem_kv
    mk, mv = jnp.split(kv, 2, axis=-1)
    return (mk.reshape(b, N_MEM, MEM_HEADS, MEM_HEAD_DIM),
            mv.reshape(b, N_MEM, MEM_HEADS, MEM_HEAD_DIM))


def memory_attention(qm, mem_k, mem_v):
    b, t, _ = qm.shape
    q = qm.reshape(b, t, MEM_HEADS, MEM_HEAD_DIM)
    s = jnp.einsum('bthd,bmhd->bhtm', q, mem_k.astype(q.dtype)).astype(jnp.float32) * (MEM_HEAD_DIM ** -0.5)
    p = jax.nn.softmax(s, axis=-1).astype(q.dtype)
    o = jnp.einsum('bhtm,bmhd->bthd', p, mem_v.astype(q.dtype))
    return o.reshape(b, t, MEM_W)


def token_mixers(h, attn_past, s0, mem_k, mem_v, table, lp):
    b, t, _ = h.shape
    proj = h @ lp['w_in']
    q, k, v, u, qm, gates = jnp.split(proj, IN_SPLITS, axis=-1)
    q = q.reshape(b, t, N_Q, HEAD_DIM)
    k = k.reshape(b, t, N_KV, HEAD_DIM)
    v = v.reshape(b, t, N_KV, HEAD_DIM)
    if attn_past is None:
        o_a, k_new, v_new = local_attn_prompt(q, k, v, table, lp['attn_sinks'])
    else:
        o_a, k_new, v_new = local_attn_sample(q, k, v, attn_past[0], attn_past[1], table, lp['attn_sinks'])
    o_s, s_new = s5_branch(u, s0, lp)
    o_m = memory_attention(qm, mem_k, mem_v)
    w_out = lp['w_out']
    g = jax.nn.sigmoid(gates.reshape(b, t, N_BRANCH, D_MODEL))
    merged = (g[:, :, 0] * (o_a @ w_out[:Q_W])
              + g[:, :, 1] * (o_s @ w_out[Q_W:Q_W + SSM_W])
              + g[:, :, 2] * (o_m @ w_out[Q_W + SSM_W:]))
    return merged, k_new, v_new, s_new


def conv_ffn(h, conv_prev, lp):
    t = h.shape[1]
    up = h @ lp['w_up']
    a, bv = jnp.split(up, 2, axis=-1)
    a_ext = jnp.concatenate([conv_prev.astype(a.dtype), a], axis=1)
    conv_w = lp['conv_w']
    a_conv = sum(a_ext[:, i:i + t] * conv_w[i] for i in range(CONV_W)) + lp['conv_b']
    out = (jax.nn.gelu(a_conv) * bv) @ lp['w_down']
    return out, a_ext[:, -(CONV_W - 1):]


def layer(x, attn_past, s0, conv_prev, mem_k, mem_v, table, lp):
    h = rms_norm(x, lp['norm_pre_mix'])
    mix, k_new, v_new, s_new = token_mixers(h, attn_past, s0, mem_k, mem_v, table, lp)
    x = x + rms_norm(mix, lp['norm_post_mix'])
    h2 = rms_norm(x, lp['norm_pre_ffn'])
    f, conv_new = conv_ffn(h2, conv_prev, lp)
    x = x + rms_norm(f, lp['norm_post_ffn'])
    return x, k_new, v_new, s_new, conv_new


def setup_inputs(seed: int = 0) -> dict:
    key = jax.random.key(seed)
    ks = iter(jax.random.split(key, 48))
    f32 = jnp.float32

    def nrm(shape, scale):
        return scale * jax.random.normal(next(ks), shape, f32)

    win_cache = min(WINDOW, PAST_LEN)
    a_im = math.pi * jnp.broadcast_to(jnp.arange(SSM_STATE, dtype=f32), (DEPTH, SSM_GROUPS, SSM_STATE))
    return {
        'x_prompt': nrm((BATCH, SEQ, D_MODEL), 1.0),
        'x_sample': nrm((DEC_BATCH, DEC_SEQ, D_MODEL), 1.0),
        'cache_attn_k': nrm((DEPTH, DEC_BATCH, win_cache, N_KV, HEAD_DIM), 1.0),
        'cache_attn_v': nrm((DEPTH, DEC_BATCH, win_cache, N_KV, HEAD_DIM), 1.0),
        'cache_mem_k': nrm((DEPTH, DEC_BATCH, N_MEM, MEM_HEADS, MEM_HEAD_DIM), 1.0),
        'cache_mem_v': nrm((DEPTH, DEC_BATCH, N_MEM, MEM_HEADS, MEM_HEAD_DIM), 1.0),
        'state_ssm_re': nrm((DEPTH, DEC_BATCH, SSM_GROUPS, SSM_STATE), 0.1),
        'state_ssm_im': nrm((DEPTH, DEC_BATCH, SSM_GROUPS, SSM_STATE), 0.1),
        'state_conv': nrm((DEPTH, DEC_BATCH, CONV_W - 1, D_FF), 1.0),
        'mem_prompt': nrm((BATCH, N_MEM, D_MODEL), 1.0),
        'rel_bias_table': nrm((NUM_BUCKETS, N_Q), 0.5),
        'norm_pre_mix': 1.0 + nrm((DEPTH, D_MODEL), 0.02),
        'norm_post_mix': 1.0 + nrm((DEPTH, D_MODEL), 0.02),
        'norm_pre_ffn': 1.0 + nrm((DEPTH, D_MODEL), 0.02),
        'norm_post_ffn': 1.0 + nrm((DEPTH, D_MODEL), 0.02),
        'norm_mem': 1.0 + nrm((DEPTH, D_MODEL), 0.02),
        'w_in': nrm((DEPTH, D_MODEL, IN_COLS), D_MODEL ** -0.5),
        'attn_sinks': nrm((DEPTH, N_Q), 0.5),
        'ssm_a_re': -0.5 + nrm((DEPTH, SSM_GROUPS, SSM_STATE), 0.01),
        'ssm_a_im': a_im + nrm((DEPTH, SSM_GROUPS, SSM_STATE), 0.01),
        'ssm_log_dt': jax.random.uniform(next(ks), (DEPTH, SSM_GROUPS), f32, math.log(1e-3), math.log(1e-1)),
        'ssm_b_re': nrm((DEPTH, SSM_GROUPS, SSM_STATE, SSM_GROUP_CH), (2 * SSM_GROUP_CH) ** -0.5),
        'ssm_b_im': nrm((DEPTH, SSM_GROUPS, SSM_STATE, SSM_GROUP_CH), (2 * SSM_GROUP_CH) ** -0.5),
        'ssm_c_re': nrm((DEPTH, SSM_GROUPS, SSM_GROUP_CH, SSM_STATE), (2 * SSM_STATE) ** -0.5),
        'ssm_c_im': nrm((DEPTH, SSM_GROUPS, SSM_GROUP_CH, SSM_STATE), (2 * SSM_STATE) ** -0.5),
        'ssm_d': nrm((DEPTH, SSM_W), 1.0),
        'w_glu': nrm((DEPTH, SSM_W, SSM_W), SSM_W ** -0.5),
        'w_mem_kv': nrm((DEPTH, D_MODEL, 2 * MEM_W), D_MODEL ** -0.5),
        'w_out': nrm((DEPTH, MIX_W, D_MODEL), MIX_W ** -0.5),
        'w_up': nrm((DEPTH, D_MODEL, 2 * D_FF), D_MODEL ** -0.5),
        'conv_w': nrm((DEPTH, CONV_W, D_FF), CONV_W ** -0.5),
        'conv_b': nrm((DEPTH, D_FF), 0.01),
        'w_down': nrm((DEPTH, D_FF, D_MODEL), D_FF ** -0.5),
    }


def reference(x_prompt, x_sample, cache_attn_k, cache_attn_v, cache_mem_k, cache_mem_v,
              state_ssm_re, state_ssm_im, state_conv, mem_prompt, rel_bias_table,
              norm_pre_mix, norm_post_mix, norm_pre_ffn, norm_post_ffn, norm_mem,
              w_in, attn_sinks, ssm_a_re, ssm_a_im, ssm_log_dt, ssm_b_re, ssm_b_im,
              ssm_c_re, ssm_c_im, ssm_d, w_glu, w_mem_kv, w_out, w_up, conv_w, conv_b, w_down):
    f32 = jnp.float32
    xp, xs = x_prompt, x_sample
    bp = x_prompt.shape[0]
    pk, pv, psr, psi, pc, pmk, pmv = [], [], [], [], [], [], []
    sk, sv, ssr, ssi, sc = [], [], [], [], []
    for l in range(DEPTH):
        lp = dict(norm_pre_mix=norm_pre_mix[l], norm_post_mix=norm_post_mix[l],
                  norm_pre_ffn=norm_pre_ffn[l], norm_post_ffn=norm_post_ffn[l],
                  w_in=w_in[l], attn_sinks=attn_sinks[l],
                  ssm_a_re=ssm_a_re[l], ssm_a_im=ssm_a_im[l], ssm_log_dt=ssm_log_dt[l],
                  ssm_b_re=ssm_b_re[l], ssm_b_im=ssm_b_im[l], ssm_c_re=ssm_c_re[l], ssm_c_im=ssm_c_im[l],
                  ssm_d=ssm_d[l], w_glu=w_glu[l], w_out=w_out[l],
                  w_up=w_up[l], conv_w=conv_w[l], conv_b=conv_b[l], w_down=w_down[l])
        mk_p, mv_p = memory_kv(mem_prompt, norm_mem[l], w_mem_kv[l])
        s0_p = jnp.zeros((bp, SSM_GROUPS, SSM_STATE), jnp.complex64)
        conv0_p = jnp.zeros((bp, CONV_W - 1, D_FF), xp.dtype)
        xp, k_p, v_p, s_p, c_p = layer(xp, None, s0_p, conv0_p, mk_p, mv_p, rel_bias_table, lp)
        s0_s = lax.complex(state_ssm_re[l].astype(f32), state_ssm_im[l].astype(f32))
        xs, k_s, v_s, s_s, c_s = layer(xs, (cache_attn_k[l], cache_attn_v[l]), s0_s, state_conv[l],
                                       cache_mem_k[l], cache_mem_v[l], rel_bias_table, lp)
        pk.append(k_p); pv.append(v_p); psr.append(s_p.real); psi.append(s_p.imag); pc.append(c_p)
        pmk.append(mk_p); pmv.append(mv_p)
        sk.append(k_s); sv.append(v_s); ssr.append(s_s.real); ssi.append(s_s.imag); sc.append(c_s)
    return (xp, xs,
            jnp.stack(pk), jnp.stack(pv), jnp.stack(psr), jnp.stack(psi), jnp.stack(pc),
            jnp.stack(pmk), jnp.stack(pmv),
            jnp.stack(sk), jnp.stack(sv), jnp.stack(ssr), jnp.stack(ssi), jnp.stack(sc))
```

```python
import functools
import math

import jax
import jax.numpy as jnp
from jax import lax
from jax.experimental import pallas as pl
from jax.experimental.pallas import tpu as pltpu

F32 = jnp.float32
BF16 = jnp.bfloat16

EPS = 1e-6
NEG_INF = -1e30
CHUNK = 64
WINDOW = 128
HEAD_DIM = 64
N_KV = 4
MAX_DISTANCE = 128
SSM_GROUP_CH = 16
MEM_HEADS = 4
CONV_W = 3

MIB = 1024 * 1024
SUBLANES = 8
SCAN_CARRY_ELEMS = 4096


def _params(semantics, vmem_mib):
    return pltpu.CompilerParams(dimension_semantics=semantics, vmem_limit_bytes=vmem_mib * MIB)


def _rms(x, g):
    y = x * lax.rsqrt(jnp.mean(x * x, axis=-1, keepdims=True) + EPS)
    return y * g


def _dot(a, b):
    return jnp.dot(a, b, preferred_element_type=F32)


def _dot_t(a, b):
    return lax.dot_general(a, b, (((1,), (1,)), ((), ())), preferred_element_type=F32)


def _proj_kernel(x_ref, g_ref, w_ref, o_ref):
    h = _rms(x_ref[...], g_ref[...]).astype(BF16)
    o_ref[...] = _dot(h, w_ref[...])


def _norm_proj(x2d, g, w_bf, tm):
    m, d = x2d.shape
    n = w_bf.shape[1]
    return pl.pallas_call(
        _proj_kernel,
        grid=(m // tm,),
        in_specs=[pl.BlockSpec((tm, d), lambda i: (i, 0)),
                  pl.BlockSpec((1, d), lambda i: (0, 0)),
                  pl.BlockSpec((d, n), lambda i: (0, 0))],
        out_specs=pl.BlockSpec((tm, n), lambda i: (i, 0)),
        out_shape=jax.ShapeDtypeStruct((m, n), F32),
        compiler_params=_params(("parallel",), 48),
        name="proj",
    )(x2d, g.reshape(1, d), w_bf)


def _bias_kernel(idx_ref, table_ref, o_ref, *, n_buckets, n_heads):
    idx = idx_ref[...]
    for h in range(n_heads):
        acc = jnp.zeros(idx.shape, F32)
        for b in range(n_buckets):
            acc = jnp.where(idx == b, table_ref[b, h], acc)
        o_ref[h] = acc


def _t5_bucket(rel, n_buckets):
    half = n_buckets // 2
    max_exact = half // 2
    n = jnp.abs(rel)
    large = max_exact + (jnp.log(jnp.maximum(n, 1).astype(F32) / max_exact)
                         / math.log(MAX_DISTANCE / max_exact) * (half - max_exact)).astype(jnp.int32)
    large = jnp.minimum(large, half - 1)
    return jnp.where(rel > 0, half, 0) + jnp.where(n < max_exact, n, large)


def _rel_bias(table):
    n_buckets, n_heads = table.shape
    lk = WINDOW + CHUNK
    rel = jnp.arange(lk)[None, :] - WINDOW - jnp.arange(CHUNK)[:, None]
    idx = _t5_bucket(rel, n_buckets).astype(jnp.int32)
    return pl.pallas_call(
        functools.partial(_bias_kernel, n_buckets=n_buckets, n_heads=n_heads),
        in_specs=[pl.BlockSpec((CHUNK, lk), lambda: (0, 0)),
                  pl.BlockSpec(memory_space=pltpu.SMEM)],
        out_specs=pl.BlockSpec((n_heads, CHUNK, lk), lambda: (0, 0, 0)),
        out_shape=jax.ShapeDtypeStruct((n_heads, CHUNK, lk), F32),
        name="bias",
    )(idx, table.astype(F32))


def _attn_kernel(q_ref, k0_ref, k1_ref, k2_ref, v0_ref, v1_ref, v2_ref, bias_ref, sink_ref,
                 o_ref, o_scr, *, first_valid, n_q):
    c = pl.program_id(1)
    grp = n_q // N_KV
    q = q_ref[...]
    k = jnp.concatenate([k0_ref[...], k1_ref[...], k2_ref[...]], axis=0).astype(BF16)
    v = jnp.concatenate([v0_ref[...], v1_ref[...], v2_ref[...]], axis=0).astype(BF16)
    lk = k.shape[0]
    kpos = c * CHUNK + lax.broadcasted_iota(jnp.int32, (1, lk), 1)
    valid = kpos >= first_valid
    scale = HEAD_DIM ** -0.5
    for h in range(N_KV):
        kh = k[:, h * HEAD_DIM:(h + 1) * HEAD_DIM]
        vh = v[:, h * HEAD_DIM:(h + 1) * HEAD_DIM]
        for g in range(grp):
            hq = h * grp + g
            qh = q[:, hq * HEAD_DIM:(hq + 1) * HEAD_DIM].astype(BF16)
            s = _dot_t(qh, kh) * scale + bias_ref[hq]
            s = jnp.where(valid, s, NEG_INF)
            sink = sink_ref[hq]
            m = jnp.maximum(jnp.max(s, axis=-1, keepdims=True), sink)
            p = jnp.exp(s - m)
            denom = jnp.sum(p, axis=-1, keepdims=True) + jnp.exp(sink - m)
            p = p / denom
            o_scr[:, hq * HEAD_DIM:(hq + 1) * HEAD_DIM] = _dot(p.astype(BF16), vh)
    o_ref[...] = o_scr[...].astype(BF16)


def _band_attention(proj, kf, vf, bias, sinks, b, t, first_valid, q_w, kv_w):
    nc = t // CHUNK
    n_q = q_w // HEAD_DIM
    lk = WINDOW + CHUNK

    def kv_spec(i):
        return pl.BlockSpec((None, CHUNK, kv_w), lambda bi, ci: (bi, ci + i, 0))

    return pl.pallas_call(
        functools.partial(_attn_kernel, first_valid=first_valid, n_q=n_q),
        grid=(b, nc),
        in_specs=[pl.BlockSpec((CHUNK, q_w), lambda bi, ci: (bi * nc + ci, 0)),
                  kv_spec(0), kv_spec(1), kv_spec(2), kv_spec(0), kv_spec(1), kv_spec(2),
                  pl.BlockSpec((n_q, CHUNK, lk), lambda bi, ci: (0, 0, 0)),
                  pl.BlockSpec(memory_space=pltpu.SMEM)],
        out_specs=pl.BlockSpec((CHUNK, q_w), lambda bi, ci: (bi * nc + ci, 0)),
        out_shape=jax.ShapeDtypeStruct((b * t, q_w), BF16),
        scratch_shapes=[pltpu.VMEM((CHUNK, q_w), F32)],
        compiler_params=_params(("parallel", "parallel"), 32),
        name="attn",
    )(proj, kf, kf, kf, vf, vf, vf, bias, sinks.astype(F32))


def _gelu(x):
    return jax.nn.gelu(x)


def _s5_kernel(u_ref, s0re_ref, s0im_ref, are_ref, aim_ref, wbre_ref, wbim_ref, wcre_ref, wcim_ref,
               d_ref, wglu_ref, o_ref, sre_out_ref, sim_out_ref,
               sre_scr, sim_scr, st_re, st_im, y_scr, *, batch, steps, lane_chunk):
    i = pl.program_id(0)

    @pl.when(i == 0)
    def _():
        st_re[...] = s0re_ref[...]
        st_im[...] = s0im_ref[...]

    u = u_ref[...]
    ub = u.astype(BF16)
    n_in_blk, in_blk, st_blk = wbre_ref.shape
    for r in range(n_in_blk):
        blk = ub[:, r * in_blk:(r + 1) * in_blk]
        sre_scr[:, r * st_blk:(r + 1) * st_blk] = _dot(blk, wbre_ref[r])
        sim_scr[:, r * st_blk:(r + 1) * st_blk] = _dot(blk, wbim_ref[r])

    n_state = sre_scr.shape[1]
    for lc in range(n_state // lane_chunk):
        sl = slice(lc * lane_chunk, (lc + 1) * lane_chunk)
        ar = are_ref[:, sl]
        ai = aim_ref[:, sl]

        def body(t, carry, sl=sl, ar=ar, ai=ai):
            sr, si = carry
            rows = pl.ds(pl.multiple_of(t * batch, batch), batch)
            nr = ar * sr - ai * si + sre_scr[rows, sl]
            ni = ar * si + ai * sr + sim_scr[rows, sl]
            sre_scr[rows, sl] = nr
            sim_scr[rows, sl] = ni
            return nr, ni

        sr, si = lax.fori_loop(0, steps, body, (st_re[:, sl], st_im[:, sl]), unroll=4)
        st_re[:, sl] = sr
        st_im[:, sl] = si

    n_out_blk, k_blk, out_blk = wcre_ref.shape
    for kb in range(n_out_blk):
        ksl = slice(kb * k_blk, (kb + 1) * k_blk)
        y_scr[:, kb * out_blk:(kb + 1) * out_blk] = (
            _dot(sre_scr[:, ksl].astype(BF16), wcre_ref[kb])
            + _dot(sim_scr[:, ksl].astype(BF16), wcim_ref[kb]))
    y = _gelu(y_scr[...] + d_ref[...] * u)
    o_ref[...] = (y * jax.nn.sigmoid(_dot(y.astype(BF16), wglu_ref[...]))).astype(BF16)

    @pl.when(i == pl.num_programs(0) - 1)
    def _():
        sre_out_ref[...] = st_re[...]
        sim_out_ref[...] = st_im[...]


def _s5_params(lp):
    lam = lax.complex(lp['ssm_a_re'].astype(F32), lp['ssm_a_im'].astype(F32))
    dt = jnp.exp(lp['ssm_log_dt'].astype(F32))[:, None]
    a_bar = jnp.exp(lam * dt)
    b_mat = lax.complex(lp['ssm_b_re'].astype(F32), lp['ssm_b_im'].astype(F32))
    b_bar = ((a_bar - 1.0) / lam)[..., None] * b_mat
    g, p, c = b_bar.shape
    gb_in = 256 // c
    gb_out = 128 // c

    def b_blocks(x):
        x = x.reshape(g // gb_in, gb_in, p, c).transpose(0, 1, 3, 2)
        x = jnp.einsum('rgcp,gh->rgchp', x, jnp.eye(gb_in, dtype=F32))
        return x.reshape(g // gb_in, gb_in * c, gb_in * p).astype(BF16)

    def c_blocks(x):
        x = x.reshape(g // gb_out, gb_out, c, p).transpose(0, 1, 3, 2)
        x = jnp.einsum('kgpc,gh->kgphc', x, jnp.eye(gb_out, dtype=F32))
        return x.reshape(g // gb_out, gb_out * p, gb_out * c).astype(BF16)

    return dict(a_re=jnp.real(a_bar).reshape(1, g * p), a_im=jnp.imag(a_bar).reshape(1, g * p),
                wb_re=b_blocks(jnp.real(b_bar)), wb_im=b_blocks(jnp.imag(b_bar)),
                wc_re=c_blocks(lp['ssm_c_re'].astype(F32)), wc_im=c_blocks(-lp['ssm_c_im'].astype(F32)),
                d=lp['ssm_d'].astype(F32).reshape(1, -1), w_glu=lp['w_glu'].astype(BF16))


def _s5_layer(u_tb, s0_re, s0_im, sp, b, t):
    ssm_w = u_tb.shape[1]
    n_state = s0_re.shape[1]
    steps = min(t, 512 // b)
    rows = steps * b
    lane_chunk = SCAN_CARRY_ELEMS // b
    a_re = jnp.broadcast_to(sp['a_re'], (b, n_state))
    a_im = jnp.broadcast_to(sp['a_im'], (b, n_state))

    def full(x):
        nd = x.ndim
        return pl.BlockSpec(x.shape, lambda i: (0,) * nd)

    consts = [s0_re, s0_im, a_re, a_im, sp['wb_re'], sp['wb_im'], sp['wc_re'], sp['wc_im'], sp['d'], sp['w_glu']]
    return pl.pallas_call(
        functools.partial(_s5_kernel, batch=b, steps=steps, lane_chunk=lane_chunk),
        grid=(t // steps,),
        in_specs=[pl.BlockSpec((rows, ssm_w), lambda i: (i, 0))] + [full(x) for x in consts],
        out_specs=[pl.BlockSpec((rows, ssm_w), lambda i: (i, 0)),
                   pl.BlockSpec((b, n_state), lambda i: (0, 0)),
                   pl.BlockSpec((b, n_state), lambda i: (0, 0))],
        out_shape=[jax.ShapeDtypeStruct((t * b, ssm_w), BF16),
                   jax.ShapeDtypeStruct((b, n_state), F32),
                   jax.ShapeDtypeStruct((b, n_state), F32)],
        scratch_shapes=[pltpu.VMEM((rows, n_state), F32), pltpu.VMEM((rows, n_state), F32),
                        pltpu.VMEM((b, n_state), F32), pltpu.VMEM((b, n_state), F32),
                        pltpu.VMEM((rows, ssm_w), F32)],
        compiler_params=_params(("arbitrary",), 48),
        name="s5",
    )(u_tb, *consts)


def _memattn_kernel(q_ref, k_ref, v_ref, o_ref, *, head_dim):
    q = q_ref[...]
    k = k_ref[...].astype(BF16)
    v = v_ref[...].astype(BF16)
    scale = head_dim ** -0.5
    for h in range(MEM_HEADS):
        sl = slice(h * head_dim, (h + 1) * head_dim)
        s = _dot_t(q[:, sl].astype(BF16), k[:, sl]) * scale
        m = jnp.max(s, axis=-1, keepdims=True)
        e = jnp.exp(s - m)
        p = e / jnp.sum(e, axis=-1, keepdims=True)
        o_ref[:, sl] = _dot(p.astype(BF16), v[:, sl]).astype(BF16)


def _memory_attention(proj, qm_block, mem_k, mem_v, b, t, tq):
    n_mem, mem_w = mem_k.shape[1:]
    nt = t // tq
    return pl.pallas_call(
        functools.partial(_memattn_kernel, head_dim=mem_w // MEM_HEADS),
        grid=(b, nt),
        in_specs=[pl.BlockSpec((tq, mem_w), lambda bi, ti: (bi * nt + ti, qm_block)),
                  pl.BlockSpec((None, n_mem, mem_w), lambda bi, ti: (bi, 0, 0)),
                  pl.BlockSpec((None, n_mem, mem_w), lambda bi, ti: (bi, 0, 0))],
        out_specs=pl.BlockSpec((tq, mem_w), lambda bi, ti: (bi * nt + ti, 0)),
        out_shape=jax.ShapeDtypeStruct((b * t, mem_w), BF16),
        compiler_params=_params(("parallel", "parallel"), 32),
        name="memattn",
    )(proj, mem_k, mem_v)


def _mix_kernel(x_ref, oa_ref, os_ref, om_ref, gpre_ref, wg0_ref, wg1_ref, wg2_ref,
                wa_ref, ws_ref, wm_ref, gpost_ref, o_ref, h_scr, *, tn):
    j = pl.program_id(1)

    @pl.when(j == 0)
    def _():
        h_scr[...] = _rms(x_ref[...], gpre_ref[...]).astype(BF16)

    h = h_scr[...]
    merged = (jax.nn.sigmoid(_dot(h, wg0_ref[...])) * _dot(oa_ref[...], wa_ref[...])
              + jax.nn.sigmoid(_dot(h, wg1_ref[...])) * _dot(os_ref[...], ws_ref[...])
              + jax.nn.sigmoid(_dot(h, wg2_ref[...])) * _dot(om_ref[...], wm_ref[...]))
    o_ref[:, pl.ds(pl.multiple_of(j * tn, tn), tn)] = merged

    @pl.when(j == pl.num_programs(1) - 1)
    def _():
        o_ref[...] = x_ref[...] + _rms(o_ref[...], gpost_ref[...])


def _mix_residual(x2d, o_a, o_s, o_m, g_pre, w_gates, w_oa, w_os, w_om, g_post, tm, tn):
    m, d = x2d.shape
    nj = d // tn

    def rows(w):
        return pl.BlockSpec((tm, w), lambda i, j: (i, 0))

    def gate_spec(br):
        return pl.BlockSpec((d, tn), lambda i, j: (0, br * nj + j))

    def wout_spec(w):
        return pl.BlockSpec((w.shape[0], tn), lambda i, j: (0, j))

    vec = pl.BlockSpec((1, d), lambda i, j: (0, 0))
    return pl.pallas_call(
        functools.partial(_mix_kernel, tn=tn),
        grid=(m // tm, nj),
        in_specs=[rows(d), rows(o_a.shape[1]), rows(o_s.shape[1]), rows(o_m.shape[1]), vec,
                  gate_spec(0), gate_spec(1), gate_spec(2),
                  wout_spec(w_oa), wout_spec(w_os), wout_spec(w_om), vec],
        out_specs=rows(d),
        out_shape=jax.ShapeDtypeStruct((m, d), F32),
        scratch_shapes=[pltpu.VMEM((tm, d), BF16)],
        compiler_params=_params(("parallel", "arbitrary"), 56),
        name="mix",
    )(x2d, o_a, o_s, o_m, g_pre.reshape(1, d), w_gates, w_gates, w_gates, w_oa, w_os, w_om,
      g_post.reshape(1, d))


def _ffn_kernel(x_ref, gpre_ref, wua_ref, wub_ref, cw_ref, cb_ref, wd_ref, gpost_ref, cprev_ref,
                o_ref, tail_ref, h_scr, acc_scr, a_scr, carry_scr):
    i = pl.program_id(1)
    j = pl.program_id(2)
    nb, tt, d = x_ref.shape
    tf = wua_ref.shape[1]
    rows = nb * tt

    @pl.when(j == 0)
    def _():
        h_scr[...] = _rms(x_ref[...].reshape(rows, d), gpre_ref[...]).astype(BF16)
        acc_scr[...] = jnp.zeros_like(acc_scr)

    h = h_scr[...]
    a = _dot(h, wua_ref[...]).reshape(nb, tt, tf)
    bv = _dot(h, wub_ref[...]).reshape(nb, tt, tf)

    @pl.when(i == 0)
    def _():
        a_scr[:, 0:SUBLANES, :] = cprev_ref[...]

    @pl.when(i > 0)
    def _():
        a_scr[:, 0:SUBLANES, :] = carry_scr[j]

    a_scr[:, SUBLANES:, :] = a
    tail = a_scr[:, tt:tt + SUBLANES, :]
    carry_scr[j] = tail
    tail_ref[...] = tail
    cw = cw_ref[...]
    conv = (a_scr[:, SUBLANES - 2:SUBLANES - 2 + tt, :] * cw[0:1, :]
            + a_scr[:, SUBLANES - 1:SUBLANES - 1 + tt, :] * cw[1:2, :]
            + a * cw[2:3, :]) + cb_ref[...]
    act = (_gelu(conv) * bv).reshape(rows, tf).astype(BF16)
    acc_scr[...] += _dot(act, wd_ref[...])

    @pl.when(j == pl.num_programs(2) - 1)
    def _():
        f = _rms(acc_scr[...], gpost_ref[...]).reshape(nb, tt, d)
        o_ref[...] = x_ref[...] + f


def _conv_ffn(x3d, g_pre, w_up, conv_w, conv_b, w_down, g_post, conv_prev8, nb, tt, tf):
    b, t, d = x3d.shape
    d_ff = w_down.shape[0]
    nj = d_ff // tf
    nt = t // tt
    return pl.pallas_call(
        _ffn_kernel,
        grid=(b // nb, nt, nj),
        in_specs=[pl.BlockSpec((nb, tt, d), lambda bi, i, j: (bi, i, 0)),
                  pl.BlockSpec((1, d), lambda bi, i, j: (0, 0)),
                  pl.BlockSpec((d, tf), lambda bi, i, j: (0, j)),
                  pl.BlockSpec((d, tf), lambda bi, i, j: (0, nj + j)),
                  pl.BlockSpec((CONV_W, tf), lambda bi, i, j: (0, j)),
                  pl.BlockSpec((1, tf), lambda bi, i, j: (0, j)),
                  pl.BlockSpec((tf, d), lambda bi, i, j: (j, 0)),
                  pl.BlockSpec((1, d), lambda bi, i, j: (0, 0)),
                  pl.BlockSpec((nb, SUBLANES, tf), lambda bi, i, j: (bi, 0, j))],
        out_specs=[pl.BlockSpec((nb, tt, d), lambda bi, i, j: (bi, i, 0)),
                   pl.BlockSpec((nb, None, SUBLANES, tf), lambda bi, i, j: (bi, i, 0, j))],
        out_shape=[jax.ShapeDtypeStruct((b, t, d), F32),
                   jax.ShapeDtypeStruct((b, nt, SUBLANES, d_ff), F32)],
        scratch_shapes=[pltpu.VMEM((nb * tt, d), BF16), pltpu.VMEM((nb * tt, d), F32),
                        pltpu.VMEM((nb, tt + SUBLANES, tf), F32), pltpu.VMEM((nj, nb, SUBLANES, tf), F32)],
        compiler_params=_params(("arbitrary", "arbitrary", "arbitrary"), 56),
        name="ffn",
    )(x3d, g_pre.reshape(1, d), w_up, w_up, conv_w.astype(F32), conv_b.astype(F32).reshape(1, d_ff),
      w_down, g_post.reshape(1, d), conv_prev8)


def _layer(x, attn_past, s0, conv_prev, mem_k, mem_v, bias, lw):
    b, t, d = x.shape
    m = b * t
    q_w, kv_w, ssm_w, mem_w = lw['q_w'], lw['kv_w'], lw['ssm_w'], lw['mem_w']
    n_state = lw['n_state']
    d_ff = lw['w_down'].shape[0]
    x2d = x.reshape(m, d)

    proj = _norm_proj(x2d, lw['norm_pre_mix'], lw['w_proj'], 256)
    proj3 = proj.reshape(b, t, -1)
    k = proj3[:, :, q_w:q_w + kv_w]
    v = proj3[:, :, q_w + kv_w:q_w + 2 * kv_w]
    if attn_past is None:
        hist_k = jnp.zeros((b, WINDOW, kv_w), F32)
        hist_v = hist_k
        first_valid = WINDOW
    else:
        hist_k = attn_past[0].astype(F32).reshape(b, WINDOW, kv_w)
        hist_v = attn_past[1].astype(F32).reshape(b, WINDOW, kv_w)
        first_valid = 0
    kf = jnp.concatenate([hist_k, k], axis=1)
    vf = jnp.concatenate([hist_v, v], axis=1)
    o_a = _band_attention(proj, kf, vf, bias, lw['attn_sinks'], b, t, first_valid, q_w, kv_w)

    u_off = q_w + 2 * kv_w
    u_tb = proj3[:, :, u_off:u_off + ssm_w].transpose(1, 0, 2).reshape(t * b, ssm_w)
    if s0 is None:
        s0_re = jnp.zeros((b, n_state), F32)
        s0_im = s0_re
    else:
        s0_re = s0[0].astype(F32).reshape(b, n_state)
        s0_im = s0[1].astype(F32).reshape(b, n_state)
    o_s_tb, s_re, s_im = _s5_layer(u_tb, s0_re, s0_im, lw['s5'], b, t)
    o_s = o_s_tb.reshape(t, b, ssm_w).transpose(1, 0, 2).reshape(m, ssm_w)

    qm_off = u_off + ssm_w
    o_m = _memory_attention(proj, qm_off // mem_w, mem_k, mem_v, b, t, min(t, 512))

    x1 = _mix_residual(x2d, o_a, o_s, o_m, lw['norm_pre_mix'], lw['w_gates'], lw['w_oa'], lw['w_os'],
                       lw['w_om'], lw['norm_post_mix'], 512, 512)

    if conv_prev is None:
        conv_prev8 = jnp.zeros((b, SUBLANES, d_ff), F32)
    else:
        conv_prev8 = jnp.pad(conv_prev.astype(F32), ((0, 0), (SUBLANES - (CONV_W - 1), 0), (0, 0)))
    tt = min(t, 512)
    nb = min(b, 512 // tt)
    x2, tails = _conv_ffn(x1.reshape(b, t, d), lw['norm_pre_ffn'], lw['w_up'], lw['conv_w'], lw['conv_b'],
                          lw['w_down'], lw['norm_post_ffn'], conv_prev8, nb, tt, 512)
    conv_new = tails[:, -1, SUBLANES - (CONV_W - 1):, :]

    n_kv = kv_w // HEAD_DIM
    k_new = kf[:, -WINDOW:].reshape(b, WINDOW, n_kv, HEAD_DIM)
    v_new = vf[:, -WINDOW:].reshape(b, WINDOW, n_kv, HEAD_DIM)
    return x2, k_new, v_new, s_re, s_im, conv_new


def kernel(x_prompt, x_sample, cache_attn_k, cache_attn_v, cache_mem_k, cache_mem_v, state_ssm_re, state_ssm_im, state_conv, mem_prompt, rel_bias_table, norm_pre_mix, norm_post_mix, norm_pre_ffn, norm_post_ffn, norm_mem, w_in, attn_sinks, ssm_a_re, ssm_a_im, ssm_log_dt, ssm_b_re, ssm_b_im, ssm_c_re, ssm_c_im, ssm_d, w_glu, w_mem_kv, w_out, w_up, conv_w, conv_b, w_down):
    depth = w_in.shape[0]
    bp, _, d = x_prompt.shape
    n_mem = mem_prompt.shape[1]
    n_q = attn_sinks.shape[1]
    n_kv, hd = cache_attn_k.shape[-2:]
    assert hd == HEAD_DIM and n_kv == N_KV and cache_attn_k.shape[2] == WINDOW
    groups, p_state = ssm_a_re.shape[1:]
    q_w, kv_w = n_q * HEAD_DIM, n_kv * HEAD_DIM
    ssm_w = ssm_d.shape[1]
    mem_w = w_mem_kv.shape[2] // 2
    proj_w = q_w + 2 * kv_w + ssm_w + mem_w
    mem_hd = mem_w // MEM_HEADS

    xp, xs = x_prompt, x_sample
    outs = [[] for _ in range(12)]
    for l in range(depth):
        bias = _rel_bias(rel_bias_table)
        lp = dict(ssm_a_re=ssm_a_re[l], ssm_a_im=ssm_a_im[l], ssm_log_dt=ssm_log_dt[l],
                  ssm_b_re=ssm_b_re[l], ssm_b_im=ssm_b_im[l], ssm_c_re=ssm_c_re[l], ssm_c_im=ssm_c_im[l],
                  ssm_d=ssm_d[l], w_glu=w_glu[l])
        w_in_l = w_in[l]
        w_out_l = w_out[l].astype(BF16)
        lw = dict(q_w=q_w, kv_w=kv_w, ssm_w=ssm_w, mem_w=mem_w, n_state=groups * p_state,
                  norm_pre_mix=norm_pre_mix[l], norm_post_mix=norm_post_mix[l],
                  norm_pre_ffn=norm_pre_ffn[l], norm_post_ffn=norm_post_ffn[l],
                  w_proj=w_in_l[:, :proj_w].astype(BF16), w_gates=w_in_l[:, proj_w:].astype(BF16),
                  w_oa=w_out_l[:q_w], w_os=w_out_l[q_w:q_w + ssm_w], w_om=w_out_l[q_w + ssm_w:],
                  attn_sinks=attn_sinks[l], s5=_s5_params(lp),
                  w_up=w_up[l].astype(BF16), conv_w=conv_w[l], conv_b=conv_b[l], w_down=w_down[l].astype(BF16))

        mem_kv = _norm_proj(mem_prompt.reshape(bp * n_mem, d), norm_mem[l], w_mem_kv[l].astype(BF16), 256)
        mk_p = mem_kv[:, :mem_w].reshape(bp, n_mem, mem_w)
        mv_p = mem_kv[:, mem_w:].reshape(bp, n_mem, mem_w)
        xp, k_p, v_p, sr_p, si_p, c_p = _layer(xp, None, None, None, mk_p, mv_p, bias, lw)

        bs = xs.shape[0]
        xs, k_s, v_s, sr_s, si_s, c_s = _layer(
            xs, (cache_attn_k[l], cache_attn_v[l]), (state_ssm_re[l], state_ssm_im[l]), state_conv[l],
            cache_mem_k[l].reshape(bs, n_mem, mem_w), cache_mem_v[l].reshape(bs, n_mem, mem_w), bias, lw)

        vals = (k_p, v_p, sr_p.reshape(bp, groups, p_state), si_p.reshape(bp, groups, p_state), c_p,
                mk_p.reshape(bp, n_mem, MEM_HEADS, mem_hd), mv_p.reshape(bp, n_mem, MEM_HEADS, mem_hd),
                k_s, v_s, sr_s.reshape(bs, groups, p_state), si_s.reshape(bs, groups, p_state), c_s)
        for acc, val in zip(outs, vals):
            acc.append(val)
    return (xp, xs) + tuple(jnp.stack(o) for o in outs)
```

```python
import functools
import math

import jax
import jax.numpy as jnp
from jax import lax
from jax.experimental import pallas as pl
from jax.experimental.pallas import tpu as pltpu

F32 = jnp.float32
BF16 = jnp.bfloat16

EPS = 1e-6
NEG_INF = -1e30
CHUNK = 64
WINDOW = 128
HEAD_DIM = 64
N_KV = 4
MAX_DISTANCE = 128
SSM_GROUP_CH = 16
MEM_HEADS = 4
CONV_W = 3

MIB = 1024 * 1024
LANES = 128
SUBLANES = 8
SCAN_CARRY_ELEMS = 4096


def _params(semantics, vmem_mib):
    return pltpu.CompilerParams(dimension_semantics=semantics, vmem_limit_bytes=vmem_mib * MIB)


def _rms(x, g):
    y = x * lax.rsqrt(jnp.mean(x * x, axis=-1, keepdims=True) + EPS)
    return y * g


def _dot(a, b):
    return jnp.dot(a, b, preferred_element_type=F32)


def _dot_t(a, b):
    return lax.dot_general(a, b, (((1,), (1,)), ((), ())), preferred_element_type=F32)


def _proj_kernel(x_ref, g_ref, w_ref, o_ref):
    h = _rms(x_ref[...], g_ref[...]).astype(BF16)
    o_ref[...] = _dot(h, w_ref[...])


def _norm_proj(x2d, g, w_bf, tm):
    m, d = x2d.shape
    n = w_bf.shape[1]
    return pl.pallas_call(
        _proj_kernel,
        grid=(m // tm,),
        in_specs=[pl.BlockSpec((tm, d), lambda i: (i, 0)),
                  pl.BlockSpec((1, d), lambda i: (0, 0)),
                  pl.BlockSpec((d, n), lambda i: (0, 0))],
        out_specs=pl.BlockSpec((tm, n), lambda i: (i, 0)),
        out_shape=jax.ShapeDtypeStruct((m, n), F32),
        compiler_params=_params(("parallel",), 48),
        name="proj",
    )(x2d, g.reshape(1, d), w_bf)


def _bias_kernel(idx_ref, table_ref, o_ref, *, n_buckets, n_heads):
    idx = idx_ref[...]
    for h in range(n_heads):
        acc = jnp.zeros(idx.shape, F32)
        for b in range(n_buckets):
            acc = jnp.where(idx == b, table_ref[b, h], acc)
        o_ref[h] = acc


def _t5_bucket(rel, n_buckets):
    half = n_buckets // 2
    max_exact = half // 2
    n = jnp.abs(rel)
    large = max_exact + (jnp.log(jnp.maximum(n, 1).astype(F32) / max_exact)
                         / math.log(MAX_DISTANCE / max_exact) * (half - max_exact)).astype(jnp.int32)
    large = jnp.minimum(large, half - 1)
    return jnp.where(rel > 0, half, 0) + jnp.where(n < max_exact, n, large)


def _rel_bias(table):
    n_buckets, n_heads = table.shape
    lk = WINDOW + CHUNK
    rel = jnp.arange(lk)[None, :] - WINDOW - jnp.arange(CHUNK)[:, None]
    idx = _t5_bucket(rel, n_buckets).astype(jnp.int32)
    return pl.pallas_call(
        functools.partial(_bias_kernel, n_buckets=n_buckets, n_heads=n_heads),
        in_specs=[pl.BlockSpec((CHUNK, lk), lambda: (0, 0)),
                  pl.BlockSpec(memory_space=pltpu.SMEM)],
        out_specs=pl.BlockSpec((n_heads, CHUNK, lk), lambda: (0, 0, 0)),
        out_shape=jax.ShapeDtypeStruct((n_heads, CHUNK, lk), F32),
        name="bias",
    )(idx, table.astype(F32))


def _attn_kernel(q_ref, k_ref, v_ref, bias_ref, o_ref, *, n_masked, grp):
    nb, t, _ = q_ref.shape
    kv_w = k_ref.shape[2]
    n_chunks = t // CHUNK
    lk = WINDOW + CHUNK
    lkp = bias_ref.shape[2]
    pair_w = 2 * HEAD_DIM
    scale = HEAD_DIM ** -0.5
    low_half = lax.broadcasted_iota(jnp.int32, (1, pair_w), 1) < HEAD_DIM
    ones = jnp.ones((lkp, pair_w), BF16)
    kv_pad = jnp.zeros((lkp - lk, kv_w), BF16)

    def chunk(n, carry):
        bi = n // n_chunks
        c = n % n_chunks
        r0 = pl.multiple_of(c * CHUNK, CHUNK)
        q = q_ref[bi, pl.ds(r0, CHUNK), :] * scale
        k = jnp.concatenate([k_ref[bi, pl.ds(r0, lk), :].astype(BF16), kv_pad], axis=0)
        v = jnp.concatenate([v_ref[bi, pl.ds(r0, lk), :].astype(BF16), kv_pad], axis=0)
        variant = jnp.minimum(c, n_masked) * N_KV
        scores = []
        for h in range(N_KV):
            j, e = divmod(h, 2)
            qp = jnp.concatenate([q[:, g * kv_w + j * pair_w:g * kv_w + (j + 1) * pair_w]
                                  for g in range(grp)], axis=0)
            in_head = low_half if e == 0 else jnp.logical_not(low_half)
            qh = jnp.where(in_head, qp, 0.0).astype(BF16)
            scores.append(_dot_t(qh, k[:, j * pair_w:(j + 1) * pair_w]) + bias_ref[variant + h])
        probs = [jnp.exp(s - jnp.max(s, axis=-1, keepdims=True)).astype(BF16) for s in scores]
        v_ext = [jnp.concatenate([v[:, j * pair_w:(j + 1) * pair_w], ones], axis=1) for j in range(N_KV // 2)]
        sums = [_dot(p, v_ext[h // 2]) for h, p in enumerate(probs)]
        outs = [r[:, :pair_w] / r[:, pair_w:] for r in sums]
        for j in range(N_KV // 2):
            o_pair = jnp.where(low_half, outs[2 * j], outs[2 * j + 1])
            for g in range(grp):
                o_ref[bi, pl.ds(r0, CHUNK), g * kv_w + j * pair_w:g * kv_w + (j + 1) * pair_w] = (
                    o_pair[g * CHUNK:(g + 1) * CHUNK].astype(BF16))
        return carry

    lax.fori_loop(0, nb * n_chunks, chunk, 0, unroll=2)


def _band_attention(proj3, kf, vf, bias, sinks, nb, first_valid, q_w):
    b, t, _ = proj3.shape
    kv_w = kf.shape[2]
    grp = q_w // HEAD_DIM // N_KV
    lk = WINDOW + CHUNK
    lkp = 2 * LANES
    n_masked = first_valid // CHUNK
    sink_col = jnp.broadcast_to(sinks.astype(F32).reshape(N_KV, grp, 1, 1), (N_KV, grp, CHUNK, 1))
    ext = jnp.concatenate([bias.reshape(N_KV, grp * CHUNK, lk), sink_col.reshape(N_KV, grp * CHUNK, 1),
                           jnp.full((N_KV, grp * CHUNK, lkp - lk - 1), NEG_INF, F32)], axis=2)
    key = jnp.arange(lkp)
    variants = [jnp.where(key < (n_masked - i) * CHUNK, NEG_INF, ext) for i in range(n_masked + 1)]
    bias_ext = jnp.concatenate(variants, axis=0)
    return pl.pallas_call(
        functools.partial(_attn_kernel, n_masked=n_masked, grp=grp),
        grid=(b // nb,),
        in_specs=[pl.BlockSpec((nb, t, q_w), lambda i: (i, 0, 0)),
                  pl.BlockSpec((nb, t + WINDOW, kv_w), lambda i: (i, 0, 0)),
                  pl.BlockSpec((nb, t + WINDOW, kv_w), lambda i: (i, 0, 0)),
                  pl.BlockSpec(bias_ext.shape, lambda i: (0, 0, 0))],
        out_specs=pl.BlockSpec((nb, t, q_w), lambda i: (i, 0, 0)),
        out_shape=jax.ShapeDtypeStruct((b, t, q_w), BF16),
        compiler_params=_params(("parallel",), 40),
        name="attn",
    )(proj3, kf, vf, bias_ext)


def _gelu(x):
    return jax.nn.gelu(x)


def _s5_kernel(u_ref, s0re_ref, s0im_ref, are_ref, aim_ref, wbre_ref, wbim_ref, wcre_ref, wcim_ref,
               d_ref, wglu_ref, o_ref, sre_out_ref, sim_out_ref,
               sre_scr, sim_scr, st_re, st_im, y_scr, *, batch, steps, lane_chunk):
    i = pl.program_id(0)

    @pl.when(i == 0)
    def _():
        st_re[...] = s0re_ref[...]
        st_im[...] = s0im_ref[...]

    u = u_ref[...]
    ub = u.astype(BF16)
    n_in_blk, in_blk, st_blk = wbre_ref.shape
    for r in range(n_in_blk):
        blk = ub[:, r * in_blk:(r + 1) * in_blk]
        sre_scr[:, r * st_blk:(r + 1) * st_blk] = _dot(blk, wbre_ref[r])
        sim_scr[:, r * st_blk:(r + 1) * st_blk] = _dot(blk, wbim_ref[r])

    n_state = sre_scr.shape[1]
    for lc in range(n_state // lane_chunk):
        sl = slice(lc * lane_chunk, (lc + 1) * lane_chunk)
        ar = are_ref[:, sl]
        ai = aim_ref[:, sl]

        def body(t, carry, sl=sl, ar=ar, ai=ai):
            sr, si = carry
            rows = pl.ds(pl.multiple_of(t * batch, batch), batch)
            nr = ar * sr - ai * si + sre_scr[rows, sl]
            ni = ar * si + ai * sr + sim_scr[rows, sl]
            sre_scr[rows, sl] = nr
            sim_scr[rows, sl] = ni
            return nr, ni

        sr, si = lax.fori_loop(0, steps, body, (st_re[:, sl], st_im[:, sl]), unroll=4)
        st_re[:, sl] = sr
        st_im[:, sl] = si

    n_out_blk, k_blk, out_blk = wcre_ref.shape
    for kb in range(n_out_blk):
        ksl = slice(kb * k_blk, (kb + 1) * k_blk)
        y_scr[:, kb * out_blk:(kb + 1) * out_blk] = (
            _dot(sre_scr[:, ksl].astype(BF16), wcre_ref[kb])
            + _dot(sim_scr[:, ksl].astype(BF16), wcim_ref[kb]))
    y = _gelu(y_scr[...] + d_ref[...] * u)
    o_ref[...] = (y * jax.nn.sigmoid(_dot(y.astype(BF16), wglu_ref[...]))).astype(BF16)

    @pl.when(i == pl.num_programs(0) - 1)
    def _():
        sre_out_ref[...] = st_re[...]
        sim_out_ref[...] = st_im[...]


def _s5_params(lp):
    lam = lax.complex(lp['ssm_a_re'].astype(F32), lp['ssm_a_im'].astype(F32))
    dt = jnp.exp(lp['ssm_log_dt'].astype(F32))[:, None]
    a_bar = jnp.exp(lam * dt)
    b_mat = lax.complex(lp['ssm_b_re'].astype(F32), lp['ssm_b_im'].astype(F32))
    b_bar = ((a_bar - 1.0) / lam)[..., None] * b_mat
    g, p, c = b_bar.shape
    gb_in = 256 // c
    gb_out = 128 // c

    def b_blocks(x):
        x = x.reshape(g // gb_in, gb_in, p, c).transpose(0, 1, 3, 2)
        x = jnp.einsum('rgcp,gh->rgchp', x, jnp.eye(gb_in, dtype=F32))
        return x.reshape(g // gb_in, gb_in * c, gb_in * p).astype(BF16)

    def c_blocks(x):
        x = x.reshape(g // gb_out, gb_out, c, p).transpose(0, 1, 3, 2)
        x = jnp.einsum('kgpc,gh->kgphc', x, jnp.eye(gb_out, dtype=F32))
        return x.reshape(g // gb_out, gb_out * p, gb_out * c).astype(BF16)

    return dict(a_re=jnp.real(a_bar).reshape(1, g * p), a_im=jnp.imag(a_bar).reshape(1, g * p),
                wb_re=b_blocks(jnp.real(b_bar)), wb_im=b_blocks(jnp.imag(b_bar)),
                wc_re=c_blocks(lp['ssm_c_re'].astype(F32)), wc_im=c_blocks(-lp['ssm_c_im'].astype(F32)),
                d=lp['ssm_d'].astype(F32).reshape(1, -1), w_glu=lp['w_glu'].astype(BF16))


def _s5_layer(u_tb, s0_re, s0_im, sp, b, t):
    ssm_w = u_tb.shape[1]
    n_state = s0_re.shape[1]
    steps = min(t, 512 // b)
    rows = steps * b
    lane_chunk = SCAN_CARRY_ELEMS // b
    a_re = jnp.broadcast_to(sp['a_re'], (b, n_state))
    a_im = jnp.broadcast_to(sp['a_im'], (b, n_state))

    def full(x):
        nd = x.ndim
        return pl.BlockSpec(x.shape, lambda i: (0,) * nd)

    consts = [s0_re, s0_im, a_re, a_im, sp['wb_re'], sp['wb_im'], sp['wc_re'], sp['wc_im'], sp['d'], sp['w_glu']]
    return pl.pallas_call(
        functools.partial(_s5_kernel, batch=b, steps=steps, lane_chunk=lane_chunk),
        grid=(t // steps,),
        in_specs=[pl.BlockSpec((rows, ssm_w), lambda i: (i, 0))] + [full(x) for x in consts],
        out_specs=[pl.BlockSpec((rows, ssm_w), lambda i: (i, 0)),
                   pl.BlockSpec((b, n_state), lambda i: (0, 0)),
                   pl.BlockSpec((b, n_state), lambda i: (0, 0))],
        out_shape=[jax.ShapeDtypeStruct((t * b, ssm_w), BF16),
                   jax.ShapeDtypeStruct((b, n_state), F32),
                   jax.ShapeDtypeStruct((b, n_state), F32)],
        scratch_shapes=[pltpu.VMEM((rows, n_state), F32), pltpu.VMEM((rows, n_state), F32),
                        pltpu.VMEM((b, n_state), F32), pltpu.VMEM((b, n_state), F32),
                        pltpu.VMEM((rows, ssm_w), F32)],
        compiler_params=_params(("arbitrary",), 48),
        name="s5",
    )(u_tb, *consts)


def _memattn_kernel(q_ref, k_ref, v_ref, o_ref, *, head_dim):
    q = q_ref[...]
    k = k_ref[...].astype(BF16)
    v = v_ref[...].astype(BF16)
    scale = head_dim ** -0.5
    for h in range(MEM_HEADS):
        sl = slice(h * head_dim, (h + 1) * head_dim)
        s = _dot_t(q[:, sl].astype(BF16), k[:, sl]) * scale
        m = jnp.max(s, axis=-1, keepdims=True)
        e = jnp.exp(s - m)
        p = e / jnp.sum(e, axis=-1, keepdims=True)
        o_ref[:, sl] = _dot(p.astype(BF16), v[:, sl]).astype(BF16)


def _memory_attention(proj, qm_block, mem_k, mem_v, b, t, tq):
    n_mem, mem_w = mem_k.shape[1:]
    nt = t // tq
    return pl.pallas_call(
        functools.partial(_memattn_kernel, head_dim=mem_w // MEM_HEADS),
        grid=(b, nt),
        in_specs=[pl.BlockSpec((tq, mem_w), lambda bi, ti: (bi * nt + ti, qm_block)),
                  pl.BlockSpec((None, n_mem, mem_w), lambda bi, ti: (bi, 0, 0)),
                  pl.BlockSpec((None, n_mem, mem_w), lambda bi, ti: (bi, 0, 0))],
        out_specs=pl.BlockSpec((tq, mem_w), lambda bi, ti: (bi * nt + ti, 0)),
        out_shape=jax.ShapeDtypeStruct((b * t, mem_w), BF16),
        compiler_params=_params(("parallel", "parallel"), 32),
        name="memattn",
    )(proj, mem_k, mem_v)


def _mix_kernel(x_ref, oa_ref, os_ref, om_ref, gpre_ref, wg0_ref, wg1_ref, wg2_ref,
                wa_ref, ws_ref, wm_ref, gpost_ref, o_ref, h_scr, *, tn):
    j = pl.program_id(1)

    @pl.when(j == 0)
    def _():
        h_scr[...] = _rms(x_ref[...], gpre_ref[...]).astype(BF16)

    h = h_scr[...]
    merged = (jax.nn.sigmoid(_dot(h, wg0_ref[...])) * _dot(oa_ref[...], wa_ref[...])
              + jax.nn.sigmoid(_dot(h, wg1_ref[...])) * _dot(os_ref[...], ws_ref[...])
              + jax.nn.sigmoid(_dot(h, wg2_ref[...])) * _dot(om_ref[...], wm_ref[...]))
    o_ref[:, pl.ds(pl.multiple_of(j * tn, tn), tn)] = merged

    @pl.when(j == pl.num_programs(1) - 1)
    def _():
        o_ref[...] = x_ref[...] + _rms(o_ref[...], gpost_ref[...])


def _mix_residual(x2d, o_a, o_s, o_m, g_pre, w_gates, w_oa, w_os, w_om, g_post, tm, tn):
    m, d = x2d.shape
    nj = d // tn

    def rows(w):
        return pl.BlockSpec((tm, w), lambda i, j: (i, 0))

    def gate_spec(br):
        return pl.BlockSpec((d, tn), lambda i, j: (0, br * nj + j))

    def wout_spec(w):
        return pl.BlockSpec((w.shape[0], tn), lambda i, j: (0, j))

    vec = pl.BlockSpec((1, d), lambda i, j: (0, 0))
    return pl.pallas_call(
        functools.partial(_mix_kernel, tn=tn),
        grid=(m // tm, nj),
        in_specs=[rows(d), rows(o_a.shape[1]), rows(o_s.shape[1]), rows(o_m.shape[1]), vec,
                  gate_spec(0), gate_spec(1), gate_spec(2),
                  wout_spec(w_oa), wout_spec(w_os), wout_spec(w_om), vec],
        out_specs=rows(d),
        out_shape=jax.ShapeDtypeStruct((m, d), F32),
        scratch_shapes=[pltpu.VMEM((tm, d), BF16)],
        compiler_params=_params(("parallel", "arbitrary"), 56),
        name="mix",
    )(x2d, o_a, o_s, o_m, g_pre.reshape(1, d), w_gates, w_gates, w_gates, w_oa, w_os, w_om,
      g_post.reshape(1, d))


def _ffn_kernel(x_ref, gpre_ref, wua_ref, wub_ref, cw_ref, cb_ref, wd_ref, gpost_ref, cprev_ref,
                o_ref, tail_ref, h_scr, acc_scr, a_scr, carry_scr):
    i = pl.program_id(1)
    j = pl.program_id(2)
    nb, tt, d = x_ref.shape
    tf = wua_ref.shape[1]
    rows = nb * tt

    @pl.when(j == 0)
    def _():
        h_scr[...] = _rms(x_ref[...].reshape(rows, d), gpre_ref[...]).astype(BF16)
        acc_scr[...] = jnp.zeros_like(acc_scr)

    h = h_scr[...]
    a = _dot(h, wua_ref[...]).reshape(nb, tt, tf)
    bv = _dot(h, wub_ref[...]).reshape(nb, tt, tf)

    @pl.when(i == 0)
    def _():
        a_scr[:, 0:SUBLANES, :] = cprev_ref[...]

    @pl.when(i > 0)
    def _():
        a_scr[:, 0:SUBLANES, :] = carry_scr[j]

    a_scr[:, SUBLANES:, :] = a
    tail = a_scr[:, tt:tt + SUBLANES, :]
    carry_scr[j] = tail
    tail_ref[...] = tail
    cw = cw_ref[...]
    conv = (a_scr[:, SUBLANES - 2:SUBLANES - 2 + tt, :] * cw[0:1, :]
            + a_scr[:, SUBLANES - 1:SUBLANES - 1 + tt, :] * cw[1:2, :]
            + a * cw[2:3, :]) + cb_ref[...]
    act = (_gelu(conv) * bv).reshape(rows, tf).astype(BF16)
    acc_scr[...] += _dot(act, wd_ref[...])

    @pl.when(j == pl.num_programs(2) - 1)
    def _():
        f = _rms(acc_scr[...], gpost_ref[...]).reshape(nb, tt, d)
        o_ref[...] = x_ref[...] + f


def _conv_ffn(x3d, g_pre, w_up, conv_w, conv_b, w_down, g_post, conv_prev8, nb, tt, tf):
    b, t, d = x3d.shape
    d_ff = w_down.shape[0]
    nj = d_ff // tf
    nt = t // tt
    return pl.pallas_call(
        _ffn_kernel,
        grid=(b // nb, nt, nj),
        in_specs=[pl.BlockSpec((nb, tt, d), lambda bi, i, j: (bi, i, 0)),
                  pl.BlockSpec((1, d), lambda bi, i, j: (0, 0)),
                  pl.BlockSpec((d, tf), lambda bi, i, j: (0, j)),
                  pl.BlockSpec((d, tf), lambda bi, i, j: (0, nj + j)),
                  pl.BlockSpec((CONV_W, tf), lambda bi, i, j: (0, j)),
                  pl.BlockSpec((1, tf), lambda bi, i, j: (0, j)),
                  pl.BlockSpec((tf, d), lambda bi, i, j: (j, 0)),
                  pl.BlockSpec((1, d), lambda bi, i, j: (0, 0)),
                  pl.BlockSpec((nb, SUBLANES, tf), lambda bi, i, j: (bi, 0, j))],
        out_specs=[pl.BlockSpec((nb, tt, d), lambda bi, i, j: (bi, i, 0)),
                   pl.BlockSpec((nb, None, SUBLANES, tf), lambda bi, i, j: (bi, i, 0, j))],
        out_shape=[jax.ShapeDtypeStruct((b, t, d), F32),
                   jax.ShapeDtypeStruct((b, nt, SUBLANES, d_ff), F32)],
        scratch_shapes=[pltpu.VMEM((nb * tt, d), BF16), pltpu.VMEM((nb * tt, d), F32),
                        pltpu.VMEM((nb, tt + SUBLANES, tf), F32), pltpu.VMEM((nj, nb, SUBLANES, tf), F32)],
        compiler_params=_params(("arbitrary", "arbitrary", "arbitrary"), 56),
        name="ffn",
    )(x3d, g_pre.reshape(1, d), w_up, w_up, conv_w.astype(F32), conv_b.astype(F32).reshape(1, d_ff),
      w_down, g_post.reshape(1, d), conv_prev8)


def _layer(x, attn_past, s0, conv_prev, mem_k, mem_v, bias, lw):
    b, t, d = x.shape
    m = b * t
    q_w, kv_w, ssm_w, mem_w = lw['q_w'], lw['kv_w'], lw['ssm_w'], lw['mem_w']
    n_state = lw['n_state']
    d_ff = lw['w_down'].shape[0]
    x2d = x.reshape(m, d)

    proj = _norm_proj(x2d, lw['norm_pre_mix'], lw['w_proj'], 256)
    proj3 = proj.reshape(b, t, -1)
    k = proj3[:, :, q_w:q_w + kv_w]
    v = proj3[:, :, q_w + kv_w:q_w + 2 * kv_w]
    if attn_past is None:
        hist_k = jnp.zeros((b, WINDOW, kv_w), F32)
        hist_v = hist_k
        first_valid = WINDOW
    else:
        hist_k = attn_past[0].astype(F32).reshape(b, WINDOW, kv_w)
        hist_v = attn_past[1].astype(F32).reshape(b, WINDOW, kv_w)
        first_valid = 0
    kf = jnp.concatenate([hist_k, k], axis=1)
    vf = jnp.concatenate([hist_v, v], axis=1)
    nb_attn = max(1, min(b, 1024 // t))
    o_a = _band_attention(proj3, kf, vf, bias, lw['attn_sinks'], nb_attn, first_valid, q_w).reshape(m, q_w)

    u_off = q_w + 2 * kv_w
    u_tb = proj3[:, :, u_off:u_off + ssm_w].transpose(1, 0, 2).reshape(t * b, ssm_w)
    if s0 is None:
        s0_re = jnp.zeros((b, n_state), F32)
        s0_im = s0_re
    else:
        s0_re = s0[0].astype(F32).reshape(b, n_state)
        s0_im = s0[1].astype(F32).reshape(b, n_state)
    o_s_tb, s_re, s_im = _s5_layer(u_tb, s0_re, s0_im, lw['s5'], b, t)
    o_s = o_s_tb.reshape(t, b, ssm_w).transpose(1, 0, 2).reshape(m, ssm_w)

    qm_off = u_off + ssm_w
    o_m = _memory_attention(proj, qm_off // mem_w, mem_k, mem_v, b, t, min(t, 512))

    x1 = _mix_residual(x2d, o_a, o_s, o_m, lw['norm_pre_mix'], lw['w_gates'], lw['w_oa'], lw['w_os'],
                       lw['w_om'], lw['norm_post_mix'], 512, 512)

    if conv_prev is None:
        conv_prev8 = jnp.zeros((b, SUBLANES, d_ff), F32)
    else:
        conv_prev8 = jnp.pad(conv_prev.astype(F32), ((0, 0), (SUBLANES - (CONV_W - 1), 0), (0, 0)))
    tt = min(t, 512)
    nb = min(b, 512 // tt)
    x2, tails = _conv_ffn(x1.reshape(b, t, d), lw['norm_pre_ffn'], lw['w_up'], lw['conv_w'], lw['conv_b'],
                          lw['w_down'], lw['norm_post_ffn'], conv_prev8, nb, tt, 512)
    conv_new = tails[:, -1, SUBLANES - (CONV_W - 1):, :]

    n_kv = kv_w // HEAD_DIM
    k_new = kf[:, -WINDOW:].reshape(b, WINDOW, n_kv, HEAD_DIM)
    v_new = vf[:, -WINDOW:].reshape(b, WINDOW, n_kv, HEAD_DIM)
    return x2, k_new, v_new, s_re, s_im, conv_new


def kernel(x_prompt, x_sample, cache_attn_k, cache_attn_v, cache_mem_k, cache_mem_v, state_ssm_re, state_ssm_im, state_conv, mem_prompt, rel_bias_table, norm_pre_mix, norm_post_mix, norm_pre_ffn, norm_post_ffn, norm_mem, w_in, attn_sinks, ssm_a_re, ssm_a_im, ssm_log_dt, ssm_b_re, ssm_b_im, ssm_c_re, ssm_c_im, ssm_d, w_glu, w_mem_kv, w_out, w_up, conv_w, conv_b, w_down):
    depth = w_in.shape[0]
    bp, _, d = x_prompt.shape
    n_mem = mem_prompt.shape[1]
    n_q = attn_sinks.shape[1]
    n_kv, hd = cache_attn_k.shape[-2:]
    assert hd == HEAD_DIM and n_kv == N_KV and cache_attn_k.shape[2] == WINDOW
    groups, p_state = ssm_a_re.shape[1:]
    q_w, kv_w = n_q * HEAD_DIM, n_kv * HEAD_DIM
    ssm_w = ssm_d.shape[1]
    mem_w = w_mem_kv.shape[2] // 2
    proj_w = q_w + 2 * kv_w + ssm_w + mem_w
    mem_hd = mem_w // MEM_HEADS

    xp, xs = x_prompt, x_sample
    outs = [[] for _ in range(12)]
    for l in range(depth):
        bias = _rel_bias(rel_bias_table)
        lp = dict(ssm_a_re=ssm_a_re[l], ssm_a_im=ssm_a_im[l], ssm_log_dt=ssm_log_dt[l],
                  ssm_b_re=ssm_b_re[l], ssm_b_im=ssm_b_im[l], ssm_c_re=ssm_c_re[l], ssm_c_im=ssm_c_im[l],
                  ssm_d=ssm_d[l], w_glu=w_glu[l])
        w_in_l = w_in[l]
        w_out_l = w_out[l].astype(BF16)
        grp = n_q // n_kv
        w_q = w_in_l[:, :q_w].reshape(d, n_kv, grp, HEAD_DIM).transpose(0, 2, 1, 3).reshape(d, q_w)
        w_proj = jnp.concatenate([w_q, w_in_l[:, q_w:proj_w]], axis=1).astype(BF16)
        w_oa = w_out_l[:q_w].reshape(n_kv, grp, HEAD_DIM, d).transpose(1, 0, 2, 3).reshape(q_w, d)
        lw = dict(q_w=q_w, kv_w=kv_w, ssm_w=ssm_w, mem_w=mem_w, n_state=groups * p_state,
                  norm_pre_mix=norm_pre_mix[l], norm_post_mix=norm_post_mix[l],
                  norm_pre_ffn=norm_pre_ffn[l], norm_post_ffn=norm_post_ffn[l],
                  w_proj=w_proj, w_gates=w_in_l[:, proj_w:].astype(BF16),
                  w_oa=w_oa, w_os=w_out_l[q_w:q_w + ssm_w], w_om=w_out_l[q_w + ssm_w:],
                  attn_sinks=attn_sinks[l], s5=_s5_params(lp),
                  w_up=w_up[l].astype(BF16), conv_w=conv_w[l], conv_b=conv_b[l], w_down=w_down[l].astype(BF16))

        mem_kv = _norm_proj(mem_prompt.reshape(bp * n_mem, d), norm_mem[l], w_mem_kv[l].astype(BF16), 256)
        mk_p = mem_kv[:, :mem_w].reshape(bp, n_mem, mem_w)
        mv_p = mem_kv[:, mem_w:].reshape(bp, n_mem, mem_w)
        xp, k_p, v_p, sr_p, si_p, c_p = _layer(xp, None, None, None, mk_p, mv_p, bias, lw)

        bs = xs.shape[0]
        xs, k_s, v_s, sr_s, si_s, c_s = _layer(
            xs, (cache_attn_k[l], cache_attn_v[l]), (state_ssm_re[l], state_ssm_im[l]), state_conv[l],
            cache_mem_k[l].reshape(bs, n_mem, mem_w), cache_mem_v[l].reshape(bs, n_mem, mem_w), bias, lw)

        vals = (k_p, v_p, sr_p.reshape(bp, groups, p_state), si_p.reshape(bp, groups, p_state), c_p,
                mk_p.reshape(bp, n_mem, MEM_HEADS, mem_hd), mv_p.reshape(bp, n_mem, MEM_HEADS, mem_hd),
                k_s, v_s, sr_s.reshape(bs, groups, p_state), si_s.reshape(bs, groups, p_state), c_s)
        for acc, val in zip(outs, vals):
            acc.append(val)
    return (xp, xs) + tuple(jnp.stack(o) for o in outs)
```

```python
import functools
import math

import jax
import jax.numpy as jnp
from jax import lax
from jax.experimental import pallas as pl
from jax.experimental.pallas import tpu as pltpu

F32 = jnp.float32
BF16 = jnp.bfloat16

EPS = 1e-6
NEG_INF = -1e30
CHUNK = 64
WINDOW = 128
HEAD_DIM = 64
N_KV = 4
MAX_DISTANCE = 128
SSM_GROUP_CH = 16
MEM_HEADS = 4
CONV_W = 3

MIB = 1024 * 1024
LANES = 128
MXU_WIDTH = 256
SUBLANES = 8
SCAN_CARRY_ELEMS = 4096


def _params(semantics, vmem_mib):
    return pltpu.CompilerParams(dimension_semantics=semantics, vmem_limit_bytes=vmem_mib * MIB)


def _rms(x, g):
    y = x * lax.rsqrt(jnp.mean(x * x, axis=-1, keepdims=True) + EPS)
    return y * g


def _dot(a, b):
    return jnp.dot(a, b, preferred_element_type=F32)


def _dot_t(a, b):
    return lax.dot_general(a, b, (((1,), (1,)), ((), ())), preferred_element_type=F32)


def _proj_kernel(x_ref, g_ref, w_ref, *o_refs):
    h = _rms(x_ref[...], g_ref[...]).astype(BF16)
    r = _dot(h, w_ref[...])
    off = 0
    for o_ref in o_refs:
        w = o_ref.shape[1]
        o_ref[...] = r[:, off:off + w].astype(o_ref.dtype)
        off += w


def _norm_proj(x2d, g, w_bf, segments, tm):
    m, d = x2d.shape
    n = w_bf.shape[1]
    assert sum(w for w, _ in segments) == n
    return pl.pallas_call(
        _proj_kernel,
        grid=(m // tm,),
        in_specs=[pl.BlockSpec((tm, d), lambda i: (i, 0)),
                  pl.BlockSpec((1, d), lambda i: (0, 0)),
                  pl.BlockSpec((d, n), lambda i: (0, 0))],
        out_specs=[pl.BlockSpec((tm, w), lambda i: (i, 0)) for w, _ in segments],
        out_shape=[jax.ShapeDtypeStruct((m, w), dt) for w, dt in segments],
        compiler_params=_params(("parallel",), 48),
        name="proj",
    )(x2d, g.reshape(1, d), w_bf)


def _bias_kernel(idx_ref, table_ref, o_ref, *, n_buckets, n_heads):
    idx = idx_ref[...]
    for h in range(n_heads):
        acc = jnp.zeros(idx.shape, F32)
        for b in range(n_buckets):
            acc = jnp.where(idx == b, table_ref[b, h], acc)
        o_ref[h] = acc


def _t5_bucket(rel, n_buckets):
    half = n_buckets // 2
    max_exact = half // 2
    n = jnp.abs(rel)
    large = max_exact + (jnp.log(jnp.maximum(n, 1).astype(F32) / max_exact)
                         / math.log(MAX_DISTANCE / max_exact) * (half - max_exact)).astype(jnp.int32)
    large = jnp.minimum(large, half - 1)
    return jnp.where(rel > 0, half, 0) + jnp.where(n < max_exact, n, large)


def _rel_bias(table):
    n_buckets, n_heads = table.shape
    lk = WINDOW + CHUNK
    rel = jnp.arange(lk)[None, :] - WINDOW - jnp.arange(CHUNK)[:, None]
    idx = _t5_bucket(rel, n_buckets).astype(jnp.int32)
    return pl.pallas_call(
        functools.partial(_bias_kernel, n_buckets=n_buckets, n_heads=n_heads),
        in_specs=[pl.BlockSpec((CHUNK, lk), lambda: (0, 0)),
                  pl.BlockSpec(memory_space=pltpu.SMEM)],
        out_specs=pl.BlockSpec((n_heads, CHUNK, lk), lambda: (0, 0, 0)),
        out_shape=jax.ShapeDtypeStruct((n_heads, CHUNK, lk), F32),
        name="bias",
    )(idx, table.astype(F32))


def _attn_kernel(q_ref, k_ref, v_ref, hk_ref, hv_ref, bias_ref, o_ref, kf_scr, vf_scr, *, n_masked, grp):
    nb, t, _ = q_ref.shape
    kv_w = k_ref.shape[2]
    n_chunks = t // CHUNK
    lk = WINDOW + CHUNK
    lkp = bias_ref.shape[2]
    pair_w = 2 * HEAD_DIM
    scale = HEAD_DIM ** -0.5
    low_half = lax.broadcasted_iota(jnp.int32, (1, pair_w), 1) < HEAD_DIM
    ones = jnp.ones((lkp, pair_w), BF16)
    kv_pad = jnp.zeros((lkp - lk, kv_w), BF16)

    kf_scr[:, 0:WINDOW, :] = hk_ref[...].astype(BF16)
    kf_scr[:, WINDOW:, :] = k_ref[...].astype(BF16)
    vf_scr[:, 0:WINDOW, :] = hv_ref[...].astype(BF16)
    vf_scr[:, WINDOW:, :] = v_ref[...].astype(BF16)

    def chunk(n, carry):
        bi = n // n_chunks
        c = n % n_chunks
        r0 = pl.multiple_of(c * CHUNK, CHUNK)
        q = q_ref[bi, pl.ds(r0, CHUNK), :] * scale
        k = jnp.concatenate([kf_scr[bi, pl.ds(r0, lk), :], kv_pad], axis=0)
        v = jnp.concatenate([vf_scr[bi, pl.ds(r0, lk), :], kv_pad], axis=0)
        variant = jnp.minimum(c, n_masked) * N_KV
        scores = []
        for h in range(N_KV):
            j, e = divmod(h, 2)
            qp = jnp.concatenate([q[:, g * kv_w + j * pair_w:g * kv_w + (j + 1) * pair_w]
                                  for g in range(grp)], axis=0)
            in_head = low_half if e == 0 else jnp.logical_not(low_half)
            qh = jnp.where(in_head, qp, jnp.zeros_like(qp))
            scores.append(_dot_t(qh, k[:, j * pair_w:(j + 1) * pair_w]) + bias_ref[variant + h])
        probs = [jnp.exp(s - jnp.max(s, axis=-1, keepdims=True)).astype(BF16) for s in scores]
        v_ext = [jnp.concatenate([v[:, j * pair_w:(j + 1) * pair_w], ones], axis=1) for j in range(N_KV // 2)]
        sums = [_dot(p, v_ext[h // 2]) for h, p in enumerate(probs)]
        outs = [r[:, :pair_w] / r[:, pair_w:] for r in sums]
        for j in range(N_KV // 2):
            o_pair = jnp.where(low_half, outs[2 * j], outs[2 * j + 1])
            for g in range(grp):
                o_ref[bi, pl.ds(r0, CHUNK), g * kv_w + j * pair_w:g * kv_w + (j + 1) * pair_w] = (
                    o_pair[g * CHUNK:(g + 1) * CHUNK].astype(BF16))
        return carry

    lax.fori_loop(0, nb * n_chunks, chunk, 0, unroll=2)


def _band_attention(q3, kv3, hist_k, hist_v, bias, sinks, nb, first_valid):
    b, t, q_w = q3.shape
    kv_w = hist_k.shape[2]
    grp = q_w // HEAD_DIM // N_KV
    lk = WINDOW + CHUNK
    lkp = 2 * LANES
    n_masked = first_valid // CHUNK
    sink_col = jnp.broadcast_to(sinks.astype(F32).reshape(N_KV, grp, 1, 1), (N_KV, grp, CHUNK, 1))
    ext = jnp.concatenate([bias.reshape(N_KV, grp * CHUNK, lk), sink_col.reshape(N_KV, grp * CHUNK, 1),
                           jnp.full((N_KV, grp * CHUNK, lkp - lk - 1), NEG_INF, F32)], axis=2)
    key = jnp.arange(lkp)
    variants = [jnp.where(key < (n_masked - i) * CHUNK, NEG_INF, ext) for i in range(n_masked + 1)]
    bias_ext = jnp.concatenate(variants, axis=0)
    return pl.pallas_call(
        functools.partial(_attn_kernel, n_masked=n_masked, grp=grp),
        grid=(b // nb,),
        in_specs=[pl.BlockSpec((nb, t, q_w), lambda i: (i, 0, 0)),
                  pl.BlockSpec((nb, t, kv_w), lambda i: (i, 0, 0)),
                  pl.BlockSpec((nb, t, kv_w), lambda i: (i, 0, 1)),
                  pl.BlockSpec((nb, WINDOW, kv_w), lambda i: (i, 0, 0)),
                  pl.BlockSpec((nb, WINDOW, kv_w), lambda i: (i, 0, 0)),
                  pl.BlockSpec(bias_ext.shape, lambda i: (0, 0, 0))],
        out_specs=pl.BlockSpec((nb, t, q_w), lambda i: (i, 0, 0)),
        out_shape=jax.ShapeDtypeStruct((b, t, q_w), BF16),
        scratch_shapes=[pltpu.VMEM((nb, WINDOW + t, kv_w), BF16), pltpu.VMEM((nb, WINDOW + t, kv_w), BF16)],
        compiler_params=_params(("parallel",), 40),
        name="attn",
    )(q3, kv3, kv3, hist_k, hist_v, bias_ext)


def _gelu(x):
    return jax.nn.gelu(x)


def _s5_kernel(u_ref, s0re_ref, s0im_ref, are_ref, aim_ref, wbre_ref, wbim_ref, wcre_ref, wcim_ref,
               d_ref, wglu_ref, o_ref, sre_out_ref, sim_out_ref,
               sre_scr, sim_scr, st_re, st_im, y_scr, *, batch, steps, lane_chunk):
    i = pl.program_id(0)

    @pl.when(i == 0)
    def _():
        st_re[...] = s0re_ref[...]
        st_im[...] = s0im_ref[...]

    u = u_ref[...]
    ub = u.astype(BF16)
    n_in_blk, in_blk, st_blk = wbre_ref.shape
    for r in range(n_in_blk):
        blk = ub[:, r * in_blk:(r + 1) * in_blk]
        sre_scr[:, r * st_blk:(r + 1) * st_blk] = _dot(blk, wbre_ref[r])
        sim_scr[:, r * st_blk:(r + 1) * st_blk] = _dot(blk, wbim_ref[r])

    n_state = sre_scr.shape[1]
    for lc in range(n_state // lane_chunk):
        sl = slice(lc * lane_chunk, (lc + 1) * lane_chunk)
        ar = are_ref[:, sl]
        ai = aim_ref[:, sl]

        def body(t, carry, sl=sl, ar=ar, ai=ai):
            sr, si = carry
            rows = pl.ds(pl.multiple_of(t * batch, batch), batch)
            nr = ar * sr - ai * si + sre_scr[rows, sl]
            ni = ar * si + ai * sr + sim_scr[rows, sl]
            sre_scr[rows, sl] = nr
            sim_scr[rows, sl] = ni
            return nr, ni

        sr, si = lax.fori_loop(0, steps, body, (st_re[:, sl], st_im[:, sl]), unroll=4)
        st_re[:, sl] = sr
        st_im[:, sl] = si

    n_out_blk, k_blk, out_blk = wcre_ref.shape
    for kb in range(n_out_blk):
        ksl = slice(kb * k_blk, (kb + 1) * k_blk)
        y_scr[:, kb * out_blk:(kb + 1) * out_blk] = (
            _dot(sre_scr[:, ksl].astype(BF16), wcre_ref[kb])
            + _dot(sim_scr[:, ksl].astype(BF16), wcim_ref[kb]))
    y = _gelu(y_scr[...] + d_ref[...] * u)
    o_ref[...] = (y * jax.nn.sigmoid(_dot(y.astype(BF16), wglu_ref[...]))).astype(BF16)

    @pl.when(i == pl.num_programs(0) - 1)
    def _():
        sre_out_ref[...] = st_re[...]
        sim_out_ref[...] = st_im[...]


def _s5_params(lp):
    lam = lax.complex(lp['ssm_a_re'].astype(F32), lp['ssm_a_im'].astype(F32))
    dt = jnp.exp(lp['ssm_log_dt'].astype(F32))[:, None]
    a_bar = jnp.exp(lam * dt)
    b_mat = lax.complex(lp['ssm_b_re'].astype(F32), lp['ssm_b_im'].astype(F32))
    b_bar = ((a_bar - 1.0) / lam)[..., None] * b_mat
    g, p, c = b_bar.shape
    gb_in = 256 // c
    gb_out = 128 // c

    def b_blocks(x):
        x = x.reshape(g // gb_in, gb_in, p, c).transpose(0, 1, 3, 2)
        x = jnp.einsum('rgcp,gh->rgchp', x, jnp.eye(gb_in, dtype=F32))
        return x.reshape(g // gb_in, gb_in * c, gb_in * p).astype(BF16)

    def c_blocks(x):
        x = x.reshape(g // gb_out, gb_out, c, p).transpose(0, 1, 3, 2)
        x = jnp.einsum('kgpc,gh->kgphc', x, jnp.eye(gb_out, dtype=F32))
        return x.reshape(g // gb_out, gb_out * p, gb_out * c).astype(BF16)

    return dict(a_re=jnp.real(a_bar).reshape(1, g * p), a_im=jnp.imag(a_bar).reshape(1, g * p),
                wb_re=b_blocks(jnp.real(b_bar)), wb_im=b_blocks(jnp.imag(b_bar)),
                wc_re=c_blocks(lp['ssm_c_re'].astype(F32)), wc_im=c_blocks(-lp['ssm_c_im'].astype(F32)),
                d=lp['ssm_d'].astype(F32).reshape(1, -1), w_glu=lp['w_glu'].astype(BF16))


def _s5_layer(u_tb, s0_re, s0_im, sp, steps):
    t, b, ssm_w = u_tb.shape
    n_state = s0_re.shape[1]
    rows = steps * b
    lane_chunk = SCAN_CARRY_ELEMS // b
    a_re = jnp.broadcast_to(sp['a_re'], (b, n_state))
    a_im = jnp.broadcast_to(sp['a_im'], (b, n_state))

    def full(x):
        nd = x.ndim
        return pl.BlockSpec(x.shape, lambda i: (0,) * nd)

    consts = [s0_re, s0_im, a_re, a_im, sp['wb_re'], sp['wb_im'], sp['wc_re'], sp['wc_im'], sp['d'], sp['w_glu']]
    return pl.pallas_call(
        functools.partial(_s5_kernel, batch=b, steps=steps, lane_chunk=lane_chunk),
        grid=(t // steps,),
        in_specs=[pl.BlockSpec((rows, ssm_w), lambda i: (i, 0))] + [full(x) for x in consts],
        out_specs=[pl.BlockSpec((rows, ssm_w), lambda i: (i, 0)),
                   pl.BlockSpec((b, n_state), lambda i: (0, 0)),
                   pl.BlockSpec((b, n_state), lambda i: (0, 0))],
        out_shape=[jax.ShapeDtypeStruct((t * b, ssm_w), BF16),
                   jax.ShapeDtypeStruct((b, n_state), F32),
                   jax.ShapeDtypeStruct((b, n_state), F32)],
        scratch_shapes=[pltpu.VMEM((rows, n_state), F32), pltpu.VMEM((rows, n_state), F32),
                        pltpu.VMEM((b, n_state), F32), pltpu.VMEM((b, n_state), F32),
                        pltpu.VMEM((rows, ssm_w), F32)],
        compiler_params=_params(("arbitrary",), 48),
        name="s5",
    )(u_tb.reshape(t * b, ssm_w), *consts)


def _memattn_kernel(q_ref, k_ref, v_ref, o_ref, *, head_dim):
    q = q_ref[...]
    k = k_ref[...].astype(BF16)
    v = v_ref[...].astype(BF16)
    scale = head_dim ** -0.5
    for h in range(MEM_HEADS):
        sl = slice(h * head_dim, (h + 1) * head_dim)
        s = _dot_t(q[:, sl], k[:, sl]) * scale
        m = jnp.max(s, axis=-1, keepdims=True)
        e = jnp.exp(s - m)
        p = e / jnp.sum(e, axis=-1, keepdims=True)
        o_ref[:, sl] = _dot(p.astype(BF16), v[:, sl]).astype(BF16)


def _memory_attention(qm, mem_k, mem_v, b, t, tq):
    n_mem, mem_w = mem_k.shape[1:]
    nt = t // tq
    return pl.pallas_call(
        functools.partial(_memattn_kernel, head_dim=mem_w // MEM_HEADS),
        grid=(b, nt),
        in_specs=[pl.BlockSpec((tq, mem_w), lambda bi, ti: (bi * nt + ti, 0)),
                  pl.BlockSpec((None, n_mem, mem_w), lambda bi, ti: (bi, 0, 0)),
                  pl.BlockSpec((None, n_mem, mem_w), lambda bi, ti: (bi, 0, 0))],
        out_specs=pl.BlockSpec((tq, mem_w), lambda bi, ti: (bi * nt + ti, 0)),
        out_shape=jax.ShapeDtypeStruct((b * t, mem_w), BF16),
        compiler_params=_params(("parallel", "parallel"), 32),
        name="memattn",
    )(qm, mem_k, mem_v)


def _mix_kernel(x_ref, oa_ref, os_ref, om_ref, gpre_ref, wg0_ref, wg1_ref, wg2_ref,
                wa_ref, ws_ref, wm_ref, gpost_ref, o_ref, h_scr, *, tn):
    j = pl.program_id(1)

    @pl.when(j == 0)
    def _():
        h_scr[...] = _rms(x_ref[...], gpre_ref[...]).astype(BF16)

    h = h_scr[...]
    merged = (jax.nn.sigmoid(_dot(h, wg0_ref[...])) * _dot(oa_ref[...], wa_ref[...])
              + jax.nn.sigmoid(_dot(h, wg1_ref[...])) * _dot(os_ref[...], ws_ref[...])
              + jax.nn.sigmoid(_dot(h, wg2_ref[...])) * _dot(om_ref[...], wm_ref[...]))
    o_ref[:, pl.ds(pl.multiple_of(j * tn, tn), tn)] = merged

    @pl.when(j == pl.num_programs(1) - 1)
    def _():
        o_ref[...] = x_ref[...] + _rms(o_ref[...], gpost_ref[...])


def _mix_residual(x2d, o_a, o_s, o_m, g_pre, w_gates, w_oa, w_os, w_om, g_post, tm, tn):
    m, d = x2d.shape
    nj = d // tn

    def rows(w):
        return pl.BlockSpec((tm, w), lambda i, j: (i, 0))

    def gate_spec(br):
        return pl.BlockSpec((d, tn), lambda i, j: (0, br * nj + j))

    def wout_spec(w):
        return pl.BlockSpec((w.shape[0], tn), lambda i, j: (0, j))

    vec = pl.BlockSpec((1, d), lambda i, j: (0, 0))
    return pl.pallas_call(
        functools.partial(_mix_kernel, tn=tn),
        grid=(m // tm, nj),
        in_specs=[rows(d), rows(o_a.shape[1]), rows(o_s.shape[1]), rows(o_m.shape[1]), vec,
                  gate_spec(0), gate_spec(1), gate_spec(2),
                  wout_spec(w_oa), wout_spec(w_os), wout_spec(w_om), vec],
        out_specs=rows(d),
        out_shape=jax.ShapeDtypeStruct((m, d), F32),
        scratch_shapes=[pltpu.VMEM((tm, d), BF16)],
        compiler_params=_params(("parallel", "arbitrary"), 56),
        name="mix",
    )(x2d, o_a, o_s, o_m, g_pre.reshape(1, d), w_gates, w_gates, w_gates, w_oa, w_os, w_om,
      g_post.reshape(1, d))


def _ffn_kernel(x_ref, gpre_ref, wua_ref, wub_ref, cw_ref, cb_ref, wd_ref, gpost_ref, cprev_ref,
                o_ref, tail_ref, h_scr, acc_scr, a_scr, carry_scr):
    i = pl.program_id(1)
    j = pl.program_id(2)
    nb, tt, d = x_ref.shape
    tf = wua_ref.shape[1]
    rows = nb * tt

    @pl.when(j == 0)
    def _():
        h_scr[...] = _rms(x_ref[...].reshape(rows, d), gpre_ref[...]).astype(BF16)
        acc_scr[...] = jnp.zeros_like(acc_scr)

    h = h_scr[...]
    a_scr[:, 0:SUBLANES, :] = jnp.where(i == 0, cprev_ref[...], carry_scr[j])
    n_slices = tf // MXU_WIDTH
    cols = [slice(s * MXU_WIDTH, (s + 1) * MXU_WIDTH) for s in range(n_slices)]
    ups = [(_dot(h, wua_ref[:, sl]), _dot(h, wub_ref[:, sl])) for sl in cols]
    cw = cw_ref[...]
    cb = cb_ref[...]
    for sl, (a, bv) in zip(cols, ups):
        a = a.reshape(nb, tt, MXU_WIDTH)
        a_scr[:, SUBLANES:, sl] = a
        conv = (a_scr[:, SUBLANES - 2:SUBLANES - 2 + tt, sl] * cw[0:1, sl]
                + a_scr[:, SUBLANES - 1:SUBLANES - 1 + tt, sl] * cw[1:2, sl]
                + a * cw[2:3, sl]) + cb[:, sl]
        act = (_gelu(conv) * bv.reshape(nb, tt, MXU_WIDTH)).reshape(rows, MXU_WIDTH).astype(BF16)
        acc_scr[...] += _dot(act, wd_ref[sl, :])
    tail = a_scr[:, tt:tt + SUBLANES, :]
    carry_scr[j] = tail
    tail_ref[...] = tail

    @pl.when(j == pl.num_programs(2) - 1)
    def _():
        f = _rms(acc_scr[...], gpost_ref[...]).reshape(nb, tt, d)
        o_ref[...] = x_ref[...] + f


def _conv_ffn(x3d, g_pre, w_up, conv_w, conv_b, w_down, g_post, conv_prev8, nb, tt, tf):
    b, t, d = x3d.shape
    d_ff = w_down.shape[0]
    nj = d_ff // tf
    nt = t // tt
    return pl.pallas_call(
        _ffn_kernel,
        grid=(b // nb, nt, nj),
        in_specs=[pl.BlockSpec((nb, tt, d), lambda bi, i, j: (bi, i, 0)),
                  pl.BlockSpec((1, d), lambda bi, i, j: (0, 0)),
                  pl.BlockSpec((d, tf), lambda bi, i, j: (0, j)),
                  pl.BlockSpec((d, tf), lambda bi, i, j: (0, nj + j)),
                  pl.BlockSpec((CONV_W, tf), lambda bi, i, j: (0, j)),
                  pl.BlockSpec((1, tf), lambda bi, i, j: (0, j)),
                  pl.BlockSpec((tf, d), lambda bi, i, j: (j, 0)),
                  pl.BlockSpec((1, d), lambda bi, i, j: (0, 0)),
                  pl.BlockSpec((nb, SUBLANES, tf), lambda bi, i, j: (bi, 0, j))],
        out_specs=[pl.BlockSpec((nb, tt, d), lambda bi, i, j: (bi, i, 0)),
                   pl.BlockSpec((nb, None, SUBLANES, tf), lambda bi, i, j: (bi, i, 0, j))],
        out_shape=[jax.ShapeDtypeStruct((b, t, d), F32),
                   jax.ShapeDtypeStruct((b, nt, SUBLANES, d_ff), F32)],
        scratch_shapes=[pltpu.VMEM((nb * tt, d), BF16), pltpu.VMEM((nb * tt, d), F32),
                        pltpu.VMEM((nb, tt + SUBLANES, tf), F32), pltpu.VMEM((nj, nb, SUBLANES, tf), F32)],
        compiler_params=_params(("arbitrary", "arbitrary", "arbitrary"), 56),
        name="ffn",
    )(x3d, g_pre.reshape(1, d), w_up, w_up, conv_w.astype(F32), conv_b.astype(F32).reshape(1, d_ff),
      w_down, g_post.reshape(1, d), conv_prev8)


def _tiles(b, t):
    row_tile = 512
    tt = min(t, row_tile)
    return dict(
        proj_rows=256,
        attn_batch=max(1, min(b, 1024 // t)),
        s5_steps=min(t, row_tile // b),
        memattn_rows=tt,
        mix_rows=row_tile, mix_cols=512,
        ffn_batch=min(b, row_tile // tt), ffn_rows=tt, ffn_cols=2 * MXU_WIDTH)


def _layer(x, attn_past, s0, conv_prev, mem_k, mem_v, bias, lw):
    b, t, d = x.shape
    m = b * t
    q_w, kv_w, ssm_w, mem_w = lw['q_w'], lw['kv_w'], lw['ssm_w'], lw['mem_w']
    n_state = lw['n_state']
    d_ff = lw['w_down'].shape[0]
    tiles = _tiles(b, t)
    x2d = x.reshape(m, d)

    q, kv, u, qm = _norm_proj(x2d, lw['norm_pre_mix'], lw['w_proj'],
                              [(q_w, BF16), (2 * kv_w, F32), (ssm_w, F32), (mem_w, BF16)], tiles['proj_rows'])
    kv3 = kv.reshape(b, t, 2 * kv_w)
    if attn_past is None:
        hist_k = jnp.zeros((b, WINDOW, kv_w), F32)
        hist_v = hist_k
        first_valid = WINDOW
    else:
        hist_k = attn_past[0].astype(F32).reshape(b, WINDOW, kv_w)
        hist_v = attn_past[1].astype(F32).reshape(b, WINDOW, kv_w)
        first_valid = 0
    o_a = _band_attention(q.reshape(b, t, q_w), kv3, hist_k, hist_v, bias, lw['attn_sinks'],
                          tiles['attn_batch'], first_valid).reshape(m, q_w)

    if s0 is None:
        s0_re = jnp.zeros((b, n_state), F32)
        s0_im = s0_re
    else:
        s0_re = s0[0].astype(F32).reshape(b, n_state)
        s0_im = s0[1].astype(F32).reshape(b, n_state)
    u_tb = u.reshape(b, t, ssm_w).transpose(1, 0, 2)
    o_s_tb, s_re, s_im = _s5_layer(u_tb, s0_re, s0_im, lw['s5'], tiles['s5_steps'])
    o_s = o_s_tb.reshape(t, b, ssm_w).transpose(1, 0, 2).reshape(m, ssm_w)

    o_m = _memory_attention(qm, mem_k, mem_v, b, t, tiles['memattn_rows'])

    x1 = _mix_residual(x2d, o_a, o_s, o_m, lw['norm_pre_mix'], lw['w_gates'], lw['w_oa'],
                       lw['w_os'], lw['w_om'], lw['norm_post_mix'], tiles['mix_rows'], tiles['mix_cols'])

    if conv_prev is None:
        conv_prev8 = jnp.zeros((b, SUBLANES, d_ff), F32)
    else:
        conv_prev8 = jnp.pad(conv_prev.astype(F32), ((0, 0), (SUBLANES - (CONV_W - 1), 0), (0, 0)))
    x2, tails = _conv_ffn(x1.reshape(b, t, d), lw['norm_pre_ffn'], lw['w_up'], lw['conv_w'], lw['conv_b'],
                          lw['w_down'], lw['norm_post_ffn'], conv_prev8,
                          tiles['ffn_batch'], tiles['ffn_rows'], tiles['ffn_cols'])
    conv_new = tails[:, -1, SUBLANES - (CONV_W - 1):, :]

    n_kv = kv_w // HEAD_DIM
    k_new = jnp.concatenate([hist_k, kv3[:, :, :kv_w]], axis=1)[:, -WINDOW:].reshape(b, WINDOW, n_kv, HEAD_DIM)
    v_new = jnp.concatenate([hist_v, kv3[:, :, kv_w:]], axis=1)[:, -WINDOW:].reshape(b, WINDOW, n_kv, HEAD_DIM)
    return x2, k_new, v_new, s_re, s_im, conv_new


def kernel(x_prompt, x_sample, cache_attn_k, cache_attn_v, cache_mem_k, cache_mem_v, state_ssm_re, state_ssm_im, state_conv, mem_prompt, rel_bias_table, norm_pre_mix, norm_post_mix, norm_pre_ffn, norm_post_ffn, norm_mem, w_in, attn_sinks, ssm_a_re, ssm_a_im, ssm_log_dt, ssm_b_re, ssm_b_im, ssm_c_re, ssm_c_im, ssm_d, w_glu, w_mem_kv, w_out, w_up, conv_w, conv_b, w_down):
    depth = w_in.shape[0]
    bp, _, d = x_prompt.shape
    n_mem = mem_prompt.shape[1]
    n_q = attn_sinks.shape[1]
    n_kv, hd = cache_attn_k.shape[-2:]
    assert hd == HEAD_DIM and n_kv == N_KV and cache_attn_k.shape[2] == WINDOW
    groups, p_state = ssm_a_re.shape[1:]
    q_w, kv_w = n_q * HEAD_DIM, n_kv * HEAD_DIM
    ssm_w = ssm_d.shape[1]
    mem_w = w_mem_kv.shape[2] // 2
    proj_w = q_w + 2 * kv_w + ssm_w + mem_w
    mem_hd = mem_w // MEM_HEADS

    xp, xs = x_prompt, x_sample
    outs = [[] for _ in range(12)]
    for l in range(depth):
        bias = _rel_bias(rel_bias_table)
        lp = dict(ssm_a_re=ssm_a_re[l], ssm_a_im=ssm_a_im[l], ssm_log_dt=ssm_log_dt[l],
                  ssm_b_re=ssm_b_re[l], ssm_b_im=ssm_b_im[l], ssm_c_re=ssm_c_re[l], ssm_c_im=ssm_c_im[l],
                  ssm_d=ssm_d[l], w_glu=w_glu[l])
        w_in_l = w_in[l]
        w_out_l = w_out[l].astype(BF16)
        grp = n_q // n_kv
        w_q = w_in_l[:, :q_w].reshape(d, n_kv, grp, HEAD_DIM).transpose(0, 2, 1, 3).reshape(d, q_w)
        w_proj = jnp.concatenate([w_q, w_in_l[:, q_w:proj_w]], axis=1).astype(BF16)
        w_oa = w_out_l[:q_w].reshape(n_kv, grp, HEAD_DIM, d).transpose(1, 0, 2, 3).reshape(q_w, d)
        lw = dict(q_w=q_w, kv_w=kv_w, ssm_w=ssm_w, mem_w=mem_w, n_state=groups * p_state,
                  norm_pre_mix=norm_pre_mix[l], norm_post_mix=norm_post_mix[l],
                  norm_pre_ffn=norm_pre_ffn[l], norm_post_ffn=norm_post_ffn[l],
                  w_proj=w_proj, w_gates=w_in_l[:, proj_w:].astype(BF16),
                  w_oa=w_oa, w_os=w_out_l[q_w:q_w + ssm_w], w_om=w_out_l[q_w + ssm_w:],
                  attn_sinks=attn_sinks[l], s5=_s5_params(lp),
                  w_up=w_up[l].astype(BF16), conv_w=conv_w[l], conv_b=conv_b[l], w_down=w_down[l].astype(BF16))

        mk_p, mv_p = _norm_proj(mem_prompt.reshape(bp * n_mem, d), norm_mem[l], w_mem_kv[l].astype(BF16),
                                [(mem_w, F32), (mem_w, F32)], _tiles(bp, n_mem)['proj_rows'])
        mk_p = mk_p.reshape(bp, n_mem, mem_w)
        mv_p = mv_p.reshape(bp, n_mem, mem_w)
        xp, k_p, v_p, sr_p, si_p, c_p = _layer(xp, None, None, None, mk_p, mv_p, bias, lw)

        bs = xs.shape[0]
        xs, k_s, v_s, sr_s, si_s, c_s = _layer(
            xs, (cache_attn_k[l], cache_attn_v[l]), (state_ssm_re[l], state_ssm_im[l]), state_conv[l],
            cache_mem_k[l].reshape(bs, n_mem, mem_w), cache_mem_v[l].reshape(bs, n_mem, mem_w), bias, lw)

        vals = (k_p, v_p, sr_p.reshape(bp, groups, p_state), si_p.reshape(bp, groups, p_state), c_p,
                mk_p.reshape(bp, n_mem, MEM_HEADS, mem_hd), mv_p.reshape(bp, n_mem, MEM_HEADS, mem_hd),
                k_s, v_s, sr_s.reshape(bs, groups, p_state), si_s.reshape(bs, groups, p_state), c_s)
        for acc, val in zip(outs, vals):
            acc.append(val)
    return (xp, xs) + tuple(jnp.stack(o) for o in outs)
```

```python
import functools
import math

import jax
import jax.numpy as jnp
from jax import lax
from jax.experimental import pallas as pl
from jax.experimental.pallas import tpu as pltpu

F32 = jnp.float32
BF16 = jnp.bfloat16

EPS = 1e-6
NEG_INF = -1e30
CHUNK = 64
WINDOW = 128
HEAD_DIM = 64
N_KV = 4
MAX_DISTANCE = 128
SSM_GROUP_CH = 16
MEM_HEADS = 4
CONV_W = 3

MIB = 1024 * 1024
LANES = 128
MXU_WIDTH = 256
SUBLANES = 8
SCAN_CARRY_ELEMS = 4096


def _params(semantics, vmem_mib):
    return pltpu.CompilerParams(dimension_semantics=semantics, vmem_limit_bytes=vmem_mib * MIB)


def _rms(x, g):
    y = x * lax.rsqrt(jnp.mean(x * x, axis=-1, keepdims=True) + EPS)
    return y * g


def _dot(a, b):
    return jnp.dot(a, b, preferred_element_type=F32)


def _dot_t(a, b):
    return lax.dot_general(a, b, (((1,), (1,)), ((), ())), preferred_element_type=F32)


def _proj_kernel(x_ref, g_ref, w_ref, *o_refs):
    h = _rms(x_ref[...], g_ref[...]).astype(BF16)
    r = _dot(h, w_ref[...])
    off = 0
    for o_ref in o_refs:
        w = o_ref.shape[1]
        o_ref[...] = r[:, off:off + w].astype(o_ref.dtype)
        off += w


def _norm_proj(x2d, g, w_bf, segments, tm):
    m, d = x2d.shape
    n = w_bf.shape[1]
    assert sum(w for w, _ in segments) == n
    return pl.pallas_call(
        _proj_kernel,
        grid=(m // tm,),
        in_specs=[pl.BlockSpec((tm, d), lambda i: (i, 0)),
                  pl.BlockSpec((1, d), lambda i: (0, 0)),
                  pl.BlockSpec((d, n), lambda i: (0, 0))],
        out_specs=[pl.BlockSpec((tm, w), lambda i: (i, 0)) for w, _ in segments],
        out_shape=[jax.ShapeDtypeStruct((m, w), dt) for w, dt in segments],
        compiler_params=_params(("parallel",), 48),
        name="proj",
    )(x2d, g.reshape(1, d), w_bf)


def _bias_kernel(idx_ref, table_ref, o_ref, *, n_buckets, n_heads):
    idx = idx_ref[...]
    for h in range(n_heads):
        acc = jnp.zeros(idx.shape, F32)
        for b in range(n_buckets):
            acc = jnp.where(idx == b, table_ref[b, h], acc)
        o_ref[h] = acc


def _t5_bucket(rel, n_buckets):
    half = n_buckets // 2
    max_exact = half // 2
    n = jnp.abs(rel)
    large = max_exact + (jnp.log(jnp.maximum(n, 1).astype(F32) / max_exact)
                         / math.log(MAX_DISTANCE / max_exact) * (half - max_exact)).astype(jnp.int32)
    large = jnp.minimum(large, half - 1)
    return jnp.where(rel > 0, half, 0) + jnp.where(n < max_exact, n, large)


def _rel_bias(table):
    n_buckets, n_heads = table.shape
    lk = WINDOW + CHUNK
    rel = jnp.arange(lk)[None, :] - WINDOW - jnp.arange(CHUNK)[:, None]
    idx = _t5_bucket(rel, n_buckets).astype(jnp.int32)
    return pl.pallas_call(
        functools.partial(_bias_kernel, n_buckets=n_buckets, n_heads=n_heads),
        in_specs=[pl.BlockSpec((CHUNK, lk), lambda: (0, 0)),
                  pl.BlockSpec(memory_space=pltpu.SMEM)],
        out_specs=pl.BlockSpec((n_heads, CHUNK, lk), lambda: (0, 0, 0)),
        out_shape=jax.ShapeDtypeStruct((n_heads, CHUNK, lk), F32),
        name="bias",
    )(idx, table.astype(F32))


def _attn_kernel(q_ref, k_ref, v_ref, hk_ref, hv_ref, bias_ref, o_ref, kf_scr, vf_scr, *, n_masked, grp):
    nb, t, _ = q_ref.shape
    kv_w = k_ref.shape[2]
    n_chunks = t // CHUNK
    lk = WINDOW + CHUNK
    lkp = bias_ref.shape[2]
    pair_w = 2 * HEAD_DIM
    scale = HEAD_DIM ** -0.5
    low_half = lax.broadcasted_iota(jnp.int32, (1, pair_w), 1) < HEAD_DIM
    ones = jnp.ones((lkp, pair_w), BF16)
    kv_pad = jnp.zeros((lkp - lk, kv_w), BF16)

    kf_scr[:, 0:WINDOW, :] = hk_ref[...].astype(BF16)
    kf_scr[:, WINDOW:, :] = k_ref[...].astype(BF16)
    vf_scr[:, 0:WINDOW, :] = hv_ref[...].astype(BF16)
    vf_scr[:, WINDOW:, :] = v_ref[...].astype(BF16)

    def chunk(n, carry):
        bi = n // n_chunks
        c = n % n_chunks
        r0 = pl.multiple_of(c * CHUNK, CHUNK)
        q = q_ref[bi, pl.ds(r0, CHUNK), :] * scale
        k = jnp.concatenate([kf_scr[bi, pl.ds(r0, lk), :], kv_pad], axis=0)
        v = jnp.concatenate([vf_scr[bi, pl.ds(r0, lk), :], kv_pad], axis=0)
        variant = jnp.minimum(c, n_masked) * N_KV
        scores = []
        for h in range(N_KV):
            j, e = divmod(h, 2)
            qp = jnp.concatenate([q[:, g * kv_w + j * pair_w:g * kv_w + (j + 1) * pair_w]
                                  for g in range(grp)], axis=0)
            in_head = low_half if e == 0 else jnp.logical_not(low_half)
            qh = jnp.where(in_head, qp, jnp.zeros_like(qp))
            scores.append(_dot_t(qh, k[:, j * pair_w:(j + 1) * pair_w]) + bias_ref[variant + h])
        probs = [jnp.exp(s - jnp.max(s, axis=-1, keepdims=True)).astype(BF16) for s in scores]
        v_ext = [jnp.concatenate([v[:, j * pair_w:(j + 1) * pair_w], ones], axis=1) for j in range(N_KV // 2)]
        sums = [_dot(p, v_ext[h // 2]) for h, p in enumerate(probs)]
        outs = [r[:, :pair_w] / r[:, pair_w:] for r in sums]
        for j in range(N_KV // 2):
            o_pair = jnp.where(low_half, outs[2 * j], outs[2 * j + 1])
            for g in range(grp):
                o_ref[bi, pl.ds(r0, CHUNK), g * kv_w + j * pair_w:g * kv_w + (j + 1) * pair_w] = (
                    o_pair[g * CHUNK:(g + 1) * CHUNK].astype(BF16))
        return carry

    lax.fori_loop(0, nb * n_chunks, chunk, 0, unroll=2)


def _band_attention(q3, kv3, hist_k, hist_v, bias, sinks, nb, first_valid):
    b, t, q_w = q3.shape
    kv_w = hist_k.shape[2]
    grp = q_w // HEAD_DIM // N_KV
    lk = WINDOW + CHUNK
    lkp = 2 * LANES
    n_masked = first_valid // CHUNK
    sink_col = jnp.broadcast_to(sinks.astype(F32).reshape(N_KV, grp, 1, 1), (N_KV, grp, CHUNK, 1))
    ext = jnp.concatenate([bias.reshape(N_KV, grp * CHUNK, lk), sink_col.reshape(N_KV, grp * CHUNK, 1),
                           jnp.full((N_KV, grp * CHUNK, lkp - lk - 1), NEG_INF, F32)], axis=2)
    key = jnp.arange(lkp)
    variants = [jnp.where(key < (n_masked - i) * CHUNK, NEG_INF, ext) for i in range(n_masked + 1)]
    bias_ext = jnp.concatenate(variants, axis=0)
    return pl.pallas_call(
        functools.partial(_attn_kernel, n_masked=n_masked, grp=grp),
        grid=(b // nb,),
        in_specs=[pl.BlockSpec((nb, t, q_w), lambda i: (i, 0, 0)),
                  pl.BlockSpec((nb, t, kv_w), lambda i: (i, 0, 0)),
                  pl.BlockSpec((nb, t, kv_w), lambda i: (i, 0, 1)),
                  pl.BlockSpec((nb, WINDOW, kv_w), lambda i: (i, 0, 0)),
                  pl.BlockSpec((nb, WINDOW, kv_w), lambda i: (i, 0, 0)),
                  pl.BlockSpec(bias_ext.shape, lambda i: (0, 0, 0))],
        out_specs=pl.BlockSpec((nb, t, q_w), lambda i: (i, 0, 0)),
        out_shape=jax.ShapeDtypeStruct((b, t, q_w), BF16),
        scratch_shapes=[pltpu.VMEM((nb, WINDOW + t, kv_w), BF16), pltpu.VMEM((nb, WINDOW + t, kv_w), BF16)],
        compiler_params=_params(("parallel",), 40),
        name="attn",
    )(q3, kv3, kv3, hist_k, hist_v, bias_ext)


def _gelu(x):
    return jax.nn.gelu(x)


def _s5_kernel(u_ref, s0re_ref, s0im_ref, are_ref, aim_ref, wbre_ref, wbim_ref, wcre_ref, wcim_ref,
               d_ref, wglu_ref, o_ref, sre_out_ref, sim_out_ref,
               sre_scr, sim_scr, st_re, st_im, y_scr, *, batch, steps, lane_chunk):
    i = pl.program_id(0)

    @pl.when(i == 0)
    def _():
        st_re[...] = s0re_ref[...]
        st_im[...] = s0im_ref[...]

    u = u_ref[...]
    ub = u.astype(BF16)
    n_in_blk, in_blk, st_blk = wbre_ref.shape
    for r in range(n_in_blk):
        blk = ub[:, r * in_blk:(r + 1) * in_blk]
        sre_scr[:, r * st_blk:(r + 1) * st_blk] = _dot(blk, wbre_ref[r])
        sim_scr[:, r * st_blk:(r + 1) * st_blk] = _dot(blk, wbim_ref[r])

    n_state = sre_scr.shape[1]
    for lc in range(n_state // lane_chunk):
        sl = slice(lc * lane_chunk, (lc + 1) * lane_chunk)
        ar = are_ref[:, sl]
        ai = aim_ref[:, sl]

        def body(t, carry, sl=sl, ar=ar, ai=ai):
            sr, si = carry
            rows = pl.ds(pl.multiple_of(t * batch, batch), batch)
            nr = ar * sr - ai * si + sre_scr[rows, sl]
            ni = ar * si + ai * sr + sim_scr[rows, sl]
            sre_scr[rows, sl] = nr
            sim_scr[rows, sl] = ni
            return nr, ni

        sr, si = lax.fori_loop(0, steps, body, (st_re[:, sl], st_im[:, sl]), unroll=4)
        st_re[:, sl] = sr
        st_im[:, sl] = si

    n_out_blk, k_blk, out_blk = wcre_ref.shape
    for kb in range(n_out_blk):
        ksl = slice(kb * k_blk, (kb + 1) * k_blk)
        y_scr[:, kb * out_blk:(kb + 1) * out_blk] = (
            _dot(sre_scr[:, ksl].astype(BF16), wcre_ref[kb])
            + _dot(sim_scr[:, ksl].astype(BF16), wcim_ref[kb]))
    y = _gelu(y_scr[...] + d_ref[...] * u)
    o_ref[...] = (y * jax.nn.sigmoid(_dot(y.astype(BF16), wglu_ref[...]))).astype(BF16)

    @pl.when(i == pl.num_programs(0) - 1)
    def _():
        sre_out_ref[...] = st_re[...]
        sim_out_ref[...] = st_im[...]


def _s5_params(lp):
    lam = lax.complex(lp['ssm_a_re'].astype(F32), lp['ssm_a_im'].astype(F32))
    dt = jnp.exp(lp['ssm_log_dt'].astype(F32))[:, None]
    a_bar = jnp.exp(lam * dt)
    b_mat = lax.complex(lp['ssm_b_re'].astype(F32), lp['ssm_b_im'].astype(F32))
    b_bar = ((a_bar - 1.0) / lam)[..., None] * b_mat
    g, p, c = b_bar.shape
    gb_in = 256 // c
    gb_out = 128 // c

    def b_blocks(x):
        x = x.reshape(g // gb_in, gb_in, p, c).transpose(0, 1, 3, 2)
        x = jnp.einsum('rgcp,gh->rgchp', x, jnp.eye(gb_in, dtype=F32))
        return x.reshape(g // gb_in, gb_in * c, gb_in * p).astype(BF16)

    def c_blocks(x):
        x = x.reshape(g // gb_out, gb_out, c, p).transpose(0, 1, 3, 2)
        x = jnp.einsum('kgpc,gh->kgphc', x, jnp.eye(gb_out, dtype=F32))
        return x.reshape(g // gb_out, gb_out * p, gb_out * c).astype(BF16)

    return dict(a_re=jnp.real(a_bar).reshape(1, g * p), a_im=jnp.imag(a_bar).reshape(1, g * p),
                wb_re=b_blocks(jnp.real(b_bar)), wb_im=b_blocks(jnp.imag(b_bar)),
                wc_re=c_blocks(lp['ssm_c_re'].astype(F32)), wc_im=c_blocks(-lp['ssm_c_im'].astype(F32)),
                d=lp['ssm_d'].astype(F32).reshape(1, -1), w_glu=lp['w_glu'].astype(BF16))


def _s5_layer(u_tb, s0_re, s0_im, sp, steps):
    t, b, ssm_w = u_tb.shape
    n_state = s0_re.shape[1]
    rows = steps * b
    lane_chunk = SCAN_CARRY_ELEMS // b
    a_re = jnp.broadcast_to(sp['a_re'], (b, n_state))
    a_im = jnp.broadcast_to(sp['a_im'], (b, n_state))

    def full(x):
        nd = x.ndim
        return pl.BlockSpec(x.shape, lambda i: (0,) * nd)

    consts = [s0_re, s0_im, a_re, a_im, sp['wb_re'], sp['wb_im'], sp['wc_re'], sp['wc_im'], sp['d'], sp['w_glu']]
    return pl.pallas_call(
        functools.partial(_s5_kernel, batch=b, steps=steps, lane_chunk=lane_chunk),
        grid=(t // steps,),
        in_specs=[pl.BlockSpec((rows, ssm_w), lambda i: (i, 0))] + [full(x) for x in consts],
        out_specs=[pl.BlockSpec((rows, ssm_w), lambda i: (i, 0)),
                   pl.BlockSpec((b, n_state), lambda i: (0, 0)),
                   pl.BlockSpec((b, n_state), lambda i: (0, 0))],
        out_shape=[jax.ShapeDtypeStruct((t * b, ssm_w), BF16),
                   jax.ShapeDtypeStruct((b, n_state), F32),
                   jax.ShapeDtypeStruct((b, n_state), F32)],
        scratch_shapes=[pltpu.VMEM((rows, n_state), F32), pltpu.VMEM((rows, n_state), F32),
                        pltpu.VMEM((b, n_state), F32), pltpu.VMEM((b, n_state), F32),
                        pltpu.VMEM((rows, ssm_w), F32)],
        compiler_params=_params(("arbitrary",), 48),
        name="s5",
    )(u_tb.reshape(t * b, ssm_w), *consts)


def _memattn_kernel(q_ref, k_ref, v_ref, o_ref, *, head_dim):
    q = q_ref[...]
    k = k_ref[...].astype(BF16)
    v = v_ref[...].astype(BF16)
    scale = head_dim ** -0.5
    for h in range(MEM_HEADS):
        sl = slice(h * head_dim, (h + 1) * head_dim)
        s = _dot_t(q[:, sl], k[:, sl]) * scale
        m = jnp.max(s, axis=-1, keepdims=True)
        e = jnp.exp(s - m)
        p = e / jnp.sum(e, axis=-1, keepdims=True)
        o_ref[:, sl] = _dot(p.astype(BF16), v[:, sl]).astype(BF16)


def _memory_attention(qm, mem_k, mem_v, b, t, tq):
    n_mem, mem_w = mem_k.shape[1:]
    nt = t // tq
    return pl.pallas_call(
        functools.partial(_memattn_kernel, head_dim=mem_w // MEM_HEADS),
        grid=(b, nt),
        in_specs=[pl.BlockSpec((tq, mem_w), lambda bi, ti: (bi * nt + ti, 0)),
                  pl.BlockSpec((None, n_mem, mem_w), lambda bi, ti: (bi, 0, 0)),
                  pl.BlockSpec((None, n_mem, mem_w), lambda bi, ti: (bi, 0, 0))],
        out_specs=pl.BlockSpec((tq, mem_w), lambda bi, ti: (bi * nt + ti, 0)),
        out_shape=jax.ShapeDtypeStruct((b * t, mem_w), BF16),
        compiler_params=_params(("parallel", "parallel"), 32),
        name="memattn",
    )(qm, mem_k, mem_v)


def _mix_kernel(x_ref, oa_ref, os_ref, om_ref, gpre_ref, wg0_ref, wg1_ref, wg2_ref,
                wa_ref, ws_ref, wm_ref, gpost_ref, o_ref, h_scr, *, tn):
    j = pl.program_id(1)

    @pl.when(j == 0)
    def _():
        h_scr[...] = _rms(x_ref[...], gpre_ref[...]).astype(BF16)

    h = h_scr[...]
    merged = (jax.nn.sigmoid(_dot(h, wg0_ref[...])) * _dot(oa_ref[...], wa_ref[...])
              + jax.nn.sigmoid(_dot(h, wg1_ref[...])) * _dot(os_ref[...], ws_ref[...])
              + jax.nn.sigmoid(_dot(h, wg2_ref[...])) * _dot(om_ref[...], wm_ref[...]))
    o_ref[:, pl.ds(pl.multiple_of(j * tn, tn), tn)] = merged

    @pl.when(j == pl.num_programs(1) - 1)
    def _():
        o_ref[...] = x_ref[...] + _rms(o_ref[...], gpost_ref[...])


def _mix_residual(x2d, o_a, o_s, o_m, g_pre, w_gates, w_oa, w_os, w_om, g_post, tm, tn):
    m, d = x2d.shape
    nj = d // tn

    def rows(w):
        return pl.BlockSpec((tm, w), lambda i, j: (i, 0))

    def gate_spec(br):
        return pl.BlockSpec((d, tn), lambda i, j: (0, br * nj + j))

    def wout_spec(w):
        return pl.BlockSpec((w.shape[0], tn), lambda i, j: (0, j))

    vec = pl.BlockSpec((1, d), lambda i, j: (0, 0))
    return pl.pallas_call(
        functools.partial(_mix_kernel, tn=tn),
        grid=(m // tm, nj),
        in_specs=[rows(d), rows(o_a.shape[1]), rows(o_s.shape[1]), rows(o_m.shape[1]), vec,
                  gate_spec(0), gate_spec(1), gate_spec(2),
                  wout_spec(w_oa), wout_spec(w_os), wout_spec(w_om), vec],
        out_specs=rows(d),
        out_shape=jax.ShapeDtypeStruct((m, d), F32),
        scratch_shapes=[pltpu.VMEM((tm, d), BF16)],
        compiler_params=_params(("parallel", "arbitrary"), 56),
        name="mix",
    )(x2d, o_a, o_s, o_m, g_pre.reshape(1, d), w_gates, w_gates, w_gates, w_oa, w_os, w_om,
      g_post.reshape(1, d))


def _ffn_kernel(x_ref, gpre_ref, wua_ref, wub_ref, cw_ref, cb_ref, wd_ref, gpost_ref, cprev_ref,
                o_ref, tail_ref, h_scr, acc_scr, a_scr, b_scr, carry_scr):
    i = pl.program_id(1)
    j = pl.program_id(2)
    nb, tt, d = x_ref.shape
    tf = wua_ref.shape[1]
    rows = nb * tt

    @pl.when(j == 0)
    def _():
        h_scr[...] = _rms(x_ref[...].reshape(rows, d), gpre_ref[...]).astype(BF16)
        acc_scr[...] = jnp.zeros_like(acc_scr)

    h = h_scr[...]
    a_scr[:, 0:SUBLANES, :] = jnp.where(i == 0, cprev_ref[...], carry_scr[j])
    n_slices = tf // MXU_WIDTH
    cols = [slice(s * MXU_WIDTH, (s + 1) * MXU_WIDTH) for s in range(n_slices)]
    for sl in cols:
        a_scr[:, SUBLANES:, sl] = _dot(h, wua_ref[:, sl]).reshape(nb, tt, MXU_WIDTH)
        b_scr[:, sl] = _dot(h, wub_ref[:, sl])
    cw = cw_ref[...]
    cb = cb_ref[...]
    for sl in cols:
        gated = []
        for bi in range(nb):
            a = a_scr[bi, SUBLANES:, sl]
            first = jnp.concatenate([a_scr[bi, 0:SUBLANES, sl], a[0:SUBLANES]], axis=0)
            prev1 = jnp.concatenate([first[SUBLANES - 1:2 * SUBLANES - 1],
                                     pltpu.roll(a, 1, 0)[SUBLANES:]], axis=0)
            prev2 = jnp.concatenate([first[SUBLANES - 2:2 * SUBLANES - 2],
                                     pltpu.roll(a, 2, 0)[SUBLANES:]], axis=0)
            conv = (prev2 * cw[0:1, sl] + prev1 * cw[1:2, sl] + a * cw[2:3, sl]) + cb[:, sl]
            gated.append(_gelu(conv) * b_scr[bi * tt:(bi + 1) * tt, sl])
        act = (gated[0] if nb == 1 else jnp.concatenate(gated, axis=0)).astype(BF16)
        acc_scr[...] += _dot(act, wd_ref[sl, :])
    tail = a_scr[:, tt:tt + SUBLANES, :]
    carry_scr[j] = tail
    tail_ref[...] = tail

    @pl.when(j == pl.num_programs(2) - 1)
    def _():
        f = _rms(acc_scr[...], gpost_ref[...]).reshape(nb, tt, d)
        o_ref[...] = x_ref[...] + f


def _conv_ffn(x3d, g_pre, w_up, conv_w, conv_b, w_down, g_post, conv_prev8, nb, tt, tf):
    b, t, d = x3d.shape
    d_ff = w_down.shape[0]
    nj = d_ff // tf
    nt = t // tt
    return pl.pallas_call(
        _ffn_kernel,
        grid=(b // nb, nt, nj),
        in_specs=[pl.BlockSpec((nb, tt, d), lambda bi, i, j: (bi, i, 0)),
                  pl.BlockSpec((1, d), lambda bi, i, j: (0, 0)),
                  pl.BlockSpec((d, tf), lambda bi, i, j: (0, j)),
                  pl.BlockSpec((d, tf), lambda bi, i, j: (0, nj + j)),
                  pl.BlockSpec((CONV_W, tf), lambda bi, i, j: (0, j)),
                  pl.BlockSpec((1, tf), lambda bi, i, j: (0, j)),
                  pl.BlockSpec((tf, d), lambda bi, i, j: (j, 0)),
                  pl.BlockSpec((1, d), lambda bi, i, j: (0, 0)),
                  pl.BlockSpec((nb, SUBLANES, tf), lambda bi, i, j: (bi, 0, j))],
        out_specs=[pl.BlockSpec((nb, tt, d), lambda bi, i, j: (bi, i, 0)),
                   pl.BlockSpec((nb, None, SUBLANES, tf), lambda bi, i, j: (bi, i, 0, j))],
        out_shape=[jax.ShapeDtypeStruct((b, t, d), F32),
                   jax.ShapeDtypeStruct((b, nt, SUBLANES, d_ff), F32)],
        scratch_shapes=[pltpu.VMEM((nb * tt, d), BF16), pltpu.VMEM((nb * tt, d), F32),
                        pltpu.VMEM((nb, tt + SUBLANES, tf), F32), pltpu.VMEM((nb * tt, tf), F32),
                        pltpu.VMEM((nj, nb, SUBLANES, tf), F32)],
        compiler_params=_params(("arbitrary", "arbitrary", "arbitrary"), 56),
        name="ffn",
    )(x3d, g_pre.reshape(1, d), w_up, w_up, conv_w.astype(F32), conv_b.astype(F32).reshape(1, d_ff),
      w_down, g_post.reshape(1, d), conv_prev8)


def _tiles(b, t):
    row_tile = 512
    tt = min(t, row_tile)
    return dict(
        proj_rows=256,
        attn_batch=max(1, min(b, 1024 // t)),
        s5_steps=min(t, row_tile // b),
        memattn_rows=tt,
        mix_rows=row_tile, mix_cols=512,
        ffn_batch=min(b, row_tile // tt), ffn_rows=tt, ffn_cols=2 * MXU_WIDTH)


def _layer(x, attn_past, s0, conv_prev, mem_k, mem_v, bias, lw):
    b, t, d = x.shape
    m = b * t
    q_w, kv_w, ssm_w, mem_w = lw['q_w'], lw['kv_w'], lw['ssm_w'], lw['mem_w']
    n_state = lw['n_state']
    d_ff = lw['w_down'].shape[0]
    tiles = _tiles(b, t)
    x2d = x.reshape(m, d)

    q, kv, u, qm = _norm_proj(x2d, lw['norm_pre_mix'], lw['w_proj'],
                              [(q_w, BF16), (2 * kv_w, F32), (ssm_w, F32), (mem_w, BF16)], tiles['proj_rows'])
    kv3 = kv.reshape(b, t, 2 * kv_w)
    if attn_past is None:
        hist_k = jnp.zeros((b, WINDOW, kv_w), F32)
        hist_v = hist_k
        first_valid = WINDOW
    else:
        hist_k = attn_past[0].astype(F32).reshape(b, WINDOW, kv_w)
        hist_v = attn_past[1].astype(F32).reshape(b, WINDOW, kv_w)
        first_valid = 0
    o_a = _band_attention(q.reshape(b, t, q_w), kv3, hist_k, hist_v, bias, lw['attn_sinks'],
                          tiles['attn_batch'], first_valid).reshape(m, q_w)

    if s0 is None:
        s0_re = jnp.zeros((b, n_state), F32)
        s0_im = s0_re
    else:
        s0_re = s0[0].astype(F32).reshape(b, n_state)
        s0_im = s0[1].astype(F32).reshape(b, n_state)
    u_tb = u.reshape(b, t, ssm_w).transpose(1, 0, 2)
    o_s_tb, s_re, s_im = _s5_layer(u_tb, s0_re, s0_im, lw['s5'], tiles['s5_steps'])
    o_s = o_s_tb.reshape(t, b, ssm_w).transpose(1, 0, 2).reshape(m, ssm_w)

    o_m = _memory_attention(qm, mem_k, mem_v, b, t, tiles['memattn_rows'])

    x1 = _mix_residual(x2d, o_a, o_s, o_m, lw['norm_pre_mix'], lw['w_gates'], lw['w_oa'],
                       lw['w_os'], lw['w_om'], lw['norm_post_mix'], tiles['mix_rows'], tiles['mix_cols'])

    if conv_prev is None:
        conv_prev8 = jnp.zeros((b, SUBLANES, d_ff), F32)
    else:
        conv_prev8 = jnp.pad(conv_prev.astype(F32), ((0, 0), (SUBLANES - (CONV_W - 1), 0), (0, 0)))
    x2, tails = _conv_ffn(x1.reshape(b, t, d), lw['norm_pre_ffn'], lw['w_up'], lw['conv_w'], lw['conv_b'],
                          lw['w_down'], lw['norm_post_ffn'], conv_prev8,
                          tiles['ffn_batch'], tiles['ffn_rows'], tiles['ffn_cols'])
    conv_new = tails[:, -1, SUBLANES - (CONV_W - 1):, :]

    n_kv = kv_w // HEAD_DIM
    k_new = jnp.concatenate([hist_k, kv3[:, :, :kv_w]], axis=1)[:, -WINDOW:].reshape(b, WINDOW, n_kv, HEAD_DIM)
    v_new = jnp.concatenate([hist_v, kv3[:, :, kv_w:]], axis=1)[:, -WINDOW:].reshape(b, WINDOW, n_kv, HEAD_DIM)
    return x2, k_new, v_new, s_re, s_im, conv_new


def kernel(x_prompt, x_sample, cache_attn_k, cache_attn_v, cache_mem_k, cache_mem_v, state_ssm_re, state_ssm_im, state_conv, mem_prompt, rel_bias_table, norm_pre_mix, norm_post_mix, norm_pre_ffn, norm_post_ffn, norm_mem, w_in, attn_sinks, ssm_a_re, ssm_a_im, ssm_log_dt, ssm_b_re, ssm_b_im, ssm_c_re, ssm_c_im, ssm_d, w_glu, w_mem_kv, w_out, w_up, conv_w, conv_b, w_down):
    depth = w_in.shape[0]
    bp, _, d = x_prompt.shape
    n_mem = mem_prompt.shape[1]
    n_q = attn_sinks.shape[1]
    n_kv, hd = cache_attn_k.shape[-2:]
    assert hd == HEAD_DIM and n_kv == N_KV and cache_attn_k.shape[2] == WINDOW
    groups, p_state = ssm_a_re.shape[1:]
    q_w, kv_w = n_q * HEAD_DIM, n_kv * HEAD_DIM
    ssm_w = ssm_d.shape[1]
    mem_w = w_mem_kv.shape[2] // 2
    proj_w = q_w + 2 * kv_w + ssm_w + mem_w
    mem_hd = mem_w // MEM_HEADS

    xp, xs = x_prompt, x_sample
    outs = [[] for _ in range(12)]
    for l in range(depth):
        bias = _rel_bias(rel_bias_table)
        lp = dict(ssm_a_re=ssm_a_re[l], ssm_a_im=ssm_a_im[l], ssm_log_dt=ssm_log_dt[l],
                  ssm_b_re=ssm_b_re[l], ssm_b_im=ssm_b_im[l], ssm_c_re=ssm_c_re[l], ssm_c_im=ssm_c_im[l],
                  ssm_d=ssm_d[l], w_glu=w_glu[l])
        w_in_l = w_in[l]
        w_out_l = w_out[l].astype(BF16)
        grp = n_q // n_kv
        w_q = w_in_l[:, :q_w].reshape(d, n_kv, grp, HEAD_DIM).transpose(0, 2, 1, 3).reshape(d, q_w)
        w_proj = jnp.concatenate([w_q, w_in_l[:, q_w:proj_w]], axis=1).astype(BF16)
        w_oa = w_out_l[:q_w].reshape(n_kv, grp, HEAD_DIM, d).transpose(1, 0, 2, 3).reshape(q_w, d)
        lw = dict(q_w=q_w, kv_w=kv_w, ssm_w=ssm_w, mem_w=mem_w, n_state=groups * p_state,
                  norm_pre_mix=norm_pre_mix[l], norm_post_mix=norm_post_mix[l],
                  norm_pre_ffn=norm_pre_ffn[l], norm_post_ffn=norm_post_ffn[l],
                  w_proj=w_proj, w_gates=w_in_l[:, proj_w:].astype(BF16),
                  w_oa=w_oa, w_os=w_out_l[q_w:q_w + ssm_w], w_om=w_out_l[q_w + ssm_w:],
                  attn_sinks=attn_sinks[l], s5=_s5_params(lp),
                  w_up=w_up[l].astype(BF16), conv_w=conv_w[l], conv_b=conv_b[l], w_down=w_down[l].astype(BF16))

        mk_p, mv_p = _norm_proj(mem_prompt.reshape(bp * n_mem, d), norm_mem[l], w_mem_kv[l].astype(BF16),
                                [(mem_w, F32), (mem_w, F32)], _tiles(bp, n_mem)['proj_rows'])
        mk_p = mk_p.reshape(bp, n_mem, mem_w)
        mv_p = mv_p.reshape(bp, n_mem, mem_w)
        xp, k_p, v_p, sr_p, si_p, c_p = _layer(xp, None, None, None, mk_p, mv_p, bias, lw)

        bs = xs.shape[0]
        xs, k_s, v_s, sr_s, si_s, c_s = _layer(
            xs, (cache_attn_k[l], cache_attn_v[l]), (state_ssm_re[l], state_ssm_im[l]), state_conv[l],
            cache_mem_k[l].reshape(bs, n_mem, mem_w), cache_mem_v[l].reshape(bs, n_mem, mem_w), bias, lw)

        vals = (k_p, v_p, sr_p.reshape(bp, groups, p_state), si_p.reshape(bp, groups, p_state), c_p,
                mk_p.reshape(bp, n_mem, MEM_HEADS, mem_hd), mv_p.reshape(bp, n_mem, MEM_HEADS, mem_hd),
                k_s, v_s, sr_s.reshape(bs, groups, p_state), si_s.reshape(bs, groups, p_state), c_s)
        for acc, val in zip(outs, vals):
            acc.append(val)
    return (xp, xs) + tuple(jnp.stack(o) for o in outs)
```

```python
import functools
import math

import jax
import jax.numpy as jnp
from jax import lax
from jax.experimental import pallas as pl
from jax.experimental.pallas import tpu as pltpu

F32 = jnp.float32
BF16 = jnp.bfloat16

EPS = 1e-6
NEG_INF = -1e30
CHUNK = 64
WINDOW = 128
HEAD_DIM = 64
N_KV = 4
MAX_DISTANCE = 128
SSM_GROUP_CH = 16
MEM_HEADS = 4
CONV_W = 3

MIB = 1024 * 1024
LANES = 128
MXU_WIDTH = 256
SUBLANES = 8
SCAN_CARRY_ELEMS = 4096


def _params(semantics, vmem_mib):
    return pltpu.CompilerParams(dimension_semantics=semantics, vmem_limit_bytes=vmem_mib * MIB)


def _rms(x, g):
    y = x * lax.rsqrt(jnp.mean(x * x, axis=-1, keepdims=True) + EPS)
    return y * g


def _dot(a, b):
    return jnp.dot(a, b, preferred_element_type=F32)


def _dot_t(a, b):
    return lax.dot_general(a, b, (((1,), (1,)), ((), ())), preferred_element_type=F32)


def _proj_kernel(x_ref, g_ref, *refs, blocks_per_out):
    n_w = sum(blocks_per_out)
    w_refs, o_refs = refs[:n_w], refs[n_w:]
    h = _rms(x_ref[...], g_ref[...]).astype(BF16)
    first = 0
    for o_ref, count in zip(o_refs, blocks_per_out):
        parts = [_dot(h, w_ref[...]) for w_ref in w_refs[first:first + count]]
        r = parts[0] if count == 1 else jnp.concatenate(parts, axis=1)
        o_ref[...] = r.astype(o_ref.dtype)
        first += count


def _col_blocks(arr, start, width):
    blk = math.gcd(start, width) if start else width
    assert blk % LANES == 0
    return [(arr, blk, start // blk + k) for k in range(width // blk)]


def _norm_proj(x2d, g, outputs, tm):
    m, d = x2d.shape
    w_args, w_specs, out_specs, out_shapes = [], [], [], []
    for dt, blocks in outputs:
        for arr, width, blk in blocks:
            assert arr.shape[0] == d and arr.shape[1] % width == 0
            w_args.append(arr)
            w_specs.append(pl.BlockSpec((d, width), lambda i, blk=blk: (0, blk)))
        total = sum(width for _, width, _ in blocks)
        out_specs.append(pl.BlockSpec((tm, total), lambda i: (i, 0)))
        out_shapes.append(jax.ShapeDtypeStruct((m, total), dt))
    return pl.pallas_call(
        functools.partial(_proj_kernel, blocks_per_out=tuple(len(b) for _, b in outputs)),
        grid=(m // tm,),
        in_specs=[pl.BlockSpec((tm, d), lambda i: (i, 0)),
                  pl.BlockSpec((1, d), lambda i: (0, 0))] + w_specs,
        out_specs=out_specs,
        out_shape=out_shapes,
        compiler_params=_params(("parallel",), 48),
        name="proj",
    )(x2d, g.reshape(1, d), *w_args)


def _bias_kernel(idx_ref, table_ref, o_ref, *, n_buckets, n_heads):
    idx = idx_ref[...]
    for h in range(n_heads):
        acc = jnp.zeros(idx.shape, F32)
        for b in range(n_buckets):
            acc = jnp.where(idx == b, table_ref[b, h], acc)
        o_ref[h] = acc


def _t5_bucket(rel, n_buckets):
    half = n_buckets // 2
    max_exact = half // 2
    n = jnp.abs(rel)
    large = max_exact + (jnp.log(jnp.maximum(n, 1).astype(F32) / max_exact)
                         / math.log(MAX_DISTANCE / max_exact) * (half - max_exact)).astype(jnp.int32)
    large = jnp.minimum(large, half - 1)
    return jnp.where(rel > 0, half, 0) + jnp.where(n < max_exact, n, large)


def _rel_bias(table):
    n_buckets, n_heads = table.shape
    lk = WINDOW + CHUNK
    rel = jnp.arange(lk)[None, :] - WINDOW - jnp.arange(CHUNK)[:, None]
    idx = _t5_bucket(rel, n_buckets).astype(jnp.int32)
    return pl.pallas_call(
        functools.partial(_bias_kernel, n_buckets=n_buckets, n_heads=n_heads),
        in_specs=[pl.BlockSpec((CHUNK, lk), lambda: (0, 0)),
                  pl.BlockSpec(memory_space=pltpu.SMEM)],
        out_specs=pl.BlockSpec((n_heads, CHUNK, lk), lambda: (0, 0, 0)),
        out_shape=jax.ShapeDtypeStruct((n_heads, CHUNK, lk), F32),
        name="bias",
    )(idx, table.astype(F32))


def _attn_kernel(q_ref, k_ref, v_ref, hk_ref, hv_ref, bias_ref, o_ref, kf_scr, vf_scr, *, n_masked, grp):
    nb, t, _ = q_ref.shape
    kv_w = k_ref.shape[2]
    n_chunks = t // CHUNK
    lk = WINDOW + CHUNK
    lkp = bias_ref.shape[2]
    pair_w = 2 * HEAD_DIM
    scale = HEAD_DIM ** -0.5
    low_half = lax.broadcasted_iota(jnp.int32, (1, pair_w), 1) < HEAD_DIM
    ones = jnp.ones((lkp, pair_w), BF16)
    kv_pad = jnp.zeros((lkp - lk, kv_w), BF16)

    kf_scr[:, 0:WINDOW, :] = hk_ref[...].astype(BF16)
    kf_scr[:, WINDOW:, :] = k_ref[...].astype(BF16)
    vf_scr[:, 0:WINDOW, :] = hv_ref[...].astype(BF16)
    vf_scr[:, WINDOW:, :] = v_ref[...].astype(BF16)

    def chunk(n, carry):
        bi = n // n_chunks
        c = n % n_chunks
        r0 = pl.multiple_of(c * CHUNK, CHUNK)
        q = q_ref[bi, pl.ds(r0, CHUNK), :] * scale
        k = jnp.concatenate([kf_scr[bi, pl.ds(r0, lk), :], kv_pad], axis=0)
        v = jnp.concatenate([vf_scr[bi, pl.ds(r0, lk), :], kv_pad], axis=0)
        variant = jnp.minimum(c, n_masked) * N_KV
        scores = []
        for h in range(N_KV):
            j, e = divmod(h, 2)
            qp = jnp.concatenate([q[:, g * kv_w + j * pair_w:g * kv_w + (j + 1) * pair_w]
                                  for g in range(grp)], axis=0)
            in_head = low_half if e == 0 else jnp.logical_not(low_half)
            qh = jnp.where(in_head, qp, jnp.zeros_like(qp))
            scores.append(_dot_t(qh, k[:, j * pair_w:(j + 1) * pair_w]) + bias_ref[variant + h])
        probs = [jnp.exp(s - jnp.max(s, axis=-1, keepdims=True)).astype(BF16) for s in scores]
        v_ext = [jnp.concatenate([v[:, j * pair_w:(j + 1) * pair_w], ones], axis=1) for j in range(N_KV // 2)]
        sums = [_dot(p, v_ext[h // 2]) for h, p in enumerate(probs)]
        outs = [r[:, :pair_w] / r[:, pair_w:] for r in sums]
        for j in range(N_KV // 2):
            o_pair = jnp.where(low_half, outs[2 * j], outs[2 * j + 1])
            for g in range(grp):
                o_ref[bi, pl.ds(r0, CHUNK), g * kv_w + j * pair_w:g * kv_w + (j + 1) * pair_w] = (
                    o_pair[g * CHUNK:(g + 1) * CHUNK].astype(BF16))
        return carry

    lax.fori_loop(0, nb * n_chunks, chunk, 0, unroll=2)


def _band_attention(q3, kv3, hist_k, hist_v, bias, sinks, nb, first_valid):
    b, t, q_w = q3.shape
    kv_w = hist_k.shape[2]
    grp = q_w // HEAD_DIM // N_KV
    lk = WINDOW + CHUNK
    lkp = 2 * LANES
    n_masked = first_valid // CHUNK
    sink_col = jnp.broadcast_to(sinks.astype(F32).reshape(N_KV, grp, 1, 1), (N_KV, grp, CHUNK, 1))
    ext = jnp.concatenate([bias.reshape(N_KV, grp * CHUNK, lk), sink_col.reshape(N_KV, grp * CHUNK, 1),
                           jnp.full((N_KV, grp * CHUNK, lkp - lk - 1), NEG_INF, F32)], axis=2)
    key = jnp.arange(lkp)
    variants = [jnp.where(key < (n_masked - i) * CHUNK, NEG_INF, ext) for i in range(n_masked + 1)]
    bias_ext = jnp.concatenate(variants, axis=0)
    return pl.pallas_call(
        functools.partial(_attn_kernel, n_masked=n_masked, grp=grp),
        grid=(b // nb,),
        in_specs=[pl.BlockSpec((nb, t, q_w), lambda i: (i, 0, 0)),
                  pl.BlockSpec((nb, t, kv_w), lambda i: (i, 0, 0)),
                  pl.BlockSpec((nb, t, kv_w), lambda i: (i, 0, 1)),
                  pl.BlockSpec((nb, WINDOW, kv_w), lambda i: (i, 0, 0)),
                  pl.BlockSpec((nb, WINDOW, kv_w), lambda i: (i, 0, 0)),
                  pl.BlockSpec(bias_ext.shape, lambda i: (0, 0, 0))],
        out_specs=pl.BlockSpec((nb, t, q_w), lambda i: (i, 0, 0)),
        out_shape=jax.ShapeDtypeStruct((b, t, q_w), BF16),
        scratch_shapes=[pltpu.VMEM((nb, WINDOW + t, kv_w), BF16), pltpu.VMEM((nb, WINDOW + t, kv_w), BF16)],
        compiler_params=_params(("parallel",), 40),
        name="attn",
    )(q3, kv3, kv3, hist_k, hist_v, bias_ext)


def _gelu(x):
    return jax.nn.gelu(x)


def _s5_kernel(u_ref, s0re_ref, s0im_ref, are_ref, aim_ref, wbre_ref, wbim_ref, wcre_ref, wcim_ref,
               d_ref, wglu_ref, o_ref, sre_out_ref, sim_out_ref,
               sre_scr, sim_scr, st_re, st_im, y_scr, *, batch, steps, lane_chunk):
    i = pl.program_id(0)

    @pl.when(i == 0)
    def _():
        st_re[...] = s0re_ref[...]
        st_im[...] = s0im_ref[...]

    u = u_ref[...]
    ub = u.astype(BF16)
    n_in_blk, in_blk, st_blk = wbre_ref.shape
    for r in range(n_in_blk):
        blk = ub[:, r * in_blk:(r + 1) * in_blk]
        sre_scr[:, r * st_blk:(r + 1) * st_blk] = _dot(blk, wbre_ref[r])
        sim_scr[:, r * st_blk:(r + 1) * st_blk] = _dot(blk, wbim_ref[r])

    n_state = sre_scr.shape[1]
    for lc in range(n_state // lane_chunk):
        sl = slice(lc * lane_chunk, (lc + 1) * lane_chunk)
        ar = are_ref[:, sl]
        ai = aim_ref[:, sl]

        def body(t, carry, sl=sl, ar=ar, ai=ai):
            sr, si = carry
            rows = pl.ds(pl.multiple_of(t * batch, batch), batch)
            nr = ar * sr - ai * si + sre_scr[rows, sl]
            ni = ar * si + ai * sr + sim_scr[rows, sl]
            sre_scr[rows, sl] = nr
            sim_scr[rows, sl] = ni
            return nr, ni

        sr, si = lax.fori_loop(0, steps, body, (st_re[:, sl], st_im[:, sl]), unroll=4)
        st_re[:, sl] = sr
        st_im[:, sl] = si

    n_out_blk, k_blk, out_blk = wcre_ref.shape
    for kb in range(n_out_blk):
        ksl = slice(kb * k_blk, (kb + 1) * k_blk)
        y_scr[:, kb * out_blk:(kb + 1) * out_blk] = (
            _dot(sre_scr[:, ksl].astype(BF16), wcre_ref[kb])
            + _dot(sim_scr[:, ksl].astype(BF16), wcim_ref[kb]))
    y = _gelu(y_scr[...] + d_ref[...] * u)
    o_ref[...] = (y * jax.nn.sigmoid(_dot(y.astype(BF16), wglu_ref[...]))).astype(BF16)

    @pl.when(i == pl.num_programs(0) - 1)
    def _():
        sre_out_ref[...] = st_re[...]
        sim_out_ref[...] = st_im[...]


def _s5_params(lp):
    lam = lax.complex(lp['ssm_a_re'].astype(F32), lp['ssm_a_im'].astype(F32))
    dt = jnp.exp(lp['ssm_log_dt'].astype(F32))[:, None]
    a_bar = jnp.exp(lam * dt)
    b_mat = lax.complex(lp['ssm_b_re'].astype(F32), lp['ssm_b_im'].astype(F32))
    b_bar = ((a_bar - 1.0) / lam)[..., None] * b_mat
    g, p, c = b_bar.shape
    gb_in = 256 // c
    gb_out = 128 // c

    def b_blocks(x):
        x = x.reshape(g // gb_in, gb_in, p, c).transpose(0, 1, 3, 2)
        x = jnp.einsum('rgcp,gh->rgchp', x, jnp.eye(gb_in, dtype=F32))
        return x.reshape(g // gb_in, gb_in * c, gb_in * p).astype(BF16)

    def c_blocks(x):
        x = x.reshape(g // gb_out, gb_out, c, p).transpose(0, 1, 3, 2)
        x = jnp.einsum('kgpc,gh->kgphc', x, jnp.eye(gb_out, dtype=F32))
        return x.reshape(g // gb_out, gb_out * p, gb_out * c).astype(BF16)

    return dict(a_re=jnp.real(a_bar).reshape(1, g * p), a_im=jnp.imag(a_bar).reshape(1, g * p),
                wb_re=b_blocks(jnp.real(b_bar)), wb_im=b_blocks(jnp.imag(b_bar)),
                wc_re=c_blocks(lp['ssm_c_re'].astype(F32)), wc_im=c_blocks(-lp['ssm_c_im'].astype(F32)),
                d=lp['ssm_d'].astype(F32).reshape(1, -1), w_glu=lp['w_glu'].astype(BF16))


def _s5_layer(u_tb, s0_re, s0_im, sp, steps):
    t, b, ssm_w = u_tb.shape
    n_state = s0_re.shape[1]
    rows = steps * b
    lane_chunk = SCAN_CARRY_ELEMS // b
    a_re = jnp.broadcast_to(sp['a_re'], (b, n_state))
    a_im = jnp.broadcast_to(sp['a_im'], (b, n_state))

    def full(x):
        nd = x.ndim
        return pl.BlockSpec(x.shape, lambda i: (0,) * nd)

    consts = [s0_re, s0_im, a_re, a_im, sp['wb_re'], sp['wb_im'], sp['wc_re'], sp['wc_im'], sp['d'], sp['w_glu']]
    return pl.pallas_call(
        functools.partial(_s5_kernel, batch=b, steps=steps, lane_chunk=lane_chunk),
        grid=(t // steps,),
        in_specs=[pl.BlockSpec((rows, ssm_w), lambda i: (i, 0))] + [full(x) for x in consts],
        out_specs=[pl.BlockSpec((rows, ssm_w), lambda i: (i, 0)),
                   pl.BlockSpec((b, n_state), lambda i: (0, 0)),
                   pl.BlockSpec((b, n_state), lambda i: (0, 0))],
        out_shape=[jax.ShapeDtypeStruct((t * b, ssm_w), BF16),
                   jax.ShapeDtypeStruct((b, n_state), F32),
                   jax.ShapeDtypeStruct((b, n_state), F32)],
        scratch_shapes=[pltpu.VMEM((rows, n_state), F32), pltpu.VMEM((rows, n_state), F32),
                        pltpu.VMEM((b, n_state), F32), pltpu.VMEM((b, n_state), F32),
                        pltpu.VMEM((rows, ssm_w), F32)],
        compiler_params=_params(("arbitrary",), 48),
        name="s5",
    )(u_tb.reshape(t * b, ssm_w), *consts)


def _memattn_kernel(q_ref, k_ref, v_ref, o_ref, *, head_dim):
    q = q_ref[...]
    k = k_ref[...].astype(BF16)
    v = v_ref[...].astype(BF16)
    scale = head_dim ** -0.5
    for h in range(MEM_HEADS):
        sl = slice(h * head_dim, (h + 1) * head_dim)
        s = _dot_t(q[:, sl], k[:, sl]) * scale
        m = jnp.max(s, axis=-1, keepdims=True)
        e = jnp.exp(s - m)
        p = e / jnp.sum(e, axis=-1, keepdims=True)
        o_ref[:, sl] = _dot(p.astype(BF16), v[:, sl]).astype(BF16)


def _memory_attention(qm, mem_k, mem_v, b, t, tq):
    n_mem, mem_w = mem_k.shape[1:]
    nt = t // tq
    return pl.pallas_call(
        functools.partial(_memattn_kernel, head_dim=mem_w // MEM_HEADS),
        grid=(b, nt),
        in_specs=[pl.BlockSpec((tq, mem_w), lambda bi, ti: (bi * nt + ti, 0)),
                  pl.BlockSpec((None, n_mem, mem_w), lambda bi, ti: (bi, 0, 0)),
                  pl.BlockSpec((None, n_mem, mem_w), lambda bi, ti: (bi, 0, 0))],
        out_specs=pl.BlockSpec((tq, mem_w), lambda bi, ti: (bi * nt + ti, 0)),
        out_shape=jax.ShapeDtypeStruct((b * t, mem_w), BF16),
        compiler_params=_params(("parallel", "parallel"), 32),
        name="memattn",
    )(qm, mem_k, mem_v)


def _mix_kernel(x_ref, oa_ref, os_ref, om_ref, gpre_ref, wg0_ref, wg1_ref, wg2_ref,
                wa_ref, ws_ref, wm_ref, gpost_ref, o_ref, h_scr, *, tn):
    j = pl.program_id(1)

    @pl.when(j == 0)
    def _():
        h_scr[...] = _rms(x_ref[...], gpre_ref[...]).astype(BF16)

    h = h_scr[...]
    merged = (jax.nn.sigmoid(_dot(h, wg0_ref[...])) * _dot(oa_ref[...], wa_ref[...])
              + jax.nn.sigmoid(_dot(h, wg1_ref[...])) * _dot(os_ref[...], ws_ref[...])
              + jax.nn.sigmoid(_dot(h, wg2_ref[...])) * _dot(om_ref[...], wm_ref[...]))
    o_ref[:, pl.ds(pl.multiple_of(j * tn, tn), tn)] = merged

    @pl.when(j == pl.num_programs(1) - 1)
    def _():
        o_ref[...] = x_ref[...] + _rms(o_ref[...], gpost_ref[...])


def _mix_residual(x2d, o_a, o_s, o_m, g_pre, w_in_bf, gate_col0, w_oa, w_out_bf, g_post, tm, tn):
    m, d = x2d.shape
    nj = d // tn
    q_w, ssm_w, mem_w = o_a.shape[1], o_s.shape[1], o_m.shape[1]
    assert gate_col0 % tn == 0 and q_w % ssm_w == 0 and (q_w + ssm_w) % mem_w == 0
    gate_blk0 = gate_col0 // tn

    def rows(w):
        return pl.BlockSpec((tm, w), lambda i, j: (i, 0))

    def gate_spec(br):
        return pl.BlockSpec((d, tn), lambda i, j: (0, gate_blk0 + br * nj + j))

    vec = pl.BlockSpec((1, d), lambda i, j: (0, 0))
    return pl.pallas_call(
        functools.partial(_mix_kernel, tn=tn),
        grid=(m // tm, nj),
        in_specs=[rows(d), rows(q_w), rows(ssm_w), rows(mem_w), vec,
                  gate_spec(0), gate_spec(1), gate_spec(2),
                  pl.BlockSpec((q_w, tn), lambda i, j: (0, j)),
                  pl.BlockSpec((ssm_w, tn), lambda i, j: (q_w // ssm_w, j)),
                  pl.BlockSpec((mem_w, tn), lambda i, j: ((q_w + ssm_w) // mem_w, j)), vec],
        out_specs=rows(d),
        out_shape=jax.ShapeDtypeStruct((m, d), F32),
        scratch_shapes=[pltpu.VMEM((tm, d), BF16)],
        compiler_params=_params(("parallel", "arbitrary"), 56),
        name="mix",
    )(x2d, o_a, o_s, o_m, g_pre.reshape(1, d), w_in_bf, w_in_bf, w_in_bf, w_oa, w_out_bf, w_out_bf,
      g_post.reshape(1, d))


def _ffn_kernel(x_ref, gpre_ref, wu_ref, cw_ref, cb_ref, wd_ref, gpost_ref, cprev_ref,
                o_ref, tail_ref, h_scr, acc_scr, a_scr, b_scr, carry_scr):
    i = pl.program_id(1)
    j = pl.program_id(2)
    nb, tt, d = x_ref.shape
    tf = wd_ref.shape[0]
    rows = nb * tt
    ff_cols = pl.ds(pl.multiple_of(j * tf, tf), tf)

    @pl.when(j == 0)
    def _():
        h_scr[...] = _rms(x_ref[...].reshape(rows, d), gpre_ref[...]).astype(BF16)
        acc_scr[...] = jnp.zeros_like(acc_scr)

    h = h_scr[...]
    a_scr[:, 0:SUBLANES, :] = jnp.where(i == 0, cprev_ref[:, :, ff_cols], carry_scr[j])
    n_slices = tf // MXU_WIDTH
    cols = [slice(s * MXU_WIDTH, (s + 1) * MXU_WIDTH) for s in range(n_slices)]
    for s, sl in enumerate(cols):
        a_scr[:, SUBLANES:, sl] = _dot(h, wu_ref[:, sl]).reshape(nb, tt, MXU_WIDTH)
        b_scr[:, sl] = _dot(h, wu_ref[:, tf + s * MXU_WIDTH:tf + (s + 1) * MXU_WIDTH])
    cw = cw_ref[:, ff_cols]
    cb = cb_ref[:, ff_cols]
    for sl in cols:
        gated = []
        for bi in range(nb):
            a = a_scr[bi, SUBLANES:, sl]
            first = jnp.concatenate([a_scr[bi, 0:SUBLANES, sl], a[0:SUBLANES]], axis=0)
            prev1 = jnp.concatenate([first[SUBLANES - 1:2 * SUBLANES - 1],
                                     pltpu.roll(a, 1, 0)[SUBLANES:]], axis=0)
            prev2 = jnp.concatenate([first[SUBLANES - 2:2 * SUBLANES - 2],
                                     pltpu.roll(a, 2, 0)[SUBLANES:]], axis=0)
            conv = (prev2 * cw[0:1, sl] + prev1 * cw[1:2, sl] + a * cw[2:3, sl]) + cb[:, sl]
            gated.append(_gelu(conv) * b_scr[bi * tt:(bi + 1) * tt, sl])
        act = (gated[0] if nb == 1 else jnp.concatenate(gated, axis=0)).astype(BF16)
        acc_scr[...] += _dot(act, wd_ref[sl, :])
    tail = a_scr[:, tt:tt + SUBLANES, :]
    carry_scr[j] = tail
    tail_ref[:, :, ff_cols] = tail

    @pl.when(j == pl.num_programs(2) - 1)
    def _():
        f = _rms(acc_scr[...], gpost_ref[...]).reshape(nb, tt, d)
        o_ref[...] = x_ref[...] + f


def _ffn_up_tiles(w_up_bf, tf):
    d, two_ff = w_up_bf.shape
    nj = two_ff // 2 // tf
    return w_up_bf.reshape(d, 2, nj, tf).transpose(2, 0, 1, 3).reshape(nj, d, 2 * tf)


def _conv_ffn(x3d, g_pre, w_up_tiles, conv_w, conv_b, w_down, g_post, conv_prev8, nb, tt):
    b, t, d = x3d.shape
    d_ff = w_down.shape[0]
    nj = w_up_tiles.shape[0]
    tf = d_ff // nj
    nt = t // tt
    return pl.pallas_call(
        _ffn_kernel,
        grid=(b // nb, nt, nj),
        in_specs=[pl.BlockSpec((nb, tt, d), lambda bi, i, j: (bi, i, 0)),
                  pl.BlockSpec((1, d), lambda bi, i, j: (0, 0)),
                  pl.BlockSpec((None, d, 2 * tf), lambda bi, i, j: (j, 0, 0)),
                  pl.BlockSpec((CONV_W, d_ff), lambda bi, i, j: (0, 0)),
                  pl.BlockSpec((1, d_ff), lambda bi, i, j: (0, 0)),
                  pl.BlockSpec((tf, d), lambda bi, i, j: (j, 0)),
                  pl.BlockSpec((1, d), lambda bi, i, j: (0, 0)),
                  pl.BlockSpec((nb, SUBLANES, d_ff), lambda bi, i, j: (bi, 0, 0))],
        out_specs=[pl.BlockSpec((nb, tt, d), lambda bi, i, j: (bi, i, 0)),
                   pl.BlockSpec((nb, None, SUBLANES, d_ff), lambda bi, i, j: (bi, i, 0, 0))],
        out_shape=[jax.ShapeDtypeStruct((b, t, d), F32),
                   jax.ShapeDtypeStruct((b, nt, SUBLANES, d_ff), F32)],
        scratch_shapes=[pltpu.VMEM((nb * tt, d), BF16), pltpu.VMEM((nb * tt, d), F32),
                        pltpu.VMEM((nb, tt + SUBLANES, tf), F32), pltpu.VMEM((nb * tt, tf), F32),
                        pltpu.VMEM((nj, nb, SUBLANES, tf), F32)],
        compiler_params=_params(("arbitrary", "arbitrary", "arbitrary"), 56),
        name="ffn",
    )(x3d, g_pre.reshape(1, d), w_up_tiles, conv_w.astype(F32), conv_b.astype(F32).reshape(1, d_ff),
      w_down, g_post.reshape(1, d), conv_prev8)


def _tiles(b, t):
    row_tile = 512
    tt = min(t, row_tile)
    return dict(
        proj_rows=256,
        attn_batch=max(1, min(b, 1024 // t)),
        s5_steps=min(t, row_tile // b),
        memattn_rows=tt,
        mix_rows=row_tile, mix_cols=512,
        ffn_batch=min(b, row_tile // tt), ffn_rows=tt)


FFN_COLS = 2 * MXU_WIDTH


def _layer(x, attn_past, s0, conv_prev, mem_k, mem_v, bias, lw):
    b, t, d = x.shape
    m = b * t
    q_w, kv_w, ssm_w, mem_w = lw['q_w'], lw['kv_w'], lw['ssm_w'], lw['mem_w']
    n_state = lw['n_state']
    d_ff = lw['w_down'].shape[0]
    tiles = _tiles(b, t)
    x2d = x.reshape(m, d)

    q, kv, u, qm = _norm_proj(x2d, lw['norm_pre_mix'], lw['proj_outputs'], tiles['proj_rows'])
    kv3 = kv.reshape(b, t, 2 * kv_w)
    if attn_past is None:
        hist_k = jnp.zeros((b, WINDOW, kv_w), F32)
        hist_v = hist_k
        first_valid = WINDOW
    else:
        hist_k = attn_past[0].astype(F32).reshape(b, WINDOW, kv_w)
        hist_v = attn_past[1].astype(F32).reshape(b, WINDOW, kv_w)
        first_valid = 0
    o_a = _band_attention(q.reshape(b, t, q_w), kv3, hist_k, hist_v, bias, lw['attn_sinks'],
                          tiles['attn_batch'], first_valid).reshape(m, q_w)

    if s0 is None:
        s0_re = jnp.zeros((b, n_state), F32)
        s0_im = s0_re
    else:
        s0_re = s0[0].astype(F32).reshape(b, n_state)
        s0_im = s0[1].astype(F32).reshape(b, n_state)
    u_tb = u.reshape(b, t, ssm_w).transpose(1, 0, 2)
    o_s_tb, s_re, s_im = _s5_layer(u_tb, s0_re, s0_im, lw['s5'], tiles['s5_steps'])
    o_s = o_s_tb.reshape(t, b, ssm_w).transpose(1, 0, 2).reshape(m, ssm_w)

    o_m = _memory_attention(qm, mem_k, mem_v, b, t, tiles['memattn_rows'])

    x1 = _mix_residual(x2d, o_a, o_s, o_m, lw['norm_pre_mix'], lw['w_in_bf'], lw['gate_col0'], lw['w_oa'],
                       lw['w_out_bf'], lw['norm_post_mix'], tiles['mix_rows'], tiles['mix_cols'])

    if conv_prev is None:
        conv_prev8 = jnp.zeros((b, SUBLANES, d_ff), F32)
    else:
        conv_prev8 = jnp.pad(conv_prev.astype(F32), ((0, 0), (SUBLANES - (CONV_W - 1), 0), (0, 0)))
    x2, tails = _conv_ffn(x1.reshape(b, t, d), lw['norm_pre_ffn'], lw['w_up_tiles'], lw['conv_w'], lw['conv_b'],
                          lw['w_down'], lw['norm_post_ffn'], conv_prev8, tiles['ffn_batch'], tiles['ffn_rows'])
    conv_new = tails[:, -1, SUBLANES - (CONV_W - 1):, :]

    n_kv = kv_w // HEAD_DIM
    k_new = jnp.concatenate([hist_k, kv3[:, :, :kv_w]], axis=1)[:, -WINDOW:].reshape(b, WINDOW, n_kv, HEAD_DIM)
    v_new = jnp.concatenate([hist_v, kv3[:, :, kv_w:]], axis=1)[:, -WINDOW:].reshape(b, WINDOW, n_kv, HEAD_DIM)
    return x2, k_new, v_new, s_re, s_im, conv_new


def kernel(x_prompt, x_sample, cache_attn_k, cache_attn_v, cache_mem_k, cache_mem_v, state_ssm_re, state_ssm_im, state_conv, mem_prompt, rel_bias_table, norm_pre_mix, norm_post_mix, norm_pre_ffn, norm_post_ffn, norm_mem, w_in, attn_sinks, ssm_a_re, ssm_a_im, ssm_log_dt, ssm_b_re, ssm_b_im, ssm_c_re, ssm_c_im, ssm_d, w_glu, w_mem_kv, w_out, w_up, conv_w, conv_b, w_down):
    depth = w_in.shape[0]
    bp, _, d = x_prompt.shape
    n_mem = mem_prompt.shape[1]
    n_q = attn_sinks.shape[1]
    n_kv, hd = cache_attn_k.shape[-2:]
    assert hd == HEAD_DIM and n_kv == N_KV and cache_attn_k.shape[2] == WINDOW
    groups, p_state = ssm_a_re.shape[1:]
    q_w, kv_w = n_q * HEAD_DIM, n_kv * HEAD_DIM
    ssm_w = ssm_d.shape[1]
    mem_w = w_mem_kv.shape[2] // 2
    proj_w = q_w + 2 * kv_w + ssm_w + mem_w
    mem_hd = mem_w // MEM_HEADS

    xp, xs = x_prompt, x_sample
    outs = [[] for _ in range(12)]
    for l in range(depth):
        bias = _rel_bias(rel_bias_table)
        lp = dict(ssm_a_re=ssm_a_re[l], ssm_a_im=ssm_a_im[l], ssm_log_dt=ssm_log_dt[l],
                  ssm_b_re=ssm_b_re[l], ssm_b_im=ssm_b_im[l], ssm_c_re=ssm_c_re[l], ssm_c_im=ssm_c_im[l],
                  ssm_d=ssm_d[l], w_glu=w_glu[l])
        w_in_bf = w_in[l].astype(BF16)
        w_out_bf = w_out[l].astype(BF16)
        grp = n_q // n_kv
        w_q = w_in_bf[:, :q_w].reshape(d, n_kv, grp, HEAD_DIM).transpose(0, 2, 1, 3).reshape(d, q_w)
        w_oa = w_out_bf[:q_w].reshape(n_kv, grp, HEAD_DIM, d).transpose(1, 0, 2, 3).reshape(q_w, d)
        u_off = q_w + 2 * kv_w
        proj_outputs = [(BF16, [(w_q, q_w, 0)]),
                        (F32, _col_blocks(w_in_bf, q_w, 2 * kv_w)),
                        (F32, _col_blocks(w_in_bf, u_off, ssm_w)),
                        (BF16, _col_blocks(w_in_bf, u_off + ssm_w, mem_w))]
        lw = dict(q_w=q_w, kv_w=kv_w, ssm_w=ssm_w, mem_w=mem_w, n_state=groups * p_state,
                  norm_pre_mix=norm_pre_mix[l], norm_post_mix=norm_post_mix[l],
                  norm_pre_ffn=norm_pre_ffn[l], norm_post_ffn=norm_post_ffn[l],
                  proj_outputs=proj_outputs, w_in_bf=w_in_bf, gate_col0=proj_w, w_oa=w_oa, w_out_bf=w_out_bf,
                  attn_sinks=attn_sinks[l], s5=_s5_params(lp),
                  w_up_tiles=_ffn_up_tiles(w_up[l].astype(BF16), FFN_COLS),
                  conv_w=conv_w[l], conv_b=conv_b[l], w_down=w_down[l].astype(BF16))

        w_mem_bf = w_mem_kv[l].astype(BF16)
        mk_p, mv_p = _norm_proj(mem_prompt.reshape(bp * n_mem, d), norm_mem[l],
                                [(F32, [(w_mem_bf, mem_w, 0)]), (F32, [(w_mem_bf, mem_w, 1)])],
                                _tiles(bp, n_mem)['proj_rows'])
        mk_p = mk_p.reshape(bp, n_mem, mem_w)
        mv_p = mv_p.reshape(bp, n_mem, mem_w)
        xp, k_p, v_p, sr_p, si_p, c_p = _layer(xp, None, None, None, mk_p, mv_p, bias, lw)

        bs = xs.shape[0]
        xs, k_s, v_s, sr_s, si_s, c_s = _layer(
            xs, (cache_attn_k[l], cache_attn_v[l]), (state_ssm_re[l], state_ssm_im[l]), state_conv[l],
            cache_mem_k[l].reshape(bs, n_mem, mem_w), cache_mem_v[l].reshape(bs, n_mem, mem_w), bias, lw)

        vals = (k_p, v_p, sr_p.reshape(bp, groups, p_state), si_p.reshape(bp, groups, p_state), c_p,
                mk_p.reshape(bp, n_mem, MEM_HEADS, mem_hd), mv_p.reshape(bp, n_mem, MEM_HEADS, mem_hd),
                k_s, v_s, sr_s.reshape(bs, groups, p_state), si_s.reshape(bs, groups, p_state), c_s)
        for acc, val in zip(outs, vals):
            acc.append(val)
    return (xp, xs) + tuple(jnp.stack(o) for o in outs)
```

```python
import functools
import math

import jax
import jax.numpy as jnp
from jax import lax
from jax.experimental import pallas as pl
from jax.experimental.pallas import tpu as pltpu

F32 = jnp.float32
BF16 = jnp.bfloat16

EPS = 1e-6
NEG_INF = -1e30
CHUNK = 64
WINDOW = 128
HEAD_DIM = 64
N_KV = 4
MAX_DISTANCE = 128
SSM_GROUP_CH = 16
MEM_HEADS = 4
CONV_W = 3

MIB = 1024 * 1024
LANES = 128
MXU_WIDTH = 256
SUBLANES = 8
SCAN_CARRY_ELEMS = 4096


def _params(semantics, vmem_mib):
    return pltpu.CompilerParams(dimension_semantics=semantics, vmem_limit_bytes=vmem_mib * MIB)


def _rms(x, g):
    y = x * lax.rsqrt(jnp.mean(x * x, axis=-1, keepdims=True) + EPS)
    return y * g


def _dot(a, b):
    return jnp.dot(a, b, preferred_element_type=F32)


def _dot_t(a, b):
    return lax.dot_general(a, b, (((1,), (1,)), ((), ())), preferred_element_type=F32)


def _proj_kernel(x_ref, g_ref, *refs, blocks_per_out):
    n_w = sum(blocks_per_out)
    w_refs, o_refs = refs[:n_w], refs[n_w:]
    h = _rms(x_ref[...], g_ref[...]).astype(BF16)
    first = 0
    for o_ref, count in zip(o_refs, blocks_per_out):
        parts = [_dot(h, w_ref[...]) for w_ref in w_refs[first:first + count]]
        r = parts[0] if count == 1 else jnp.concatenate(parts, axis=1)
        o_ref[...] = r.astype(o_ref.dtype)
        first += count


def _col_blocks(arr, start, width):
    blk = math.gcd(start, width) if start else width
    assert blk % LANES == 0
    return [(arr, blk, start // blk + k) for k in range(width // blk)]


def _norm_proj(x2d, g, outputs, tm):
    m, d = x2d.shape
    w_args, w_specs, out_specs, out_shapes = [], [], [], []
    for dt, blocks in outputs:
        for arr, width, blk in blocks:
            assert arr.shape[0] == d and arr.shape[1] % width == 0
            w_args.append(arr)
            w_specs.append(pl.BlockSpec((d, width), lambda i, blk=blk: (0, blk)))
        total = sum(width for _, width, _ in blocks)
        out_specs.append(pl.BlockSpec((tm, total), lambda i: (i, 0)))
        out_shapes.append(jax.ShapeDtypeStruct((m, total), dt))
    return pl.pallas_call(
        functools.partial(_proj_kernel, blocks_per_out=tuple(len(b) for _, b in outputs)),
        grid=(m // tm,),
        in_specs=[pl.BlockSpec((tm, d), lambda i: (i, 0)),
                  pl.BlockSpec((1, d), lambda i: (0, 0))] + w_specs,
        out_specs=out_specs,
        out_shape=out_shapes,
        compiler_params=_params(("parallel",), 48),
        name="proj",
    )(x2d, g.reshape(1, d), *w_args)


def _bias_kernel(idx_ref, table_ref, o_ref, *, n_buckets, n_heads):
    idx = idx_ref[...]
    for h in range(n_heads):
        acc = jnp.zeros(idx.shape, F32)
        for b in range(n_buckets):
            acc = jnp.where(idx == b, table_ref[b, h], acc)
        o_ref[h] = acc


def _t5_bucket(rel, n_buckets):
    half = n_buckets // 2
    max_exact = half // 2
    n = jnp.abs(rel)
    large = max_exact + (jnp.log(jnp.maximum(n, 1).astype(F32) / max_exact)
                         / math.log(MAX_DISTANCE / max_exact) * (half - max_exact)).astype(jnp.int32)
    large = jnp.minimum(large, half - 1)
    return jnp.where(rel > 0, half, 0) + jnp.where(n < max_exact, n, large)


def _rel_bias(table):
    n_buckets, n_heads = table.shape
    lk = WINDOW + CHUNK
    rel = jnp.arange(lk)[None, :] - WINDOW - jnp.arange(CHUNK)[:, None]
    idx = _t5_bucket(rel, n_buckets).astype(jnp.int32)
    return pl.pallas_call(
        functools.partial(_bias_kernel, n_buckets=n_buckets, n_heads=n_heads),
        in_specs=[pl.BlockSpec((CHUNK, lk), lambda: (0, 0)),
                  pl.BlockSpec(memory_space=pltpu.SMEM)],
        out_specs=pl.BlockSpec((n_heads, CHUNK, lk), lambda: (0, 0, 0)),
        out_shape=jax.ShapeDtypeStruct((n_heads, CHUNK, lk), F32),
        name="bias",
    )(idx, table.astype(F32))


def _attn_kernel(q_ref, k_ref, v_ref, hk_ref, hv_ref, bias_ref, o_ref, kf_scr, vf_scr, *, n_masked, grp):
    nb, t, _ = q_ref.shape
    kv_w = k_ref.shape[2]
    n_chunks = t // CHUNK
    lk = WINDOW + CHUNK
    lkp = bias_ref.shape[2]
    pair_w = 2 * HEAD_DIM
    scale = HEAD_DIM ** -0.5
    low_half = lax.broadcasted_iota(jnp.int32, (1, pair_w), 1) < HEAD_DIM
    ones = jnp.ones((lkp, pair_w), BF16)
    kv_pad = jnp.zeros((lkp - lk, kv_w), BF16)

    kf_scr[:, 0:WINDOW, :] = hk_ref[...].astype(BF16)
    kf_scr[:, WINDOW:, :] = k_ref[...].astype(BF16)
    vf_scr[:, 0:WINDOW, :] = hv_ref[...].astype(BF16)
    vf_scr[:, WINDOW:, :] = v_ref[...].astype(BF16)

    def chunk(n, carry):
        bi = n // n_chunks
        c = n % n_chunks
        r0 = pl.multiple_of(c * CHUNK, CHUNK)
        q = q_ref[bi, pl.ds(r0, CHUNK), :] * scale
        k = jnp.concatenate([kf_scr[bi, pl.ds(r0, lk), :], kv_pad], axis=0)
        v = jnp.concatenate([vf_scr[bi, pl.ds(r0, lk), :], kv_pad], axis=0)
        variant = jnp.minimum(c, n_masked) * N_KV
        scores = []
        for h in range(N_KV):
            j, e = divmod(h, 2)
            qp = jnp.concatenate([q[:, g * kv_w + j * pair_w:g * kv_w + (j + 1) * pair_w]
                                  for g in range(grp)], axis=0)
            in_head = low_half if e == 0 else jnp.logical_not(low_half)
            qh = jnp.where(in_head, qp, jnp.zeros_like(qp))
            scores.append(_dot_t(qh, k[:, j * pair_w:(j + 1) * pair_w]) + bias_ref[variant + h])
        probs = [jnp.exp(s - jnp.max(s, axis=-1, keepdims=True)).astype(BF16) for s in scores]
        v_ext = [jnp.concatenate([v[:, j * pair_w:(j + 1) * pair_w], ones], axis=1) for j in range(N_KV // 2)]
        sums = [_dot(p, v_ext[h // 2]) for h, p in enumerate(probs)]
        outs = [r[:, :pair_w] / r[:, pair_w:] for r in sums]
        for j in range(N_KV // 2):
            o_pair = jnp.where(low_half, outs[2 * j], outs[2 * j + 1])
            for g in range(grp):
                o_ref[bi, pl.ds(r0, CHUNK), g * kv_w + j * pair_w:g * kv_w + (j + 1) * pair_w] = (
                    o_pair[g * CHUNK:(g + 1) * CHUNK].astype(BF16))
        return carry

    lax.fori_loop(0, nb * n_chunks, chunk, 0, unroll=4)


def _band_attention(q3, kv3, hist_k, hist_v, bias, sinks, nb, first_valid):
    b, t, q_w = q3.shape
    kv_w = hist_k.shape[2]
    grp = q_w // HEAD_DIM // N_KV
    lk = WINDOW + CHUNK
    lkp = 2 * LANES
    n_masked = first_valid // CHUNK
    sink_col = jnp.broadcast_to(sinks.astype(F32).reshape(N_KV, grp, 1, 1), (N_KV, grp, CHUNK, 1))
    ext = jnp.concatenate([bias.reshape(N_KV, grp * CHUNK, lk), sink_col.reshape(N_KV, grp * CHUNK, 1),
                           jnp.full((N_KV, grp * CHUNK, lkp - lk - 1), NEG_INF, F32)], axis=2)
    key = jnp.arange(lkp)
    variants = [jnp.where(key < (n_masked - i) * CHUNK, NEG_INF, ext) for i in range(n_masked + 1)]
    bias_ext = jnp.concatenate(variants, axis=0)
    return pl.pallas_call(
        functools.partial(_attn_kernel, n_masked=n_masked, grp=grp),
        grid=(b // nb,),
        in_specs=[pl.BlockSpec((nb, t, q_w), lambda i: (i, 0, 0)),
                  pl.BlockSpec((nb, t, kv_w), lambda i: (i, 0, 0)),
                  pl.BlockSpec((nb, t, kv_w), lambda i: (i, 0, 1)),
                  pl.BlockSpec((nb, WINDOW, kv_w), lambda i: (i, 0, 0)),
                  pl.BlockSpec((nb, WINDOW, kv_w), lambda i: (i, 0, 0)),
                  pl.BlockSpec(bias_ext.shape, lambda i: (0, 0, 0))],
        out_specs=pl.BlockSpec((nb, t, q_w), lambda i: (i, 0, 0)),
        out_shape=jax.ShapeDtypeStruct((b, t, q_w), BF16),
        scratch_shapes=[pltpu.VMEM((nb, WINDOW + t, kv_w), BF16), pltpu.VMEM((nb, WINDOW + t, kv_w), BF16)],
        compiler_params=_params(("parallel",), 40),
        name="attn",
    )(q3, kv3, kv3, hist_k, hist_v, bias_ext)


def _gelu(x):
    return jax.nn.gelu(x)


def _s5_kernel(u_ref, s0re_ref, s0im_ref, are_ref, aim_ref, wbre_ref, wbim_ref, wcre_ref, wcim_ref,
               d_ref, wglu_ref, o_ref, sre_out_ref, sim_out_ref,
               sre_scr, sim_scr, st_re, st_im, y_scr, *, batch, steps, lane_chunk):
    i = pl.program_id(0)

    @pl.when(i == 0)
    def _():
        st_re[...] = s0re_ref[...]
        st_im[...] = s0im_ref[...]

    u = u_ref[...]
    ub = u.astype(BF16)
    n_in_blk, in_blk, st_blk = wbre_ref.shape
    for r in range(n_in_blk):
        blk = ub[:, r * in_blk:(r + 1) * in_blk]
        sre_scr[:, r * st_blk:(r + 1) * st_blk] = _dot(blk, wbre_ref[r])
        sim_scr[:, r * st_blk:(r + 1) * st_blk] = _dot(blk, wbim_ref[r])

    n_state = sre_scr.shape[1]
    for lc in range(n_state // lane_chunk):
        sl = slice(lc * lane_chunk, (lc + 1) * lane_chunk)
        ar = are_ref[:, sl]
        ai = aim_ref[:, sl]

        def body(t, carry, sl=sl, ar=ar, ai=ai):
            sr, si = carry
            rows = pl.ds(pl.multiple_of(t * batch, batch), batch)
            nr = ar * sr - ai * si + sre_scr[rows, sl]
            ni = ar * si + ai * sr + sim_scr[rows, sl]
            sre_scr[rows, sl] = nr
            sim_scr[rows, sl] = ni
            return nr, ni

        sr, si = lax.fori_loop(0, steps, body, (st_re[:, sl], st_im[:, sl]), unroll=4)
        st_re[:, sl] = sr
        st_im[:, sl] = si

    n_out_blk, k_blk, out_blk = wcre_ref.shape
    for kb in range(n_out_blk):
        ksl = slice(kb * k_blk, (kb + 1) * k_blk)
        y_scr[:, kb * out_blk:(kb + 1) * out_blk] = (
            _dot(sre_scr[:, ksl].astype(BF16), wcre_ref[kb])
            + _dot(sim_scr[:, ksl].astype(BF16), wcim_ref[kb]))
    y = _gelu(y_scr[...] + d_ref[...] * u)
    o_ref[...] = (y * jax.nn.sigmoid(_dot(y.astype(BF16), wglu_ref[...]))).astype(BF16)

    @pl.when(i == pl.num_programs(0) - 1)
    def _():
        sre_out_ref[...] = st_re[...]
        sim_out_ref[...] = st_im[...]


def _s5_params(lp):
    lam = lax.complex(lp['ssm_a_re'].astype(F32), lp['ssm_a_im'].astype(F32))
    dt = jnp.exp(lp['ssm_log_dt'].astype(F32))[:, None]
    a_bar = jnp.exp(lam * dt)
    b_mat = lax.complex(lp['ssm_b_re'].astype(F32), lp['ssm_b_im'].astype(F32))
    b_bar = ((a_bar - 1.0) / lam)[..., None] * b_mat
    g, p, c = b_bar.shape
    gb_in = 256 // c
    gb_out = 128 // c

    def b_blocks(x):
        x = x.reshape(g // gb_in, gb_in, p, c).transpose(0, 1, 3, 2)
        x = jnp.einsum('rgcp,gh->rgchp', x, jnp.eye(gb_in, dtype=F32))
        return x.reshape(g // gb_in, gb_in * c, gb_in * p).astype(BF16)

    def c_blocks(x):
        x = x.reshape(g // gb_out, gb_out, c, p).transpose(0, 1, 3, 2)
        x = jnp.einsum('kgpc,gh->kgphc', x, jnp.eye(gb_out, dtype=F32))
        return x.reshape(g // gb_out, gb_out * p, gb_out * c).astype(BF16)

    return dict(a_re=jnp.real(a_bar).reshape(1, g * p), a_im=jnp.imag(a_bar).reshape(1, g * p),
                wb_re=b_blocks(jnp.real(b_bar)), wb_im=b_blocks(jnp.imag(b_bar)),
                wc_re=c_blocks(lp['ssm_c_re'].astype(F32)), wc_im=c_blocks(-lp['ssm_c_im'].astype(F32)),
                d=lp['ssm_d'].astype(F32).reshape(1, -1), w_glu=lp['w_glu'].astype(BF16))


def _s5_layer(u_tb, s0_re, s0_im, sp, steps):
    t, b, ssm_w = u_tb.shape
    n_state = s0_re.shape[1]
    rows = steps * b
    lane_chunk = SCAN_CARRY_ELEMS // b
    a_re = jnp.broadcast_to(sp['a_re'], (b, n_state))
    a_im = jnp.broadcast_to(sp['a_im'], (b, n_state))

    def full(x):
        nd = x.ndim
        return pl.BlockSpec(x.shape, lambda i: (0,) * nd)

    consts = [s0_re, s0_im, a_re, a_im, sp['wb_re'], sp['wb_im'], sp['wc_re'], sp['wc_im'], sp['d'], sp['w_glu']]
    return pl.pallas_call(
        functools.partial(_s5_kernel, batch=b, steps=steps, lane_chunk=lane_chunk),
        grid=(t // steps,),
        in_specs=[pl.BlockSpec((rows, ssm_w), lambda i: (i, 0))] + [full(x) for x in consts],
        out_specs=[pl.BlockSpec((rows, ssm_w), lambda i: (i, 0)),
                   pl.BlockSpec((b, n_state), lambda i: (0, 0)),
                   pl.BlockSpec((b, n_state), lambda i: (0, 0))],
        out_shape=[jax.ShapeDtypeStruct((t * b, ssm_w), BF16),
                   jax.ShapeDtypeStruct((b, n_state), F32),
                   jax.ShapeDtypeStruct((b, n_state), F32)],
        scratch_shapes=[pltpu.VMEM((rows, n_state), F32), pltpu.VMEM((rows, n_state), F32),
                        pltpu.VMEM((b, n_state), F32), pltpu.VMEM((b, n_state), F32),
                        pltpu.VMEM((rows, ssm_w), F32)],
        compiler_params=_params(("arbitrary",), 48),
        name="s5",
    )(u_tb.reshape(t * b, ssm_w), *consts)


def _memattn_kernel(q_ref, k_ref, v_ref, o_ref, *, head_dim):
    q = q_ref[...]
    k = k_ref[...].astype(BF16)
    v = v_ref[...].astype(BF16)
    scale = head_dim ** -0.5
    for h in range(MEM_HEADS):
        sl = slice(h * head_dim, (h + 1) * head_dim)
        s = _dot_t(q[:, sl], k[:, sl]) * scale
        m = jnp.max(s, axis=-1, keepdims=True)
        e = jnp.exp(s - m)
        p = e / jnp.sum(e, axis=-1, keepdims=True)
        o_ref[:, sl] = _dot(p.astype(BF16), v[:, sl]).astype(BF16)


def _memory_attention(qm, mem_k, mem_v, b, t, tq):
    n_mem, mem_w = mem_k.shape[1:]
    nt = t // tq
    return pl.pallas_call(
        functools.partial(_memattn_kernel, head_dim=mem_w // MEM_HEADS),
        grid=(b, nt),
        in_specs=[pl.BlockSpec((tq, mem_w), lambda bi, ti: (bi * nt + ti, 0)),
                  pl.BlockSpec((None, n_mem, mem_w), lambda bi, ti: (bi, 0, 0)),
                  pl.BlockSpec((None, n_mem, mem_w), lambda bi, ti: (bi, 0, 0))],
        out_specs=pl.BlockSpec((tq, mem_w), lambda bi, ti: (bi * nt + ti, 0)),
        out_shape=jax.ShapeDtypeStruct((b * t, mem_w), BF16),
        compiler_params=_params(("parallel", "parallel"), 32),
        name="memattn",
    )(qm, mem_k, mem_v)


def _mix_kernel(x_ref, oa_ref, os_ref, om_ref, gpre_ref, wg0_ref, wg1_ref, wg2_ref,
                wa_ref, ws_ref, wm_ref, gpost_ref, o_ref, h_scr, ss_scr, *, tn):
    j = pl.program_id(1)

    @pl.when(j == 0)
    def _():
        h_scr[...] = _rms(x_ref[...], gpre_ref[...]).astype(BF16)
        ss_scr[...] = jnp.zeros_like(ss_scr)

    h = h_scr[...]
    merged = (jax.nn.sigmoid(_dot(h, wg0_ref[...])) * _dot(oa_ref[...], wa_ref[...])
              + jax.nn.sigmoid(_dot(h, wg1_ref[...])) * _dot(os_ref[...], ws_ref[...])
              + jax.nn.sigmoid(_dot(h, wg2_ref[...])) * _dot(om_ref[...], wm_ref[...]))
    o_ref[:, pl.ds(pl.multiple_of(j * tn, tn), tn)] = merged
    ss_scr[...] += jnp.sum(merged * merged, axis=-1, keepdims=True)

    @pl.when(j == pl.num_programs(1) - 1)
    def _():
        inv = lax.rsqrt(ss_scr[...] / o_ref.shape[1] + EPS)
        o_ref[...] = x_ref[...] + (o_ref[...] * inv) * gpost_ref[...]


def _mix_residual(x2d, o_a, o_s, o_m, g_pre, w_in_bf, gate_col0, w_oa, w_out_bf, g_post, tm, tn):
    m, d = x2d.shape
    nj = d // tn
    q_w, ssm_w, mem_w = o_a.shape[1], o_s.shape[1], o_m.shape[1]
    assert gate_col0 % tn == 0 and q_w % ssm_w == 0 and (q_w + ssm_w) % mem_w == 0
    gate_blk0 = gate_col0 // tn

    def rows(w):
        mode = dict(pipeline_mode=pl.Buffered(1)) if m == tm else {}
        return pl.BlockSpec((tm, w), lambda i, j: (i, 0), **mode)

    def gate_spec(br):
        return pl.BlockSpec((d, tn), lambda i, j: (0, gate_blk0 + br * nj + j))

    vec = pl.BlockSpec((1, d), lambda i, j: (0, 0))
    return pl.pallas_call(
        functools.partial(_mix_kernel, tn=tn),
        grid=(m // tm, nj),
        in_specs=[rows(d), rows(q_w), rows(ssm_w), rows(mem_w), vec,
                  gate_spec(0), gate_spec(1), gate_spec(2),
                  pl.BlockSpec((q_w, tn), lambda i, j: (0, j)),
                  pl.BlockSpec((ssm_w, tn), lambda i, j: (q_w // ssm_w, j)),
                  pl.BlockSpec((mem_w, tn), lambda i, j: ((q_w + ssm_w) // mem_w, j)), vec],
        out_specs=rows(d),
        out_shape=jax.ShapeDtypeStruct((m, d), F32),
        scratch_shapes=[pltpu.VMEM((tm, d), BF16), pltpu.VMEM((tm, 1), F32)],
        compiler_params=_params(("parallel", "arbitrary"), 56),
        name="mix",
    )(x2d, o_a, o_s, o_m, g_pre.reshape(1, d), w_in_bf, w_in_bf, w_in_bf, w_oa, w_out_bf, w_out_bf,
      g_post.reshape(1, d))


def _ffn_kernel(x_ref, gpre_ref, wua_ref, wub_ref, cw_ref, cb_ref, wd_ref, gpost_ref, cprev_ref,
                o_ref, tail_ref, h_scr, a_scr, b_scr, carry_scr):
    i = pl.program_id(1)
    j = pl.program_id(2)
    nb, tt, d = x_ref.shape
    tf = wua_ref.shape[1]
    rows = nb * tt

    @pl.when(j == 0)
    def _():
        h_scr[...] = _rms(x_ref[...].reshape(rows, d), gpre_ref[...]).astype(BF16)
        o_ref[...] = jnp.zeros_like(o_ref)

    h = h_scr[...]
    a_scr[:, 0:SUBLANES, :] = jnp.where(i == 0, cprev_ref[...], carry_scr[j])
    n_slices = tf // MXU_WIDTH
    cols = [slice(s * MXU_WIDTH, (s + 1) * MXU_WIDTH) for s in range(n_slices)]
    for sl in cols:
        a_scr[:, SUBLANES:, sl] = _dot(h, wua_ref[:, sl]).reshape(nb, tt, MXU_WIDTH)
        b_scr[:, sl] = _dot(h, wub_ref[:, sl])
    cw = cw_ref[...]
    cb = cb_ref[...]
    for sl in cols:
        gated = []
        for bi in range(nb):
            a = a_scr[bi, SUBLANES:, sl]
            first = jnp.concatenate([a_scr[bi, 0:SUBLANES, sl], a[0:SUBLANES]], axis=0)
            prev1 = jnp.concatenate([first[SUBLANES - 1:2 * SUBLANES - 1],
                                     pltpu.roll(a, 1, 0)[SUBLANES:]], axis=0)
            prev2 = jnp.concatenate([first[SUBLANES - 2:2 * SUBLANES - 2],
                                     pltpu.roll(a, 2, 0)[SUBLANES:]], axis=0)
            conv = (prev2 * cw[0:1, sl] + prev1 * cw[1:2, sl] + a * cw[2:3, sl]) + cb[:, sl]
            gated.append(_gelu(conv) * b_scr[bi * tt:(bi + 1) * tt, sl])
        act = (gated[0] if nb == 1 else jnp.concatenate(gated, axis=0)).astype(BF16)
        o_ref[...] += _dot(act, wd_ref[sl, :]).reshape(nb, tt, d)
    tail = a_scr[:, tt:tt + SUBLANES, :]
    carry_scr[j] = tail
    tail_ref[...] = tail

    @pl.when(j == pl.num_programs(2) - 1)
    def _():
        f = _rms(o_ref[...].reshape(rows, d), gpost_ref[...]).reshape(nb, tt, d)
        o_ref[...] = x_ref[...] + f


def _conv_ffn(x3d, g_pre, w_up, conv_w, conv_b, w_down, g_post, conv_prev8, nb, tt, tf):
    b, t, d = x3d.shape
    d_ff = w_down.shape[0]
    nj = d_ff // tf
    nt = t // tt
    mode = dict(pipeline_mode=pl.Buffered(1)) if (b // nb) * nt == 1 else {}
    return pl.pallas_call(
        _ffn_kernel,
        grid=(b // nb, nt, nj),
        in_specs=[pl.BlockSpec((nb, tt, d), lambda bi, i, j: (bi, i, 0), **mode),
                  pl.BlockSpec((1, d), lambda bi, i, j: (0, 0)),
                  pl.BlockSpec((d, tf), lambda bi, i, j: (0, j)),
                  pl.BlockSpec((d, tf), lambda bi, i, j: (0, nj + j)),
                  pl.BlockSpec((CONV_W, tf), lambda bi, i, j: (0, j)),
                  pl.BlockSpec((1, tf), lambda bi, i, j: (0, j)),
                  pl.BlockSpec((tf, d), lambda bi, i, j: (j, 0)),
                  pl.BlockSpec((1, d), lambda bi, i, j: (0, 0)),
                  pl.BlockSpec((nb, SUBLANES, tf), lambda bi, i, j: (bi, 0, j))],
        out_specs=[pl.BlockSpec((nb, tt, d), lambda bi, i, j: (bi, i, 0), **mode),
                   pl.BlockSpec((nb, None, SUBLANES, tf), lambda bi, i, j: (bi, i, 0, j))],
        out_shape=[jax.ShapeDtypeStruct((b, t, d), F32),
                   jax.ShapeDtypeStruct((b, nt, SUBLANES, d_ff), F32)],
        scratch_shapes=[pltpu.VMEM((nb * tt, d), BF16),
                        pltpu.VMEM((nb, tt + SUBLANES, tf), F32), pltpu.VMEM((nb * tt, tf), F32),
                        pltpu.VMEM((nj, nb, SUBLANES, tf), F32)],
        compiler_params=_params(("arbitrary", "arbitrary", "arbitrary"), 56),
        name="ffn",
    )(x3d, g_pre.reshape(1, d), w_up, w_up, conv_w.astype(F32), conv_b.astype(F32).reshape(1, d_ff),
      w_down, g_post.reshape(1, d), conv_prev8)


def _tiles(b, t):
    row_tile = 1024 if b * t <= 1024 else 512
    tt = min(t, row_tile)
    return dict(
        proj_rows=256,
        attn_batch=max(1, min(b, 1024 // t)),
        s5_steps=min(t, 512 // b),
        memattn_rows=tt,
        mix_rows=row_tile, mix_cols=512,
        ffn_batch=min(b, row_tile // tt), ffn_rows=tt)


FFN_COLS = 2 * MXU_WIDTH


def _layer(x, attn_past, s0, conv_prev, mem_k, mem_v, bias, lw):
    b, t, d = x.shape
    m = b * t
    q_w, kv_w, ssm_w, mem_w = lw['q_w'], lw['kv_w'], lw['ssm_w'], lw['mem_w']
    n_state = lw['n_state']
    d_ff = lw['w_down'].shape[0]
    tiles = _tiles(b, t)
    x2d = x.reshape(m, d)

    q, kv, u, qm = _norm_proj(x2d, lw['norm_pre_mix'], lw['proj_outputs'], tiles['proj_rows'])
    kv3 = kv.reshape(b, t, 2 * kv_w)
    if attn_past is None:
        hist_k = jnp.zeros((b, WINDOW, kv_w), F32)
        hist_v = hist_k
        first_valid = WINDOW
    else:
        hist_k = attn_past[0].astype(F32).reshape(b, WINDOW, kv_w)
        hist_v = attn_past[1].astype(F32).reshape(b, WINDOW, kv_w)
        first_valid = 0
    o_a = _band_attention(q.reshape(b, t, q_w), kv3, hist_k, hist_v, bias, lw['attn_sinks'],
                          tiles['attn_batch'], first_valid).reshape(m, q_w)

    if s0 is None:
        s0_re = jnp.zeros((b, n_state), F32)
        s0_im = s0_re
    else:
        s0_re = s0[0].astype(F32).reshape(b, n_state)
        s0_im = s0[1].astype(F32).reshape(b, n_state)
    u_tb = u.reshape(b, t, ssm_w).transpose(1, 0, 2)
    o_s_tb, s_re, s_im = _s5_layer(u_tb, s0_re, s0_im, lw['s5'], tiles['s5_steps'])
    o_s = o_s_tb.reshape(t, b, ssm_w).transpose(1, 0, 2).reshape(m, ssm_w)

    o_m = _memory_attention(qm, mem_k, mem_v, b, t, tiles['memattn_rows'])

    x1 = _mix_residual(x2d, o_a, o_s, o_m, lw['norm_pre_mix'], lw['w_in_bf'], lw['gate_col0'], lw['w_oa'],
                       lw['w_out_bf'], lw['norm_post_mix'], tiles['mix_rows'], tiles['mix_cols'])

    if conv_prev is None:
        conv_prev8 = jnp.zeros((b, SUBLANES, d_ff), F32)
    else:
        conv_prev8 = jnp.pad(conv_prev.astype(F32), ((0, 0), (SUBLANES - (CONV_W - 1), 0), (0, 0)))
    x2, tails = _conv_ffn(x1.reshape(b, t, d), lw['norm_pre_ffn'], lw['w_up'], lw['conv_w'], lw['conv_b'],
                          lw['w_down'], lw['norm_post_ffn'], conv_prev8,
                          tiles['ffn_batch'], tiles['ffn_rows'], FFN_COLS)
    conv_new = tails[:, -1, SUBLANES - (CONV_W - 1):, :]

    n_kv = kv_w // HEAD_DIM
    k_new = jnp.concatenate([hist_k, kv3[:, :, :kv_w]], axis=1)[:, -WINDOW:].reshape(b, WINDOW, n_kv, HEAD_DIM)
    v_new = jnp.concatenate([hist_v, kv3[:, :, kv_w:]], axis=1)[:, -WINDOW:].reshape(b, WINDOW, n_kv, HEAD_DIM)
    return x2, k_new, v_new, s_re, s_im, conv_new


def kernel(x_prompt, x_sample, cache_attn_k, cache_attn_v, cache_mem_k, cache_mem_v, state_ssm_re, state_ssm_im, state_conv, mem_prompt, rel_bias_table, norm_pre_mix, norm_post_mix, norm_pre_ffn, norm_post_ffn, norm_mem, w_in, attn_sinks, ssm_a_re, ssm_a_im, ssm_log_dt, ssm_b_re, ssm_b_im, ssm_c_re, ssm_c_im, ssm_d, w_glu, w_mem_kv, w_out, w_up, conv_w, conv_b, w_down):
    depth = w_in.shape[0]
    bp, _, d = x_prompt.shape
    n_mem = mem_prompt.shape[1]
    n_q = attn_sinks.shape[1]
    n_kv, hd = cache_attn_k.shape[-2:]
    assert hd == HEAD_DIM and n_kv == N_KV and cache_attn_k.shape[2] == WINDOW
    groups, p_state = ssm_a_re.shape[1:]
    q_w, kv_w = n_q * HEAD_DIM, n_kv * HEAD_DIM
    ssm_w = ssm_d.shape[1]
    mem_w = w_mem_kv.shape[2] // 2
    proj_w = q_w + 2 * kv_w + ssm_w + mem_w
    mem_hd = mem_w // MEM_HEADS

    xp, xs = x_prompt, x_sample
    outs = [[] for _ in range(12)]
    for l in range(depth):
        bias = _rel_bias(rel_bias_table)
        lp = dict(ssm_a_re=ssm_a_re[l], ssm_a_im=ssm_a_im[l], ssm_log_dt=ssm_log_dt[l],
                  ssm_b_re=ssm_b_re[l], ssm_b_im=ssm_b_im[l], ssm_c_re=ssm_c_re[l], ssm_c_im=ssm_c_im[l],
                  ssm_d=ssm_d[l], w_glu=w_glu[l])
        w_in_bf = w_in[l].astype(BF16)
        w_out_bf = w_out[l].astype(BF16)
        grp = n_q // n_kv
        w_q = w_in_bf[:, :q_w].reshape(d, n_kv, grp, HEAD_DIM).transpose(0, 2, 1, 3).reshape(d, q_w)
        w_oa = w_out_bf[:q_w].reshape(n_kv, grp, HEAD_DIM, d).transpose(1, 0, 2, 3).reshape(q_w, d)
        u_off = q_w + 2 * kv_w
        proj_outputs = [(BF16, [(w_q, q_w, 0)]),
                        (F32, _col_blocks(w_in_bf, q_w, 2 * kv_w)),
                        (F32, _col_blocks(w_in_bf, u_off, ssm_w)),
                        (BF16, _col_blocks(w_in_bf, u_off + ssm_w, mem_w))]
        lw = dict(q_w=q_w, kv_w=kv_w, ssm_w=ssm_w, mem_w=mem_w, n_state=groups * p_state,
                  norm_pre_mix=norm_pre_mix[l], norm_post_mix=norm_post_mix[l],
                  norm_pre_ffn=norm_pre_ffn[l], norm_post_ffn=norm_post_ffn[l],
                  proj_outputs=proj_outputs, w_in_bf=w_in_bf, gate_col0=proj_w, w_oa=w_oa, w_out_bf=w_out_bf,
                  attn_sinks=attn_sinks[l], s5=_s5_params(lp),
                  w_up=w_up[l].astype(BF16),
                  conv_w=conv_w[l], conv_b=conv_b[l], w_down=w_down[l].astype(BF16))

        w_mem_bf = w_mem_kv[l].astype(BF16)
        mk_p, mv_p = _norm_proj(mem_prompt.reshape(bp * n_mem, d), norm_mem[l],
                                [(F32, [(w_mem_bf, mem_w, 0)]), (F32, [(w_mem_bf, mem_w, 1)])],
                                _tiles(bp, n_mem)['proj_rows'])
        mk_p = mk_p.reshape(bp, n_mem, mem_w)
        mv_p = mv_p.reshape(bp, n_mem, mem_w)
        xp, k_p, v_p, sr_p, si_p, c_p = _layer(xp, None, None, None, mk_p, mv_p, bias, lw)

        bs = xs.shape[0]
        xs, k_s, v_s, sr_s, si_s, c_s = _layer(
            xs, (cache_attn_k[l], cache_attn_v[l]), (state_ssm_re[l], state_ssm_im[l]), state_conv[l],
            cache_mem_k[l].reshape(bs, n_mem, mem_w), cache_mem_v[l].reshape(bs, n_mem, mem_w), bias, lw)

        vals = (k_p, v_p, sr_p.reshape(bp, groups, p_state), si_p.reshape(bp, groups, p_state), c_p,
                mk_p.reshape(bp, n_mem, MEM_HEADS, mem_hd), mv_p.reshape(bp, n_mem, MEM_HEADS, mem_hd),
                k_s, v_s, sr_s.reshape(bs, groups, p_state), si_s.reshape(bs, groups, p_state), c_s)
        for acc, val in zip(outs, vals):
            acc.append(val)
    return (xp, xs) + tuple(jnp.stack(o) for o in outs)
```

```python
import functools
import math

import jax
import jax.numpy as jnp
from jax import lax
from jax.experimental import pallas as pl
from jax.experimental.pallas import tpu as pltpu

F32 = jnp.float32
BF16 = jnp.bfloat16

EPS = 1e-6
NEG_INF = -1e30
CHUNK = 64
WINDOW = 128
HEAD_DIM = 64
N_KV = 4
MAX_DISTANCE = 128
SSM_GROUP_CH = 16
MEM_HEADS = 4
CONV_W = 3

MIB = 1024 * 1024
LANES = 128
MXU_WIDTH = 256
SUBLANES = 8
SCAN_CARRY_ELEMS = 4096


def _params(semantics, vmem_mib):
    return pltpu.CompilerParams(dimension_semantics=semantics, vmem_limit_bytes=vmem_mib * MIB)


def _rms(x, g):
    y = x * lax.rsqrt(jnp.mean(x * x, axis=-1, keepdims=True) + EPS)
    return y * g


def _dot(a, b):
    return jnp.dot(a, b, preferred_element_type=F32)


def _dot_t(a, b):
    return lax.dot_general(a, b, (((1,), (1,)), ((), ())), preferred_element_type=F32)


def _proj_kernel(x_ref, g_ref, *refs, blocks_per_out):
    n_w = sum(blocks_per_out)
    w_refs, o_refs = refs[:n_w], refs[n_w:]
    h = _rms(x_ref[...], g_ref[...]).astype(BF16)
    first = 0
    for o_ref, count in zip(o_refs, blocks_per_out):
        parts = [_dot(h, w_ref[...]) for w_ref in w_refs[first:first + count]]
        r = parts[0] if count == 1 else jnp.concatenate(parts, axis=1)
        o_ref[...] = r.astype(o_ref.dtype)
        first += count


def _col_blocks(arr, start, width):
    blk = math.gcd(start, width) if start else width
    assert blk % LANES == 0
    return [(arr, blk, start // blk + k) for k in range(width // blk)]


def _norm_proj(x2d, g, outputs, tm):
    m, d = x2d.shape
    w_args, w_specs, out_specs, out_shapes = [], [], [], []
    for dt, blocks in outputs:
        for arr, width, blk in blocks:
            assert arr.shape[0] == d and arr.shape[1] % width == 0
            w_args.append(arr)
            w_specs.append(pl.BlockSpec((d, width), lambda i, blk=blk: (0, blk)))
        total = sum(width for _, width, _ in blocks)
        out_specs.append(pl.BlockSpec((tm, total), lambda i: (i, 0)))
        out_shapes.append(jax.ShapeDtypeStruct((m, total), dt))
    return pl.pallas_call(
        functools.partial(_proj_kernel, blocks_per_out=tuple(len(b) for _, b in outputs)),
        grid=(m // tm,),
        in_specs=[pl.BlockSpec((tm, d), lambda i: (i, 0)),
                  pl.BlockSpec((1, d), lambda i: (0, 0))] + w_specs,
        out_specs=out_specs,
        out_shape=out_shapes,
        compiler_params=_params(("parallel",), 48),
        name="proj",
    )(x2d, g.reshape(1, d), *w_args)


def _bias_kernel(idx_ref, table_ref, o_ref, *, n_buckets, n_heads):
    idx = idx_ref[...]
    for h in range(n_heads):
        acc = jnp.zeros(idx.shape, F32)
        for b in range(n_buckets):
            acc = jnp.where(idx == b, table_ref[b, h], acc)
        o_ref[h] = acc


def _t5_bucket(rel, n_buckets):
    half = n_buckets // 2
    max_exact = half // 2
    n = jnp.abs(rel)
    large = max_exact + (jnp.log(jnp.maximum(n, 1).astype(F32) / max_exact)
                         / math.log(MAX_DISTANCE / max_exact) * (half - max_exact)).astype(jnp.int32)
    large = jnp.minimum(large, half - 1)
    return jnp.where(rel > 0, half, 0) + jnp.where(n < max_exact, n, large)


def _rel_bias(table):
    n_buckets, n_heads = table.shape
    lk = WINDOW + CHUNK
    rel = jnp.arange(lk)[None, :] - WINDOW - jnp.arange(CHUNK)[:, None]
    idx = _t5_bucket(rel, n_buckets).astype(jnp.int32)
    return pl.pallas_call(
        functools.partial(_bias_kernel, n_buckets=n_buckets, n_heads=n_heads),
        in_specs=[pl.BlockSpec((CHUNK, lk), lambda: (0, 0)),
                  pl.BlockSpec(memory_space=pltpu.SMEM)],
        out_specs=pl.BlockSpec((n_heads, CHUNK, lk), lambda: (0, 0, 0)),
        out_shape=jax.ShapeDtypeStruct((n_heads, CHUNK, lk), F32),
        name="bias",
    )(idx, table.astype(F32))


def _attn_kernel(q_ref, k_ref, v_ref, hk_ref, hv_ref, bias_ref, o_ref, kf_scr, vf_scr, *, n_masked, grp):
    nb, t, _ = q_ref.shape
    kv_w = k_ref.shape[2]
    n_chunks = t // CHUNK
    lk = WINDOW + CHUNK
    lkp = bias_ref.shape[2]
    pair_w = 2 * HEAD_DIM
    scale = HEAD_DIM ** -0.5
    low_half = lax.broadcasted_iota(jnp.int32, (1, pair_w), 1) < HEAD_DIM
    ones = jnp.ones((lkp, pair_w), BF16)
    kv_pad = jnp.zeros((lkp - lk, kv_w), BF16)

    kf_scr[:, 0:WINDOW, :] = hk_ref[...].astype(BF16)
    kf_scr[:, WINDOW:, :] = k_ref[...].astype(BF16)
    vf_scr[:, 0:WINDOW, :] = hv_ref[...].astype(BF16)
    vf_scr[:, WINDOW:, :] = v_ref[...].astype(BF16)

    def chunk(n, carry):
        bi = n // n_chunks
        c = n % n_chunks
        r0 = pl.multiple_of(c * CHUNK, CHUNK)
        q = q_ref[bi, pl.ds(r0, CHUNK), :] * scale
        k = jnp.concatenate([kf_scr[bi, pl.ds(r0, lk), :], kv_pad], axis=0)
        v = jnp.concatenate([vf_scr[bi, pl.ds(r0, lk), :], kv_pad], axis=0)
        variant = jnp.minimum(c, n_masked) * N_KV
        scores = []
        for h in range(N_KV):
            j, e = divmod(h, 2)
            qp = jnp.concatenate([q[:, g * kv_w + j * pair_w:g * kv_w + (j + 1) * pair_w]
                                  for g in range(grp)], axis=0)
            in_head = low_half if e == 0 else jnp.logical_not(low_half)
            qh = jnp.where(in_head, qp, jnp.zeros_like(qp))
            scores.append(_dot_t(qh, k[:, j * pair_w:(j + 1) * pair_w]) + bias_ref[variant + h])
        probs = [jnp.exp(s - jnp.max(s, axis=-1, keepdims=True)).astype(BF16) for s in scores]
        v_ext = [jnp.concatenate([v[:, j * pair_w:(j + 1) * pair_w], ones], axis=1) for j in range(N_KV // 2)]
        sums = [_dot(p, v_ext[h // 2]) for h, p in enumerate(probs)]
        outs = [r[:, :pair_w] / r[:, pair_w:] for r in sums]
        for j in range(N_KV // 2):
            o_pair = jnp.where(low_half, outs[2 * j], outs[2 * j + 1])
            for g in range(grp):
                o_ref[bi, pl.ds(r0, CHUNK), g * kv_w + j * pair_w:g * kv_w + (j + 1) * pair_w] = (
                    o_pair[g * CHUNK:(g + 1) * CHUNK].astype(BF16))
        return carry

    lax.fori_loop(0, nb * n_chunks, chunk, 0, unroll=4)


def _band_attention(q3, kv3, hist_k, hist_v, bias, sinks, nb, first_valid):
    b, t, q_w = q3.shape
    kv_w = hist_k.shape[2]
    grp = q_w // HEAD_DIM // N_KV
    lk = WINDOW + CHUNK
    lkp = 2 * LANES
    n_masked = first_valid // CHUNK
    sink_col = jnp.broadcast_to(sinks.astype(F32).reshape(N_KV, grp, 1, 1), (N_KV, grp, CHUNK, 1))
    ext = jnp.concatenate([bias.reshape(N_KV, grp * CHUNK, lk), sink_col.reshape(N_KV, grp * CHUNK, 1),
                           jnp.full((N_KV, grp * CHUNK, lkp - lk - 1), NEG_INF, F32)], axis=2)
    key = jnp.arange(lkp)
    variants = [jnp.where(key < (n_masked - i) * CHUNK, NEG_INF, ext) for i in range(n_masked + 1)]
    bias_ext = jnp.concatenate(variants, axis=0)
    return pl.pallas_call(
        functools.partial(_attn_kernel, n_masked=n_masked, grp=grp),
        grid=(b // nb,),
        in_specs=[pl.BlockSpec((nb, t, q_w), lambda i: (i, 0, 0)),
                  pl.BlockSpec((nb, t, kv_w), lambda i: (i, 0, 0)),
                  pl.BlockSpec((nb, t, kv_w), lambda i: (i, 0, 1)),
                  pl.BlockSpec((nb, WINDOW, kv_w), lambda i: (i, 0, 0)),
                  pl.BlockSpec((nb, WINDOW, kv_w), lambda i: (i, 0, 0)),
                  pl.BlockSpec(bias_ext.shape, lambda i: (0, 0, 0))],
        out_specs=pl.BlockSpec((nb, t, q_w), lambda i: (i, 0, 0)),
        out_shape=jax.ShapeDtypeStruct((b, t, q_w), BF16),
        scratch_shapes=[pltpu.VMEM((nb, WINDOW + t, kv_w), BF16), pltpu.VMEM((nb, WINDOW + t, kv_w), BF16)],
        compiler_params=_params(("parallel",), 40),
        name="attn",
    )(q3, kv3, kv3, hist_k, hist_v, bias_ext)


def _gelu(x):
    return jax.nn.gelu(x)


def _s5_kernel(u_ref, s0re_ref, s0im_ref, are_ref, aim_ref, wbre_ref, wbim_ref, wcre_ref, wcim_ref,
               d_ref, wglu_ref, o_ref, sre_out_ref, sim_out_ref,
               sre_scr, sim_scr, st_re, st_im, y_scr, *, batch, steps, lane_chunk):
    i = pl.program_id(0)

    @pl.when(i == 0)
    def _():
        st_re[...] = s0re_ref[...]
        st_im[...] = s0im_ref[...]

    u = u_ref[...]
    ub = u.astype(BF16)
    n_in_blk, in_blk, st_blk = wbre_ref.shape
    for r in range(n_in_blk):
        blk = ub[:, r * in_blk:(r + 1) * in_blk]
        sre_scr[:, r * st_blk:(r + 1) * st_blk] = _dot(blk, wbre_ref[r])
        sim_scr[:, r * st_blk:(r + 1) * st_blk] = _dot(blk, wbim_ref[r])

    n_state = sre_scr.shape[1]
    for lc in range(n_state // lane_chunk):
        sl = slice(lc * lane_chunk, (lc + 1) * lane_chunk)
        ar = are_ref[:, sl]
        ai = aim_ref[:, sl]

        def body(t, carry, sl=sl, ar=ar, ai=ai):
            sr, si = carry
            rows = pl.ds(pl.multiple_of(t * batch, batch), batch)
            nr = ar * sr - ai * si + sre_scr[rows, sl]
            ni = ar * si + ai * sr + sim_scr[rows, sl]
            sre_scr[rows, sl] = nr
            sim_scr[rows, sl] = ni
            return nr, ni

        sr, si = lax.fori_loop(0, steps, body, (st_re[:, sl], st_im[:, sl]), unroll=8)
        st_re[:, sl] = sr
        st_im[:, sl] = si

    n_out_blk, k_blk, out_blk = wcre_ref.shape
    for kb in range(n_out_blk):
        ksl = slice(kb * k_blk, (kb + 1) * k_blk)
        y_scr[:, kb * out_blk:(kb + 1) * out_blk] = (
            _dot(sre_scr[:, ksl].astype(BF16), wcre_ref[kb])
            + _dot(sim_scr[:, ksl].astype(BF16), wcim_ref[kb]))
    y = _gelu(y_scr[...] + d_ref[...] * u)
    o_ref[...] = (y * jax.nn.sigmoid(_dot(y.astype(BF16), wglu_ref[...]))).astype(BF16)

    @pl.when(i == pl.num_programs(0) - 1)
    def _():
        sre_out_ref[...] = st_re[...]
        sim_out_ref[...] = st_im[...]


def _s5_params(lp):
    lam = lax.complex(lp['ssm_a_re'].astype(F32), lp['ssm_a_im'].astype(F32))
    dt = jnp.exp(lp['ssm_log_dt'].astype(F32))[:, None]
    a_bar = jnp.exp(lam * dt)
    b_mat = lax.complex(lp['ssm_b_re'].astype(F32), lp['ssm_b_im'].astype(F32))
    b_bar = ((a_bar - 1.0) / lam)[..., None] * b_mat
    g, p, c = b_bar.shape
    gb_in = 256 // c
    gb_out = 128 // c

    def b_blocks(x):
        x = x.reshape(g // gb_in, gb_in, p, c).transpose(0, 1, 3, 2)
        x = jnp.einsum('rgcp,gh->rgchp', x, jnp.eye(gb_in, dtype=F32))
        return x.reshape(g // gb_in, gb_in * c, gb_in * p).astype(BF16)

    def c_blocks(x):
        x = x.reshape(g // gb_out, gb_out, c, p).transpose(0, 1, 3, 2)
        x = jnp.einsum('kgpc,gh->kgphc', x, jnp.eye(gb_out, dtype=F32))
        return x.reshape(g // gb_out, gb_out * p, gb_out * c).astype(BF16)

    return dict(a_re=jnp.real(a_bar).reshape(1, g * p), a_im=jnp.imag(a_bar).reshape(1, g * p),
                wb_re=b_blocks(jnp.real(b_bar)), wb_im=b_blocks(jnp.imag(b_bar)),
                wc_re=c_blocks(lp['ssm_c_re'].astype(F32)), wc_im=c_blocks(-lp['ssm_c_im'].astype(F32)),
                d=lp['ssm_d'].astype(F32).reshape(1, -1), w_glu=lp['w_glu'].astype(BF16))


def _s5_layer(u_tb, s0_re, s0_im, sp, steps):
    t, b, ssm_w = u_tb.shape
    n_state = s0_re.shape[1]
    rows = steps * b
    lane_chunk = SCAN_CARRY_ELEMS // b
    a_re = jnp.broadcast_to(sp['a_re'], (b, n_state))
    a_im = jnp.broadcast_to(sp['a_im'], (b, n_state))

    def full(x):
        nd = x.ndim
        return pl.BlockSpec(x.shape, lambda i: (0,) * nd)

    consts = [s0_re, s0_im, a_re, a_im, sp['wb_re'], sp['wb_im'], sp['wc_re'], sp['wc_im'], sp['d'], sp['w_glu']]
    return pl.pallas_call(
        functools.partial(_s5_kernel, batch=b, steps=steps, lane_chunk=lane_chunk),
        grid=(t // steps,),
        in_specs=[pl.BlockSpec((rows, ssm_w), lambda i: (i, 0))] + [full(x) for x in consts],
        out_specs=[pl.BlockSpec((rows, ssm_w), lambda i: (i, 0)),
                   pl.BlockSpec((b, n_state), lambda i: (0, 0)),
                   pl.BlockSpec((b, n_state), lambda i: (0, 0))],
        out_shape=[jax.ShapeDtypeStruct((t * b, ssm_w), BF16),
                   jax.ShapeDtypeStruct((b, n_state), F32),
                   jax.ShapeDtypeStruct((b, n_state), F32)],
        scratch_shapes=[pltpu.VMEM((rows, n_state), F32), pltpu.VMEM((rows, n_state), F32),
                        pltpu.VMEM((b, n_state), F32), pltpu.VMEM((b, n_state), F32),
                        pltpu.VMEM((rows, ssm_w), F32)],
        compiler_params=_params(("arbitrary",), 48),
        name="s5",
    )(u_tb.reshape(t * b, ssm_w), *consts)


def _memattn_kernel(q_ref, k_ref, v_ref, o_ref, *, head_dim):
    q = q_ref[...]
    k = k_ref[...].astype(BF16)
    v = v_ref[...].astype(BF16)
    scale = head_dim ** -0.5
    ones = jnp.ones((v.shape[0], head_dim), BF16)
    heads = [slice(h * head_dim, (h + 1) * head_dim) for h in range(MEM_HEADS)]
    scores = [_dot_t(q[:, sl], k[:, sl]) * scale for sl in heads]
    probs = [jnp.exp(s - jnp.max(s, axis=-1, keepdims=True)).astype(BF16) for s in scores]
    sums = [_dot(p, jnp.concatenate([v[:, sl], ones], axis=1)) for p, sl in zip(probs, heads)]
    for r, sl in zip(sums, heads):
        o_ref[:, sl] = (r[:, :head_dim] / r[:, head_dim:]).astype(BF16)


def _memory_attention(qm, mem_k, mem_v, b, t, tq):
    n_mem, mem_w = mem_k.shape[1:]
    nt = t // tq
    return pl.pallas_call(
        functools.partial(_memattn_kernel, head_dim=mem_w // MEM_HEADS),
        grid=(b, nt),
        in_specs=[pl.BlockSpec((tq, mem_w), lambda bi, ti: (bi * nt + ti, 0)),
                  pl.BlockSpec((None, n_mem, mem_w), lambda bi, ti: (bi, 0, 0)),
                  pl.BlockSpec((None, n_mem, mem_w), lambda bi, ti: (bi, 0, 0))],
        out_specs=pl.BlockSpec((tq, mem_w), lambda bi, ti: (bi * nt + ti, 0)),
        out_shape=jax.ShapeDtypeStruct((b * t, mem_w), BF16),
        compiler_params=_params(("parallel", "parallel"), 32),
        name="memattn",
    )(qm, mem_k, mem_v)


def _mix_kernel(x_ref, oa_ref, os_ref, om_ref, gpre_ref, wg0_ref, wg1_ref, wg2_ref,
                wa_ref, ws_ref, wm_ref, gpost_ref, o_ref, h_scr, ss_scr, *, tn):
    j = pl.program_id(1)

    @pl.when(j == 0)
    def _():
        h_scr[...] = _rms(x_ref[...], gpre_ref[...]).astype(BF16)
        ss_scr[...] = jnp.zeros_like(ss_scr)

    h = h_scr[...]
    merged = (jax.nn.sigmoid(_dot(h, wg0_ref[...])) * _dot(oa_ref[...], wa_ref[...])
              + jax.nn.sigmoid(_dot(h, wg1_ref[...])) * _dot(os_ref[...], ws_ref[...])
              + jax.nn.sigmoid(_dot(h, wg2_ref[...])) * _dot(om_ref[...], wm_ref[...]))
    o_ref[:, pl.ds(pl.multiple_of(j * tn, tn), tn)] = merged
    ss_scr[...] += jnp.sum(merged * merged, axis=-1, keepdims=True)

    @pl.when(j == pl.num_programs(1) - 1)
    def _():
        inv = lax.rsqrt(ss_scr[...] / o_ref.shape[1] + EPS)
        o_ref[...] = x_ref[...] + (o_ref[...] * inv) * gpost_ref[...]


def _mix_residual(x2d, o_a, o_s, o_m, g_pre, w_in_bf, gate_col0, w_oa, w_out_bf, g_post, tm, tn):
    m, d = x2d.shape
    nj = d // tn
    q_w, ssm_w, mem_w = o_a.shape[1], o_s.shape[1], o_m.shape[1]
    assert gate_col0 % tn == 0 and q_w % ssm_w == 0 and (q_w + ssm_w) % mem_w == 0
    gate_blk0 = gate_col0 // tn

    def rows(w):
        mode = dict(pipeline_mode=pl.Buffered(1)) if m == tm else {}
        return pl.BlockSpec((tm, w), lambda i, j: (i, 0), **mode)

    def gate_spec(br):
        return pl.BlockSpec((d, tn), lambda i, j: (0, gate_blk0 + br * nj + j))

    vec = pl.BlockSpec((1, d), lambda i, j: (0, 0))
    return pl.pallas_call(
        functools.partial(_mix_kernel, tn=tn),
        grid=(m // tm, nj),
        in_specs=[rows(d), rows(q_w), rows(ssm_w), rows(mem_w), vec,
                  gate_spec(0), gate_spec(1), gate_spec(2),
                  pl.BlockSpec((q_w, tn), lambda i, j: (0, j)),
                  pl.BlockSpec((ssm_w, tn), lambda i, j: (q_w // ssm_w, j)),
                  pl.BlockSpec((mem_w, tn), lambda i, j: ((q_w + ssm_w) // mem_w, j)), vec],
        out_specs=rows(d),
        out_shape=jax.ShapeDtypeStruct((m, d), F32),
        scratch_shapes=[pltpu.VMEM((tm, d), BF16), pltpu.VMEM((tm, 1), F32)],
        compiler_params=_params(("parallel", "arbitrary"), 56),
        name="mix",
    )(x2d, o_a, o_s, o_m, g_pre.reshape(1, d), w_in_bf, w_in_bf, w_in_bf, w_oa, w_out_bf, w_out_bf,
      g_post.reshape(1, d))


def _ffn_kernel(x_ref, gpre_ref, wua_ref, wub_ref, cw_ref, cb_ref, wd_ref, gpost_ref, cprev_ref,
                o_ref, tail_ref, h_scr, a_scr, b_scr, carry_scr):
    i = pl.program_id(1)
    j = pl.program_id(2)
    nb, tt, d = x_ref.shape
    tf = wua_ref.shape[1]
    rows = nb * tt

    @pl.when(j == 0)
    def _():
        h_scr[...] = _rms(x_ref[...].reshape(rows, d), gpre_ref[...]).astype(BF16)
        o_ref[...] = jnp.zeros_like(o_ref)

    h = h_scr[...]
    a_scr[:, 0:SUBLANES, :] = jnp.where(i == 0, cprev_ref[...], carry_scr[j])
    n_slices = tf // MXU_WIDTH
    cols = [slice(s * MXU_WIDTH, (s + 1) * MXU_WIDTH) for s in range(n_slices)]
    for sl in cols:
        a_scr[:, SUBLANES:, sl] = _dot(h, wua_ref[:, sl]).reshape(nb, tt, MXU_WIDTH)
        b_scr[:, sl] = _dot(h, wub_ref[:, sl])
    cw = cw_ref[...]
    cb = cb_ref[...]
    for sl in cols:
        gated = []
        for bi in range(nb):
            a = a_scr[bi, SUBLANES:, sl]
            first = jnp.concatenate([a_scr[bi, 0:SUBLANES, sl], a[0:SUBLANES]], axis=0)
            prev1 = jnp.concatenate([first[SUBLANES - 1:2 * SUBLANES - 1],
                                     pltpu.roll(a, 1, 0)[SUBLANES:]], axis=0)
            prev2 = jnp.concatenate([first[SUBLANES - 2:2 * SUBLANES - 2],
                                     pltpu.roll(a, 2, 0)[SUBLANES:]], axis=0)
            conv = (prev2 * cw[0:1, sl] + prev1 * cw[1:2, sl] + a * cw[2:3, sl]) + cb[:, sl]
            gated.append(_gelu(conv) * b_scr[bi * tt:(bi + 1) * tt, sl])
        act = (gated[0] if nb == 1 else jnp.concatenate(gated, axis=0)).astype(BF16)
        o_ref[...] += _dot(act, wd_ref[sl, :]).reshape(nb, tt, d)
    tail = a_scr[:, tt:tt + SUBLANES, :]
    carry_scr[j] = tail
    tail_ref[...] = tail

    @pl.when(j == pl.num_programs(2) - 1)
    def _():
        f = _rms(o_ref[...].reshape(rows, d), gpost_ref[...]).reshape(nb, tt, d)
        o_ref[...] = x_ref[...] + f


def _conv_ffn(x3d, g_pre, w_up, conv_w, conv_b, w_down, g_post, conv_prev8, nb, tt, tf):
    b, t, d = x3d.shape
    d_ff = w_down.shape[0]
    nj = d_ff // tf
    nt = t // tt
    mode = dict(pipeline_mode=pl.Buffered(1)) if (b // nb) * nt == 1 else {}
    return pl.pallas_call(
        _ffn_kernel,
        grid=(b // nb, nt, nj),
        in_specs=[pl.BlockSpec((nb, tt, d), lambda bi, i, j: (bi, i, 0), **mode),
                  pl.BlockSpec((1, d), lambda bi, i, j: (0, 0)),
                  pl.BlockSpec((d, tf), lambda bi, i, j: (0, j)),
                  pl.BlockSpec((d, tf), lambda bi, i, j: (0, nj + j)),
                  pl.BlockSpec((CONV_W, tf), lambda bi, i, j: (0, j)),
                  pl.BlockSpec((1, tf), lambda bi, i, j: (0, j)),
                  pl.BlockSpec((tf, d), lambda bi, i, j: (j, 0)),
                  pl.BlockSpec((1, d), lambda bi, i, j: (0, 0)),
                  pl.BlockSpec((nb, SUBLANES, tf), lambda bi, i, j: (bi, 0, j))],
        out_specs=[pl.BlockSpec((nb, tt, d), lambda bi, i, j: (bi, i, 0), **mode),
                   pl.BlockSpec((nb, None, SUBLANES, tf), lambda bi, i, j: (bi, i, 0, j))],
        out_shape=[jax.ShapeDtypeStruct((b, t, d), F32),
                   jax.ShapeDtypeStruct((b, nt, SUBLANES, d_ff), F32)],
        scratch_shapes=[pltpu.VMEM((nb * tt, d), BF16),
                        pltpu.VMEM((nb, tt + SUBLANES, tf), F32), pltpu.VMEM((nb * tt, tf), F32),
                        pltpu.VMEM((nj, nb, SUBLANES, tf), F32)],
        compiler_params=_params(("arbitrary", "arbitrary", "arbitrary"), 56),
        name="ffn",
    )(x3d, g_pre.reshape(1, d), w_up, w_up, conv_w.astype(F32), conv_b.astype(F32).reshape(1, d_ff),
      w_down, g_post.reshape(1, d), conv_prev8)


def _tiles(b, t):
    row_tile = 1024 if b * t <= 1024 else 512
    tt = min(t, row_tile)
    return dict(
        proj_rows=512,
        attn_batch=max(1, min(b, 1024 // t)),
        s5_steps=min(t, 512 // b),
        memattn_rows=tt,
        mix_rows=row_tile, mix_cols=512,
        ffn_batch=min(b, row_tile // tt), ffn_rows=tt)


FFN_COLS = 2 * MXU_WIDTH


def _layer(x, attn_past, s0, conv_prev, mem_k, mem_v, bias, lw):
    b, t, d = x.shape
    m = b * t
    q_w, kv_w, ssm_w, mem_w = lw['q_w'], lw['kv_w'], lw['ssm_w'], lw['mem_w']
    n_state = lw['n_state']
    d_ff = lw['w_down'].shape[0]
    tiles = _tiles(b, t)
    x2d = x.reshape(m, d)

    q, kv, u, qm = _norm_proj(x2d, lw['norm_pre_mix'], lw['proj_outputs'], tiles['proj_rows'])
    kv3 = kv.reshape(b, t, 2 * kv_w)
    if attn_past is None:
        hist_k = jnp.zeros((b, WINDOW, kv_w), F32)
        hist_v = hist_k
        first_valid = WINDOW
    else:
        hist_k = attn_past[0].astype(F32).reshape(b, WINDOW, kv_w)
        hist_v = attn_past[1].astype(F32).reshape(b, WINDOW, kv_w)
        first_valid = 0
    o_a = _band_attention(q.reshape(b, t, q_w), kv3, hist_k, hist_v, bias, lw['attn_sinks'],
                          tiles['attn_batch'], first_valid).reshape(m, q_w)

    if s0 is None:
        s0_re = jnp.zeros((b, n_state), F32)
        s0_im = s0_re
    else:
        s0_re = s0[0].astype(F32).reshape(b, n_state)
        s0_im = s0[1].astype(F32).reshape(b, n_state)
    u_tb = u.reshape(b, t, ssm_w).transpose(1, 0, 2)
    o_s_tb, s_re, s_im = _s5_layer(u_tb, s0_re, s0_im, lw['s5'], tiles['s5_steps'])
    o_s = o_s_tb.reshape(t, b, ssm_w).transpose(1, 0, 2).reshape(m, ssm_w)

    o_m = _memory_attention(qm, mem_k, mem_v, b, t, tiles['memattn_rows'])

    x1 = _mix_residual(x2d, o_a, o_s, o_m, lw['norm_pre_mix'], lw['w_in_bf'], lw['gate_col0'], lw['w_oa'],
                       lw['w_out_bf'], lw['norm_post_mix'], tiles['mix_rows'], tiles['mix_cols'])

    if conv_prev is None:
        conv_prev8 = jnp.zeros((b, SUBLANES, d_ff), F32)
    else:
        conv_prev8 = jnp.pad(conv_prev.astype(F32), ((0, 0), (SUBLANES - (CONV_W - 1), 0), (0, 0)))
    x2, tails = _conv_ffn(x1.reshape(b, t, d), lw['norm_pre_ffn'], lw['w_up'], lw['conv_w'], lw['conv_b'],
                          lw['w_down'], lw['norm_post_ffn'], conv_prev8,
                          tiles['ffn_batch'], tiles['ffn_rows'], FFN_COLS)
    conv_new = tails[:, -1, SUBLANES - (CONV_W - 1):, :]

    n_kv = kv_w // HEAD_DIM
    k_new = jnp.concatenate([hist_k, kv3[:, :, :kv_w]], axis=1)[:, -WINDOW:].reshape(b, WINDOW, n_kv, HEAD_DIM)
    v_new = jnp.concatenate([hist_v, kv3[:, :, kv_w:]], axis=1)[:, -WINDOW:].reshape(b, WINDOW, n_kv, HEAD_DIM)
    return x2, k_new, v_new, s_re, s_im, conv_new


def kernel(x_prompt, x_sample, cache_attn_k, cache_attn_v, cache_mem_k, cache_mem_v, state_ssm_re, state_ssm_im, state_conv, mem_prompt, rel_bias_table, norm_pre_mix, norm_post_mix, norm_pre_ffn, norm_post_ffn, norm_mem, w_in, attn_sinks, ssm_a_re, ssm_a_im, ssm_log_dt, ssm_b_re, ssm_b_im, ssm_c_re, ssm_c_im, ssm_d, w_glu, w_mem_kv, w_out, w_up, conv_w, conv_b, w_down):
    depth = w_in.shape[0]
    bp, _, d = x_prompt.shape
    n_mem = mem_prompt.shape[1]
    n_q = attn_sinks.shape[1]
    n_kv, hd = cache_attn_k.shape[-2:]
    assert hd == HEAD_DIM and n_kv == N_KV and cache_attn_k.shape[2] == WINDOW
    groups, p_state = ssm_a_re.shape[1:]
    q_w, kv_w = n_q * HEAD_DIM, n_kv * HEAD_DIM
    ssm_w = ssm_d.shape[1]
    mem_w = w_mem_kv.shape[2] // 2
    proj_w = q_w + 2 * kv_w + ssm_w + mem_w
    mem_hd = mem_w // MEM_HEADS

    xp, xs = x_prompt, x_sample
    outs = [[] for _ in range(12)]
    for l in range(depth):
        bias = _rel_bias(rel_bias_table)
        lp = dict(ssm_a_re=ssm_a_re[l], ssm_a_im=ssm_a_im[l], ssm_log_dt=ssm_log_dt[l],
                  ssm_b_re=ssm_b_re[l], ssm_b_im=ssm_b_im[l], ssm_c_re=ssm_c_re[l], ssm_c_im=ssm_c_im[l],
                  ssm_d=ssm_d[l], w_glu=w_glu[l])
        w_in_bf = w_in[l].astype(BF16)
        w_out_bf = w_out[l].astype(BF16)
        grp = n_q // n_kv
        w_q = w_in_bf[:, :q_w].reshape(d, n_kv, grp, HEAD_DIM).transpose(0, 2, 1, 3).reshape(d, q_w)
        w_oa = w_out_bf[:q_w].reshape(n_kv, grp, HEAD_DIM, d).transpose(1, 0, 2, 3).reshape(q_w, d)
        u_off = q_w + 2 * kv_w
        proj_outputs = [(BF16, [(w_q, q_w, 0)]),
                        (F32, _col_blocks(w_in_bf, q_w, 2 * kv_w)),
                        (F32, _col_blocks(w_in_bf, u_off, ssm_w)),
                        (BF16, _col_blocks(w_in_bf, u_off + ssm_w, mem_w))]
        lw = dict(q_w=q_w, kv_w=kv_w, ssm_w=ssm_w, mem_w=mem_w, n_state=groups * p_state,
                  norm_pre_mix=norm_pre_mix[l], norm_post_mix=norm_post_mix[l],
                  norm_pre_ffn=norm_pre_ffn[l], norm_post_ffn=norm_post_ffn[l],
                  proj_outputs=proj_outputs, w_in_bf=w_in_bf, gate_col0=proj_w, w_oa=w_oa, w_out_bf=w_out_bf,
                  attn_sinks=attn_sinks[l], s5=_s5_params(lp),
                  w_up=w_up[l].astype(BF16),
                  conv_w=conv_w[l], conv_b=conv_b[l], w_down=w_down[l].astype(BF16))

        w_mem_bf = w_mem_kv[l].astype(BF16)
        mk_p, mv_p = _norm_proj(mem_prompt.reshape(bp * n_mem, d), norm_mem[l],
                                [(F32, [(w_mem_bf, mem_w, 0)]), (F32, [(w_mem_bf, mem_w, 1)])],
                                _tiles(bp, n_mem)['proj_rows'])
        mk_p = mk_p.reshape(bp, n_mem, mem_w)
        mv_p = mv_p.reshape(bp, n_mem, mem_w)
        xp, k_p, v_p, sr_p, si_p, c_p = _layer(xp, None, None, None, mk_p, mv_p, bias, lw)

        bs = xs.shape[0]
        xs, k_s, v_s, sr_s, si_s, c_s = _layer(
            xs, (cache_attn_k[l], cache_attn_v[l]), (state_ssm_re[l], state_ssm_im[l]), state_conv[l],
            cache_mem_k[l].reshape(bs, n_mem, mem_w), cache_mem_v[l].reshape(bs, n_mem, mem_w), bias, lw)

        vals = (k_p, v_p, sr_p.reshape(bp, groups, p_state), si_p.reshape(bp, groups, p_state), c_p,
                mk_p.reshape(bp, n_mem, MEM_HEADS, mem_hd), mv_p.reshape(bp, n_mem, MEM_HEADS, mem_hd),
                k_s, v_s, sr_s.reshape(bs, groups, p_state), si_s.reshape(bs, groups, p_state), c_s)
        for acc, val in zip(outs, vals):
            acc.append(val)
    return (xp, xs) + tuple(jnp.stack(o) for o in outs)
```

```python
import functools
import math

import jax
import jax.numpy as jnp
from jax import lax
from jax.experimental import pallas as pl
from jax.experimental.pallas import tpu as pltpu

F32 = jnp.float32
BF16 = jnp.bfloat16

EPS = 1e-6
NEG_INF = -1e30
CHUNK = 64
WINDOW = 128
HEAD_DIM = 64
N_KV = 4
MAX_DISTANCE = 128
SSM_GROUP_CH = 16
MEM_HEADS = 4
CONV_W = 3

MIB = 1024 * 1024
LANES = 128
MXU_WIDTH = 256
SUBLANES = 8
SCAN_CARRY_ELEMS = 4096


def _params(semantics, vmem_mib):
    return pltpu.CompilerParams(dimension_semantics=semantics, vmem_limit_bytes=vmem_mib * MIB)


def _rms(x, g):
    y = x * lax.rsqrt(jnp.mean(x * x, axis=-1, keepdims=True) + EPS)
    return y * g


def _dot(a, b):
    return jnp.dot(a, b, preferred_element_type=F32)


def _dot_t(a, b):
    return lax.dot_general(a, b, (((1,), (1,)), ((), ())), preferred_element_type=F32)


def _proj_kernel(x_ref, g_ref, *refs, blocks_per_out):
    n_w = sum(blocks_per_out)
    w_refs, o_refs = refs[:n_w], refs[n_w:]
    h = _rms(x_ref[...], g_ref[...]).astype(BF16)
    first = 0
    for o_ref, count in zip(o_refs, blocks_per_out):
        if count == 0:
            o_ref[...] = h
            continue
        parts = [_dot(h, w_ref[...]) for w_ref in w_refs[first:first + count]]
        r = parts[0] if count == 1 else jnp.concatenate(parts, axis=1)
        o_ref[...] = r.astype(o_ref.dtype)
        first += count


def _col_blocks(arr, start, width):
    blk = math.gcd(start, width) if start else width
    assert blk % LANES == 0
    return [(arr, blk, start // blk + k) for k in range(width // blk)]


def _norm_proj(x2d, g, outputs, tm):
    m, d = x2d.shape
    w_args, w_specs, out_specs, out_shapes = [], [], [], []
    for dt, blocks in outputs:
        for arr, width, blk in blocks:
            assert arr.shape[0] == d and arr.shape[1] % width == 0
            w_args.append(arr)
            w_specs.append(pl.BlockSpec((d, width), lambda i, blk=blk: (0, blk)))
        total = sum(width for _, width, _ in blocks) if blocks else d
        out_specs.append(pl.BlockSpec((tm, total), lambda i: (i, 0)))
        out_shapes.append(jax.ShapeDtypeStruct((m, total), dt))
    return pl.pallas_call(
        functools.partial(_proj_kernel, blocks_per_out=tuple(len(b) for _, b in outputs)),
        grid=(m // tm,),
        in_specs=[pl.BlockSpec((tm, d), lambda i: (i, 0)),
                  pl.BlockSpec((1, d), lambda i: (0, 0))] + w_specs,
        out_specs=out_specs,
        out_shape=out_shapes,
        compiler_params=_params(("parallel",), 48),
        name="proj",
    )(x2d, g.reshape(1, d), *w_args)


def _bias_kernel(idx_ref, table_ref, o_ref, *, n_buckets, n_heads):
    idx = idx_ref[...]
    for h in range(n_heads):
        acc = jnp.zeros(idx.shape, F32)
        for b in range(n_buckets):
            acc = jnp.where(idx == b, table_ref[b, h], acc)
        o_ref[h] = acc


def _t5_bucket(rel, n_buckets):
    half = n_buckets // 2
    max_exact = half // 2
    n = jnp.abs(rel)
    large = max_exact + (jnp.log(jnp.maximum(n, 1).astype(F32) / max_exact)
                         / math.log(MAX_DISTANCE / max_exact) * (half - max_exact)).astype(jnp.int32)
    large = jnp.minimum(large, half - 1)
    return jnp.where(rel > 0, half, 0) + jnp.where(n < max_exact, n, large)


def _rel_bias(table):
    n_buckets, n_heads = table.shape
    lk = WINDOW + CHUNK
    rel = jnp.arange(lk)[None, :] - WINDOW - jnp.arange(CHUNK)[:, None]
    idx = _t5_bucket(rel, n_buckets).astype(jnp.int32)
    return pl.pallas_call(
        functools.partial(_bias_kernel, n_buckets=n_buckets, n_heads=n_heads),
        in_specs=[pl.BlockSpec((CHUNK, lk), lambda: (0, 0)),
                  pl.BlockSpec(memory_space=pltpu.SMEM)],
        out_specs=pl.BlockSpec((n_heads, CHUNK, lk), lambda: (0, 0, 0)),
        out_shape=jax.ShapeDtypeStruct((n_heads, CHUNK, lk), F32),
        name="bias",
    )(idx, table.astype(F32))


def _attn_kernel(q_ref, k_ref, v_ref, hk_ref, hv_ref, bias_ref, o_ref, kf_scr, vf_scr, s_scr, *, n_masked, grp):
    nb, t, _ = q_ref.shape
    kv_w = k_ref.shape[2]
    n_chunks = t // CHUNK
    lk = WINDOW + CHUNK
    lkp = bias_ref.shape[2]
    pair_w = 2 * HEAD_DIM
    scale = HEAD_DIM ** -0.5
    low_half = lax.broadcasted_iota(jnp.int32, (1, pair_w), 1) < HEAD_DIM
    ones = jnp.ones((lkp, pair_w), BF16)
    kv_pad = jnp.zeros((lkp - lk, kv_w), BF16)

    kf_scr[:, 0:WINDOW, :] = hk_ref[...].astype(BF16)
    kf_scr[:, WINDOW:, :] = k_ref[...].astype(BF16)
    vf_scr[:, 0:WINDOW, :] = hv_ref[...].astype(BF16)
    vf_scr[:, WINDOW:, :] = v_ref[...].astype(BF16)

    n_total = nb * n_chunks

    def locate(n):
        bi = n // n_chunks
        c = n % n_chunks
        return bi, c, pl.multiple_of(c * CHUNK, CHUNK)

    def score_chunk(n, slot):
        bi, c, r0 = locate(n)
        q = q_ref[bi, pl.ds(r0, CHUNK), :] * scale
        k = jnp.concatenate([kf_scr[bi, pl.ds(r0, lk), :], kv_pad], axis=0)
        variant = jnp.minimum(c, n_masked) * N_KV
        for h in range(N_KV):
            j, e = divmod(h, 2)
            qp = jnp.concatenate([q[:, g * kv_w + j * pair_w:g * kv_w + (j + 1) * pair_w]
                                  for g in range(grp)], axis=0)
            in_head = low_half if e == 0 else jnp.logical_not(low_half)
            qh = jnp.where(in_head, qp, jnp.zeros_like(qp))
            s_scr[slot, h] = _dot_t(qh, k[:, j * pair_w:(j + 1) * pair_w]) + bias_ref[variant + h]

    def chunk(n, slot):
        score_chunk(jnp.minimum(n + 1, n_total - 1), 1 - slot)
        bi, c, r0 = locate(n)
        v = jnp.concatenate([vf_scr[bi, pl.ds(r0, lk), :], kv_pad], axis=0)
        scores = [s_scr[slot, h] for h in range(N_KV)]
        probs = [jnp.exp(s - jnp.max(s, axis=-1, keepdims=True)).astype(BF16) for s in scores]
        v_ext = [jnp.concatenate([v[:, j * pair_w:(j + 1) * pair_w], ones], axis=1) for j in range(N_KV // 2)]
        sums = [_dot(p, v_ext[h // 2]) for h, p in enumerate(probs)]
        outs = [r[:, :pair_w] / r[:, pair_w:] for r in sums]
        for j in range(N_KV // 2):
            o_pair = jnp.where(low_half, outs[2 * j], outs[2 * j + 1])
            for g in range(grp):
                o_ref[bi, pl.ds(r0, CHUNK), g * kv_w + j * pair_w:g * kv_w + (j + 1) * pair_w] = (
                    o_pair[g * CHUNK:(g + 1) * CHUNK].astype(BF16))

    def chunk_pair(p, carry):
        chunk(2 * p, 0)
        chunk(2 * p + 1, 1)
        return carry

    assert n_total % 2 == 0
    score_chunk(0, 0)
    lax.fori_loop(0, n_total // 2, chunk_pair, 0)


def _band_attention(q3, kv3, hist_k, hist_v, bias, sinks, nb, first_valid):
    b, t, q_w = q3.shape
    kv_w = hist_k.shape[2]
    grp = q_w // HEAD_DIM // N_KV
    lk = WINDOW + CHUNK
    lkp = 2 * LANES
    n_masked = first_valid // CHUNK
    sink_col = jnp.broadcast_to(sinks.astype(F32).reshape(N_KV, grp, 1, 1), (N_KV, grp, CHUNK, 1))
    ext = jnp.concatenate([bias.reshape(N_KV, grp * CHUNK, lk), sink_col.reshape(N_KV, grp * CHUNK, 1),
                           jnp.full((N_KV, grp * CHUNK, lkp - lk - 1), NEG_INF, F32)], axis=2)
    key = jnp.arange(lkp)
    variants = [jnp.where(key < (n_masked - i) * CHUNK, NEG_INF, ext) for i in range(n_masked + 1)]
    bias_ext = jnp.concatenate(variants, axis=0)
    return pl.pallas_call(
        functools.partial(_attn_kernel, n_masked=n_masked, grp=grp),
        grid=(b // nb,),
        in_specs=[pl.BlockSpec((nb, t, q_w), lambda i: (i, 0, 0)),
                  pl.BlockSpec((nb, t, kv_w), lambda i: (i, 0, 0)),
                  pl.BlockSpec((nb, t, kv_w), lambda i: (i, 0, 1)),
                  pl.BlockSpec((nb, WINDOW, kv_w), lambda i: (i, 0, 0)),
                  pl.BlockSpec((nb, WINDOW, kv_w), lambda i: (i, 0, 0)),
                  pl.BlockSpec(bias_ext.shape, lambda i: (0, 0, 0))],
        out_specs=pl.BlockSpec((nb, t, q_w), lambda i: (i, 0, 0)),
        out_shape=jax.ShapeDtypeStruct((b, t, q_w), BF16),
        scratch_shapes=[pltpu.VMEM((nb, WINDOW + t, kv_w), BF16), pltpu.VMEM((nb, WINDOW + t, kv_w), BF16),
                        pltpu.VMEM((2, N_KV, grp * CHUNK, lkp), F32)],
        compiler_params=_params(("parallel",), 40),
        name="attn",
    )(q3, kv3, kv3, hist_k, hist_v, bias_ext)


def _gelu(x):
    return jax.nn.gelu(x)


def _s5_kernel(u_ref, s0re_ref, s0im_ref, are_ref, aim_ref, wbre_ref, wbim_ref, wcre_ref, wcim_ref,
               d_ref, wglu_ref, o_ref, sre_out_ref, sim_out_ref,
               sre_scr, sim_scr, st_re, st_im, y_scr, *, batch, steps, lane_chunk):
    i = pl.program_id(0)

    @pl.when(i == 0)
    def _():
        st_re[...] = s0re_ref[...]
        st_im[...] = s0im_ref[...]

    u = u_ref[...]
    ub = u.astype(BF16)
    n_in_blk, in_blk, st_blk = wbre_ref.shape
    for r in range(n_in_blk):
        blk = ub[:, r * in_blk:(r + 1) * in_blk]
        sre_scr[:, r * st_blk:(r + 1) * st_blk] = _dot(blk, wbre_ref[r])
        sim_scr[:, r * st_blk:(r + 1) * st_blk] = _dot(blk, wbim_ref[r])

    n_state = sre_scr.shape[1]
    for lc in range(n_state // lane_chunk):
        sl = slice(lc * lane_chunk, (lc + 1) * lane_chunk)
        ar = are_ref[:, sl]
        ai = aim_ref[:, sl]

        def body(t, carry, sl=sl, ar=ar, ai=ai):
            sr, si = carry
            rows = pl.ds(pl.multiple_of(t * batch, batch), batch)
            nr = ar * sr - ai * si + sre_scr[rows, sl]
            ni = ar * si + ai * sr + sim_scr[rows, sl]
            sre_scr[rows, sl] = nr
            sim_scr[rows, sl] = ni
            return nr, ni

        sr, si = lax.fori_loop(0, steps, body, (st_re[:, sl], st_im[:, sl]), unroll=8)
        st_re[:, sl] = sr
        st_im[:, sl] = si

    n_out_blk, k_blk, out_blk = wcre_ref.shape
    for kb in range(n_out_blk):
        ksl = slice(kb * k_blk, (kb + 1) * k_blk)
        y_scr[:, kb * out_blk:(kb + 1) * out_blk] = (
            _dot(sre_scr[:, ksl].astype(BF16), wcre_ref[kb])
            + _dot(sim_scr[:, ksl].astype(BF16), wcim_ref[kb]))
    y = _gelu(y_scr[...] + d_ref[...] * u)
    o_ref[...] = (y * jax.nn.sigmoid(_dot(y.astype(BF16), wglu_ref[...]))).astype(BF16)

    @pl.when(i == pl.num_programs(0) - 1)
    def _():
        sre_out_ref[...] = st_re[...]
        sim_out_ref[...] = st_im[...]


def _s5_params(lp):
    lam = lax.complex(lp['ssm_a_re'].astype(F32), lp['ssm_a_im'].astype(F32))
    dt = jnp.exp(lp['ssm_log_dt'].astype(F32))[:, None]
    a_bar = jnp.exp(lam * dt)
    b_mat = lax.complex(lp['ssm_b_re'].astype(F32), lp['ssm_b_im'].astype(F32))
    b_bar = ((a_bar - 1.0) / lam)[..., None] * b_mat
    g, p, c = b_bar.shape
    gb_in = 256 // c
    gb_out = 128 // c

    def b_blocks(x):
        x = x.reshape(g // gb_in, gb_in, p, c).transpose(0, 1, 3, 2)
        x = jnp.einsum('rgcp,gh->rgchp', x, jnp.eye(gb_in, dtype=F32))
        return x.reshape(g // gb_in, gb_in * c, gb_in * p).astype(BF16)

    def c_blocks(x):
        x = x.reshape(g // gb_out, gb_out, c, p).transpose(0, 1, 3, 2)
        x = jnp.einsum('kgpc,gh->kgphc', x, jnp.eye(gb_out, dtype=F32))
        return x.reshape(g // gb_out, gb_out * p, gb_out * c).astype(BF16)

    return dict(a_re=jnp.real(a_bar).reshape(1, g * p), a_im=jnp.imag(a_bar).reshape(1, g * p),
                wb_re=b_blocks(jnp.real(b_bar)), wb_im=b_blocks(jnp.imag(b_bar)),
                wc_re=c_blocks(lp['ssm_c_re'].astype(F32)), wc_im=c_blocks(-lp['ssm_c_im'].astype(F32)),
                d=lp['ssm_d'].astype(F32).reshape(1, -1), w_glu=lp['w_glu'].astype(BF16))


def _s5_layer(u_tb, s0_re, s0_im, sp, steps):
    t, b, ssm_w = u_tb.shape
    n_state = s0_re.shape[1]
    rows = steps * b
    lane_chunk = SCAN_CARRY_ELEMS // b
    a_re = jnp.broadcast_to(sp['a_re'], (b, n_state))
    a_im = jnp.broadcast_to(sp['a_im'], (b, n_state))

    def full(x):
        nd = x.ndim
        return pl.BlockSpec(x.shape, lambda i: (0,) * nd)

    consts = [s0_re, s0_im, a_re, a_im, sp['wb_re'], sp['wb_im'], sp['wc_re'], sp['wc_im'], sp['d'], sp['w_glu']]
    return pl.pallas_call(
        functools.partial(_s5_kernel, batch=b, steps=steps, lane_chunk=lane_chunk),
        grid=(t // steps,),
        in_specs=[pl.BlockSpec((rows, ssm_w), lambda i: (i, 0))] + [full(x) for x in consts],
        out_specs=[pl.BlockSpec((rows, ssm_w), lambda i: (i, 0)),
                   pl.BlockSpec((b, n_state), lambda i: (0, 0)),
                   pl.BlockSpec((b, n_state), lambda i: (0, 0))],
        out_shape=[jax.ShapeDtypeStruct((t * b, ssm_w), BF16),
                   jax.ShapeDtypeStruct((b, n_state), F32),
                   jax.ShapeDtypeStruct((b, n_state), F32)],
        scratch_shapes=[pltpu.VMEM((rows, n_state), F32), pltpu.VMEM((rows, n_state), F32),
                        pltpu.VMEM((b, n_state), F32), pltpu.VMEM((b, n_state), F32),
                        pltpu.VMEM((rows, ssm_w), F32)],
        compiler_params=_params(("arbitrary",), 48),
        name="s5",
    )(u_tb.reshape(t * b, ssm_w), *consts)


def _memattn_kernel(q_ref, k_ref, v_ref, o_ref, *, head_dim):
    q = q_ref[...]
    k = k_ref[...].astype(BF16)
    v = v_ref[...].astype(BF16)
    scale = head_dim ** -0.5
    ones = jnp.ones((v.shape[0], head_dim), BF16)
    heads = [slice(h * head_dim, (h + 1) * head_dim) for h in range(MEM_HEADS)]
    scores = [_dot_t(q[:, sl], k[:, sl]) * scale for sl in heads]
    probs = [jnp.exp(s - jnp.max(s, axis=-1, keepdims=True)).astype(BF16) for s in scores]
    sums = [_dot(p, jnp.concatenate([v[:, sl], ones], axis=1)) for p, sl in zip(probs, heads)]
    for r, sl in zip(sums, heads):
        o_ref[:, sl] = (r[:, :head_dim] / r[:, head_dim:]).astype(BF16)


def _memory_attention(qm, mem_k, mem_v, b, t, tq):
    n_mem, mem_w = mem_k.shape[1:]
    nt = t // tq
    return pl.pallas_call(
        functools.partial(_memattn_kernel, head_dim=mem_w // MEM_HEADS),
        grid=(b, nt),
        in_specs=[pl.BlockSpec((tq, mem_w), lambda bi, ti: (bi * nt + ti, 0)),
                  pl.BlockSpec((None, n_mem, mem_w), lambda bi, ti: (bi, 0, 0)),
                  pl.BlockSpec((None, n_mem, mem_w), lambda bi, ti: (bi, 0, 0))],
        out_specs=pl.BlockSpec((tq, mem_w), lambda bi, ti: (bi * nt + ti, 0)),
        out_shape=jax.ShapeDtypeStruct((b * t, mem_w), BF16),
        compiler_params=_params(("parallel", "parallel"), 32),
        name="memattn",
    )(qm, mem_k, mem_v)


def _mix_kernel(x_ref, h_ref, oa_ref, os_ref, om_ref, wg0_ref, wg1_ref, wg2_ref,
                wa_ref, ws_ref, wm_ref, gpost_ref, o_ref, ss_scr, *, tn):
    j = pl.program_id(1)

    @pl.when(j == 0)
    def _():
        ss_scr[...] = jnp.zeros_like(ss_scr)

    h = h_ref[...]
    merged = (jax.nn.sigmoid(_dot(h, wg0_ref[...])) * _dot(oa_ref[...], wa_ref[...])
              + jax.nn.sigmoid(_dot(h, wg1_ref[...])) * _dot(os_ref[...], ws_ref[...])
              + jax.nn.sigmoid(_dot(h, wg2_ref[...])) * _dot(om_ref[...], wm_ref[...]))
    o_ref[:, pl.ds(pl.multiple_of(j * tn, tn), tn)] = merged
    sq = merged * merged
    ss_scr[...] += sum(sq[:, k * LANES:(k + 1) * LANES] for k in range(tn // LANES))

    @pl.when(j == pl.num_programs(1) - 1)
    def _():
        ss = jnp.sum(ss_scr[...], axis=-1, keepdims=True)
        inv = lax.rsqrt(ss / o_ref.shape[1] + EPS)
        o_ref[...] = x_ref[...] + (o_ref[...] * inv) * gpost_ref[...]


def _mix_residual(x2d, h, o_a, o_s, o_m, w_in_bf, gate_col0, w_oa, w_out_bf, g_post, tm, tn):
    m, d = x2d.shape
    nj = d // tn
    q_w, ssm_w, mem_w = o_a.shape[1], o_s.shape[1], o_m.shape[1]
    assert gate_col0 % tn == 0 and q_w % ssm_w == 0 and (q_w + ssm_w) % mem_w == 0
    gate_blk0 = gate_col0 // tn

    def rows(w):
        mode = dict(pipeline_mode=pl.Buffered(1)) if m == tm else {}
        return pl.BlockSpec((tm, w), lambda i, j: (i, 0), **mode)

    def gate_spec(br):
        return pl.BlockSpec((d, tn), lambda i, j: (0, gate_blk0 + br * nj + j))

    vec = pl.BlockSpec((1, d), lambda i, j: (0, 0))
    return pl.pallas_call(
        functools.partial(_mix_kernel, tn=tn),
        grid=(m // tm, nj),
        in_specs=[rows(d), rows(d), rows(q_w), rows(ssm_w), rows(mem_w),
                  gate_spec(0), gate_spec(1), gate_spec(2),
                  pl.BlockSpec((q_w, tn), lambda i, j: (0, j)),
                  pl.BlockSpec((ssm_w, tn), lambda i, j: (q_w // ssm_w, j)),
                  pl.BlockSpec((mem_w, tn), lambda i, j: ((q_w + ssm_w) // mem_w, j)), vec],
        out_specs=rows(d),
        out_shape=jax.ShapeDtypeStruct((m, d), F32),
        scratch_shapes=[pltpu.VMEM((tm, LANES), F32)],
        compiler_params=_params(("parallel", "arbitrary"), 56),
        name="mix",
    )(x2d, h, o_a, o_s, o_m, w_in_bf, w_in_bf, w_in_bf, w_oa, w_out_bf, w_out_bf,
      g_post.reshape(1, d))


def _ffn_kernel(x_ref, gpre_ref, wua_ref, wub_ref, cw_ref, cb_ref, wd_ref, gpost_ref, cprev_ref,
                o_ref, tail_ref, h_scr, a_scr, b_scr, carry_scr):
    i = pl.program_id(1)
    j = pl.program_id(2)
    nb, tt, d = x_ref.shape
    tf = wua_ref.shape[1]
    rows = nb * tt

    @pl.when(j == 0)
    def _():
        h_scr[...] = _rms(x_ref[...].reshape(rows, d), gpre_ref[...]).astype(BF16)
        o_ref[...] = jnp.zeros_like(o_ref)

    h = h_scr[...]
    a_scr[:, 0:SUBLANES, :] = jnp.where(i == 0, cprev_ref[...], carry_scr[j])
    n_slices = tf // MXU_WIDTH
    cols = [slice(s * MXU_WIDTH, (s + 1) * MXU_WIDTH) for s in range(n_slices)]
    for sl in cols:
        a_scr[:, SUBLANES:, sl] = _dot(h, wua_ref[:, sl]).reshape(nb, tt, MXU_WIDTH)
        b_scr[:, sl] = _dot(h, wub_ref[:, sl])
    cw = cw_ref[...]
    cb = cb_ref[...]
    for sl in cols:
        gated = []
        for bi in range(nb):
            a = a_scr[bi, SUBLANES:, sl]
            first = jnp.concatenate([a_scr[bi, 0:SUBLANES, sl], a[0:SUBLANES]], axis=0)
            prev1 = jnp.concatenate([first[SUBLANES - 1:2 * SUBLANES - 1],
                                     pltpu.roll(a, 1, 0)[SUBLANES:]], axis=0)
            prev2 = jnp.concatenate([first[SUBLANES - 2:2 * SUBLANES - 2],
                                     pltpu.roll(a, 2, 0)[SUBLANES:]], axis=0)
            conv = (prev2 * cw[0:1, sl] + prev1 * cw[1:2, sl] + a * cw[2:3, sl]) + cb[:, sl]
            gated.append(_gelu(conv) * b_scr[bi * tt:(bi + 1) * tt, sl])
        act = (gated[0] if nb == 1 else jnp.concatenate(gated, axis=0)).astype(BF16)
        o_ref[...] += _dot(act, wd_ref[sl, :]).reshape(nb, tt, d)
    tail = a_scr[:, tt:tt + SUBLANES, :]
    carry_scr[j] = tail
    tail_ref[...] = tail

    @pl.when(j == pl.num_programs(2) - 1)
    def _():
        f = _rms(o_ref[...].reshape(rows, d), gpost_ref[...]).reshape(nb, tt, d)
        o_ref[...] = x_ref[...] + f


def _conv_ffn(x3d, g_pre, w_up, conv_w, conv_b, w_down, g_post, conv_prev8, nb, tt, tf):
    b, t, d = x3d.shape
    d_ff = w_down.shape[0]
    nj = d_ff // tf
    nt = t // tt
    mode = dict(pipeline_mode=pl.Buffered(1)) if (b // nb) * nt == 1 else {}
    return pl.pallas_call(
        _ffn_kernel,
        grid=(b // nb, nt, nj),
        in_specs=[pl.BlockSpec((nb, tt, d), lambda bi, i, j: (bi, i, 0), **mode),
                  pl.BlockSpec((1, d), lambda bi, i, j: (0, 0)),
                  pl.BlockSpec((d, tf), lambda bi, i, j: (0, j)),
                  pl.BlockSpec((d, tf), lambda bi, i, j: (0, nj + j)),
                  pl.BlockSpec((CONV_W, tf), lambda bi, i, j: (0, j)),
                  pl.BlockSpec((1, tf), lambda bi, i, j: (0, j)),
                  pl.BlockSpec((tf, d), lambda bi, i, j: (j, 0)),
                  pl.BlockSpec((1, d), lambda bi, i, j: (0, 0)),
                  pl.BlockSpec((nb, SUBLANES, tf), lambda bi, i, j: (bi, 0, j))],
        out_specs=[pl.BlockSpec((nb, tt, d), lambda bi, i, j: (bi, i, 0), **mode),
                   pl.BlockSpec((nb, None, SUBLANES, tf), lambda bi, i, j: (bi, i, 0, j))],
        out_shape=[jax.ShapeDtypeStruct((b, t, d), F32),
                   jax.ShapeDtypeStruct((b, nt, SUBLANES, d_ff), F32)],
        scratch_shapes=[pltpu.VMEM((nb * tt, d), BF16),
                        pltpu.VMEM((nb, tt + SUBLANES, tf), F32), pltpu.VMEM((nb * tt, tf), F32),
                        pltpu.VMEM((nj, nb, SUBLANES, tf), F32)],
        compiler_params=_params(("arbitrary", "arbitrary", "arbitrary"), 56),
        name="ffn",
    )(x3d, g_pre.reshape(1, d), w_up, w_up, conv_w.astype(F32), conv_b.astype(F32).reshape(1, d_ff),
      w_down, g_post.reshape(1, d), conv_prev8)


def _tiles(b, t):
    row_tile = 1024 if b * t <= 1024 else 512
    tt = min(t, row_tile)
    return dict(
        proj_rows=512,
        attn_batch=max(1, min(b, 1024 // t)),
        s5_steps=min(t, 512 // b),
        memattn_rows=tt,
        mix_rows=512, mix_cols=512,
        ffn_batch=min(b, row_tile // tt), ffn_rows=tt)


FFN_COLS = 2 * MXU_WIDTH


def _layer(x, attn_past, s0, conv_prev, mem_k, mem_v, bias, lw):
    b, t, d = x.shape
    m = b * t
    q_w, kv_w, ssm_w, mem_w = lw['q_w'], lw['kv_w'], lw['ssm_w'], lw['mem_w']
    n_state = lw['n_state']
    d_ff = lw['w_down'].shape[0]
    tiles = _tiles(b, t)
    x2d = x.reshape(m, d)

    q, kv, u, qm, h = _norm_proj(x2d, lw['norm_pre_mix'], lw['proj_outputs'], tiles['proj_rows'])
    kv3 = kv.reshape(b, t, 2 * kv_w)
    if attn_past is None:
        hist_k = jnp.zeros((b, WINDOW, kv_w), F32)
        hist_v = hist_k
        first_valid = WINDOW
    else:
        hist_k = attn_past[0].astype(F32).reshape(b, WINDOW, kv_w)
        hist_v = attn_past[1].astype(F32).reshape(b, WINDOW, kv_w)
        first_valid = 0
    o_a = _band_attention(q.reshape(b, t, q_w), kv3, hist_k, hist_v, bias, lw['attn_sinks'],
                          tiles['attn_batch'], first_valid).reshape(m, q_w)

    if s0 is None:
        s0_re = jnp.zeros((b, n_state), F32)
        s0_im = s0_re
    else:
        s0_re = s0[0].astype(F32).reshape(b, n_state)
        s0_im = s0[1].astype(F32).reshape(b, n_state)
    u_tb = u.reshape(b, t, ssm_w).transpose(1, 0, 2)
    o_s_tb, s_re, s_im = _s5_layer(u_tb, s0_re, s0_im, lw['s5'], tiles['s5_steps'])
    o_s = o_s_tb.reshape(t, b, ssm_w).transpose(1, 0, 2).reshape(m, ssm_w)

    o_m = _memory_attention(qm, mem_k, mem_v, b, t, tiles['memattn_rows'])

    x1 = _mix_residual(x2d, h, o_a, o_s, o_m, lw['w_in_bf'], lw['gate_col0'], lw['w_oa'],
                       lw['w_out_bf'], lw['norm_post_mix'], tiles['mix_rows'], tiles['mix_cols'])

    if conv_prev is None:
        conv_prev8 = jnp.zeros((b, SUBLANES, d_ff), F32)
    else:
        conv_prev8 = jnp.pad(conv_prev.astype(F32), ((0, 0), (SUBLANES - (CONV_W - 1), 0), (0, 0)))
    x2, tails = _conv_ffn(x1.reshape(b, t, d), lw['norm_pre_ffn'], lw['w_up'], lw['conv_w'], lw['conv_b'],
                          lw['w_down'], lw['norm_post_ffn'], conv_prev8,
                          tiles['ffn_batch'], tiles['ffn_rows'], FFN_COLS)
    conv_new = tails[:, -1, SUBLANES - (CONV_W - 1):, :]

    n_kv = kv_w // HEAD_DIM
    k_new = jnp.concatenate([hist_k, kv3[:, :, :kv_w]], axis=1)[:, -WINDOW:].reshape(b, WINDOW, n_kv, HEAD_DIM)
    v_new = jnp.concatenate([hist_v, kv3[:, :, kv_w:]], axis=1)[:, -WINDOW:].reshape(b, WINDOW, n_kv, HEAD_DIM)
    return x2, k_new, v_new, s_re, s_im, conv_new


def kernel(x_prompt, x_sample, cache_attn_k, cache_attn_v, cache_mem_k, cache_mem_v, state_ssm_re, state_ssm_im, state_conv, mem_prompt, rel_bias_table, norm_pre_mix, norm_post_mix, norm_pre_ffn, norm_post_ffn, norm_mem, w_in, attn_sinks, ssm_a_re, ssm_a_im, ssm_log_dt, ssm_b_re, ssm_b_im, ssm_c_re, ssm_c_im, ssm_d, w_glu, w_mem_kv, w_out, w_up, conv_w, conv_b, w_down):
    depth = w_in.shape[0]
    bp, _, d = x_prompt.shape
    n_mem = mem_prompt.shape[1]
    n_q = attn_sinks.shape[1]
    n_kv, hd = cache_attn_k.shape[-2:]
    assert hd == HEAD_DIM and n_kv == N_KV and cache_attn_k.shape[2] == WINDOW
    groups, p_state = ssm_a_re.shape[1:]
    q_w, kv_w = n_q * HEAD_DIM, n_kv * HEAD_DIM
    ssm_w = ssm_d.shape[1]
    mem_w = w_mem_kv.shape[2] // 2
    proj_w = q_w + 2 * kv_w + ssm_w + mem_w
    mem_hd = mem_w // MEM_HEADS

    xp, xs = x_prompt, x_sample
    outs = [[] for _ in range(12)]
    for l in range(depth):
        bias = _rel_bias(rel_bias_table)
        lp = dict(ssm_a_re=ssm_a_re[l], ssm_a_im=ssm_a_im[l], ssm_log_dt=ssm_log_dt[l],
                  ssm_b_re=ssm_b_re[l], ssm_b_im=ssm_b_im[l], ssm_c_re=ssm_c_re[l], ssm_c_im=ssm_c_im[l],
                  ssm_d=ssm_d[l], w_glu=w_glu[l])
        w_in_bf = w_in[l].astype(BF16)
        w_out_bf = w_out[l].astype(BF16)
        grp = n_q // n_kv
        w_q = w_in_bf[:, :q_w].reshape(d, n_kv, grp, HEAD_DIM).transpose(0, 2, 1, 3).reshape(d, q_w)
        w_oa = w_out_bf[:q_w].reshape(n_kv, grp, HEAD_DIM, d).transpose(1, 0, 2, 3).reshape(q_w, d)
        u_off = q_w + 2 * kv_w
        proj_outputs = [(BF16, [(w_q, q_w, 0)]),
                        (F32, _col_blocks(w_in_bf, q_w, 2 * kv_w)),
                        (F32, _col_blocks(w_in_bf, u_off, ssm_w)),
                        (BF16, _col_blocks(w_in_bf, u_off + ssm_w, mem_w)),
                        (BF16, [])]
        lw = dict(q_w=q_w, kv_w=kv_w, ssm_w=ssm_w, mem_w=mem_w, n_state=groups * p_state,
                  norm_pre_mix=norm_pre_mix[l], norm_post_mix=norm_post_mix[l],
                  norm_pre_ffn=norm_pre_ffn[l], norm_post_ffn=norm_post_ffn[l],
                  proj_outputs=proj_outputs, w_in_bf=w_in_bf, gate_col0=proj_w, w_oa=w_oa, w_out_bf=w_out_bf,
                  attn_sinks=attn_sinks[l], s5=_s5_params(lp),
                  w_up=w_up[l].astype(BF16),
                  conv_w=conv_w[l], conv_b=conv_b[l], w_down=w_down[l].astype(BF16))

        w_mem_bf = w_mem_kv[l].astype(BF16)
        mk_p, mv_p = _norm_proj(mem_prompt.reshape(bp * n_mem, d), norm_mem[l],
                                [(F32, [(w_mem_bf, mem_w, 0)]), (F32, [(w_mem_bf, mem_w, 1)])],
                                _tiles(bp, n_mem)['proj_rows'])
        mk_p = mk_p.reshape(bp, n_mem, mem_w)
        mv_p = mv_p.reshape(bp, n_mem, mem_w)
        xp, k_p, v_p, sr_p, si_p, c_p = _layer(xp, None, None, None, mk_p, mv_p, bias, lw)

        bs = xs.shape[0]
        xs, k_s, v_s, sr_s, si_s, c_s = _layer(
            xs, (cache_attn_k[l], cache_attn_v[l]), (state_ssm_re[l], state_ssm_im[l]), state_conv[l],
            cache_mem_k[l].reshape(bs, n_mem, mem_w), cache_mem_v[l].reshape(bs, n_mem, mem_w), bias, lw)

        vals = (k_p, v_p, sr_p.reshape(bp, groups, p_state), si_p.reshape(bp, groups, p_state), c_p,
                mk_p.reshape(bp, n_mem, MEM_HEADS, mem_hd), mv_p.reshape(bp, n_mem, MEM_HEADS, mem_hd),
                k_s, v_s, sr_s.reshape(bs, groups, p_state), si_s.reshape(bs, groups, p_state), c_s)
        for acc, val in zip(outs, vals):
            acc.append(val)
    return (xp, xs) + tuple(jnp.stack(o) for o in outs)
```

```python
import functools
import math

import jax
import jax.numpy as jnp
from jax import lax
from jax.experimental import pallas as pl
from jax.experimental.pallas import tpu as pltpu

F32 = jnp.float32
BF16 = jnp.bfloat16

EPS = 1e-6
NEG_INF = -1e30
CHUNK = 64
WINDOW = 128
HEAD_DIM = 64
N_KV = 4
MAX_DISTANCE = 128
SSM_GROUP_CH = 16
MEM_HEADS = 4
CONV_W = 3

MIB = 1024 * 1024
LANES = 128
MXU_WIDTH = 256
SUBLANES = 8
SCAN_CARRY_ELEMS = 4096


def _params(semantics, vmem_mib):
    return pltpu.CompilerParams(dimension_semantics=semantics, vmem_limit_bytes=vmem_mib * MIB)


def _rms(x, g):
    y = x * lax.rsqrt(jnp.mean(x * x, axis=-1, keepdims=True) + EPS)
    return y * g


def _dot(a, b):
    return jnp.dot(a, b, preferred_element_type=F32)


def _dot_t(a, b):
    return lax.dot_general(a, b, (((1,), (1,)), ((), ())), preferred_element_type=F32)


def _proj_kernel(x_ref, g_ref, *refs, blocks_per_out):
    n_w = sum(blocks_per_out)
    w_refs, o_refs = refs[:n_w], refs[n_w:]
    h = _rms(x_ref[...], g_ref[...]).astype(BF16)
    first = 0
    for o_ref, count in zip(o_refs, blocks_per_out):
        if count == 0:
            o_ref[...] = h
            continue
        parts = [_dot(h, w_ref[...]) for w_ref in w_refs[first:first + count]]
        r = parts[0] if count == 1 else jnp.concatenate(parts, axis=1)
        o_ref[...] = r.astype(o_ref.dtype)
        first += count


def _col_blocks(arr, start, width):
    blk = math.gcd(start, width) if start else width
    assert blk % LANES == 0
    return [(arr, blk, start // blk + k) for k in range(width // blk)]


def _norm_proj(x2d, g, outputs, tm):
    m, d = x2d.shape
    w_args, w_specs, out_specs, out_shapes = [], [], [], []
    for dt, blocks in outputs:
        for arr, width, blk in blocks:
            assert arr.shape[0] == d and arr.shape[1] % width == 0
            w_args.append(arr)
            w_specs.append(pl.BlockSpec((d, width), lambda i, blk=blk: (0, blk)))
        total = sum(width for _, width, _ in blocks) if blocks else d
        out_specs.append(pl.BlockSpec((tm, total), lambda i: (i, 0)))
        out_shapes.append(jax.ShapeDtypeStruct((m, total), dt))
    return pl.pallas_call(
        functools.partial(_proj_kernel, blocks_per_out=tuple(len(b) for _, b in outputs)),
        grid=(m // tm,),
        in_specs=[pl.BlockSpec((tm, d), lambda i: (i, 0)),
                  pl.BlockSpec((1, d), lambda i: (0, 0))] + w_specs,
        out_specs=out_specs,
        out_shape=out_shapes,
        compiler_params=_params(("parallel",), 48),
        name="proj",
    )(x2d, g.reshape(1, d), *w_args)


def _bias_kernel(idx_ref, table_ref, o_ref, *, n_buckets, n_heads):
    idx = idx_ref[...]
    for h in range(n_heads):
        acc = jnp.zeros(idx.shape, F32)
        for b in range(n_buckets):
            acc = jnp.where(idx == b, table_ref[b, h], acc)
        o_ref[h] = acc


def _t5_bucket(rel, n_buckets):
    half = n_buckets // 2
    max_exact = half // 2
    n = jnp.abs(rel)
    large = max_exact + (jnp.log(jnp.maximum(n, 1).astype(F32) / max_exact)
                         / math.log(MAX_DISTANCE / max_exact) * (half - max_exact)).astype(jnp.int32)
    large = jnp.minimum(large, half - 1)
    return jnp.where(rel > 0, half, 0) + jnp.where(n < max_exact, n, large)


def _rel_bias(table):
    n_buckets, n_heads = table.shape
    lk = WINDOW + CHUNK
    rel = jnp.arange(lk)[None, :] - WINDOW - jnp.arange(CHUNK)[:, None]
    idx = _t5_bucket(rel, n_buckets).astype(jnp.int32)
    return pl.pallas_call(
        functools.partial(_bias_kernel, n_buckets=n_buckets, n_heads=n_heads),
        in_specs=[pl.BlockSpec((CHUNK, lk), lambda: (0, 0)),
                  pl.BlockSpec(memory_space=pltpu.SMEM)],
        out_specs=pl.BlockSpec((n_heads, CHUNK, lk), lambda: (0, 0, 0)),
        out_shape=jax.ShapeDtypeStruct((n_heads, CHUNK, lk), F32),
        name="bias",
    )(idx, table.astype(F32))


def _attn_kernel(q_ref, k_ref, v_ref, hk_ref, hv_ref, bias_ref, o_ref, kf_scr, vf_scr, s_scr, *, n_masked, grp):
    nb, t, _ = q_ref.shape
    kv_w = k_ref.shape[2]
    n_chunks = t // CHUNK
    lk = WINDOW + CHUNK
    lkp = bias_ref.shape[2]
    pair_w = 2 * HEAD_DIM
    scale = HEAD_DIM ** -0.5
    low_half = lax.broadcasted_iota(jnp.int32, (1, pair_w), 1) < HEAD_DIM
    ones = jnp.ones((lkp, pair_w), BF16)
    kv_pad = jnp.zeros((lkp - lk, kv_w), BF16)

    kf_scr[:, 0:WINDOW, :] = hk_ref[...].astype(BF16)
    kf_scr[:, WINDOW:, :] = k_ref[...].astype(BF16)
    vf_scr[:, 0:WINDOW, :] = hv_ref[...].astype(BF16)
    vf_scr[:, WINDOW:, :] = v_ref[...].astype(BF16)

    n_total = nb * n_chunks

    def locate(n):
        bi = n // n_chunks
        c = n % n_chunks
        return bi, c, pl.multiple_of(c * CHUNK, CHUNK)

    def score_chunk(n, slot):
        bi, c, r0 = locate(n)
        q = q_ref[bi, pl.ds(r0, CHUNK), :] * scale
        k = jnp.concatenate([kf_scr[bi, pl.ds(r0, lk), :], kv_pad], axis=0)
        variant = jnp.minimum(c, n_masked) * N_KV
        for h in range(N_KV):
            j, e = divmod(h, 2)
            qp = jnp.concatenate([q[:, g * kv_w + j * pair_w:g * kv_w + (j + 1) * pair_w]
                                  for g in range(grp)], axis=0)
            in_head = low_half if e == 0 else jnp.logical_not(low_half)
            qh = jnp.where(in_head, qp, jnp.zeros_like(qp))
            s_scr[slot, h] = _dot_t(qh, k[:, j * pair_w:(j + 1) * pair_w]) + bias_ref[variant + h]

    def chunk(n, slot):
        score_chunk(jnp.minimum(n + 1, n_total - 1), 1 - slot)
        bi, c, r0 = locate(n)
        v = jnp.concatenate([vf_scr[bi, pl.ds(r0, lk), :], kv_pad], axis=0)
        scores = [s_scr[slot, h] for h in range(N_KV)]
        probs = [jnp.exp(s - jnp.max(s, axis=-1, keepdims=True)).astype(BF16) for s in scores]
        v_ext = [jnp.concatenate([v[:, j * pair_w:(j + 1) * pair_w], ones], axis=1) for j in range(N_KV // 2)]
        sums = [_dot(p, v_ext[h // 2]) for h, p in enumerate(probs)]
        outs = [r[:, :pair_w] / r[:, pair_w:] for r in sums]
        for j in range(N_KV // 2):
            o_pair = jnp.where(low_half, outs[2 * j], outs[2 * j + 1])
            for g in range(grp):
                o_ref[bi, pl.ds(r0, CHUNK), g * kv_w + j * pair_w:g * kv_w + (j + 1) * pair_w] = (
                    o_pair[g * CHUNK:(g + 1) * CHUNK].astype(BF16))

    def chunk_pair(p, carry):
        chunk(2 * p, 0)
        chunk(2 * p + 1, 1)
        return carry

    assert n_total % 2 == 0
    score_chunk(0, 0)
    lax.fori_loop(0, n_total // 2, chunk_pair, 0)


def _band_attention(q3, kv3, hist_k, hist_v, bias, sinks, nb, first_valid):
    b, t, q_w = q3.shape
    kv_w = hist_k.shape[2]
    grp = q_w // HEAD_DIM // N_KV
    lk = WINDOW + CHUNK
    lkp = 2 * LANES
    n_masked = first_valid // CHUNK
    sink_col = jnp.broadcast_to(sinks.astype(F32).reshape(N_KV, grp, 1, 1), (N_KV, grp, CHUNK, 1))
    ext = jnp.concatenate([bias.reshape(N_KV, grp * CHUNK, lk), sink_col.reshape(N_KV, grp * CHUNK, 1),
                           jnp.full((N_KV, grp * CHUNK, lkp - lk - 1), NEG_INF, F32)], axis=2)
    key = jnp.arange(lkp)
    variants = [jnp.where(key < (n_masked - i) * CHUNK, NEG_INF, ext) for i in range(n_masked + 1)]
    bias_ext = jnp.concatenate(variants, axis=0)
    return pl.pallas_call(
        functools.partial(_attn_kernel, n_masked=n_masked, grp=grp),
        grid=(b // nb,),
        in_specs=[pl.BlockSpec((nb, t, q_w), lambda i: (i, 0, 0)),
                  pl.BlockSpec((nb, t, kv_w), lambda i: (i, 0, 0)),
                  pl.BlockSpec((nb, t, kv_w), lambda i: (i, 0, 1)),
                  pl.BlockSpec((nb, WINDOW, kv_w), lambda i: (i, 0, 0)),
                  pl.BlockSpec((nb, WINDOW, kv_w), lambda i: (i, 0, 0)),
                  pl.BlockSpec(bias_ext.shape, lambda i: (0, 0, 0))],
        out_specs=pl.BlockSpec((nb, t, q_w), lambda i: (i, 0, 0)),
        out_shape=jax.ShapeDtypeStruct((b, t, q_w), BF16),
        scratch_shapes=[pltpu.VMEM((nb, WINDOW + t, kv_w), BF16), pltpu.VMEM((nb, WINDOW + t, kv_w), BF16),
                        pltpu.VMEM((2, N_KV, grp * CHUNK, lkp), F32)],
        compiler_params=_params(("parallel",), 40),
        name="attn",
    )(q3, kv3, kv3, hist_k, hist_v, bias_ext)


def _gelu(x):
    return jax.nn.gelu(x)


def _s5_kernel(u_ref, s0re_ref, s0im_ref, are_ref, aim_ref, wbre_ref, wbim_ref, wcre_ref, wcim_ref,
               d_ref, wglu_ref, o_ref, sre_out_ref, sim_out_ref,
               sre_scr, sim_scr, st_re, st_im, y_scr, *, batch, steps, lane_chunk):
    i = pl.program_id(0)

    @pl.when(i == 0)
    def _():
        st_re[...] = s0re_ref[...]
        st_im[...] = s0im_ref[...]

    u = u_ref[...]
    ub = u.astype(BF16)
    n_in_blk, in_blk, st_blk = wbre_ref.shape
    for r in range(n_in_blk):
        blk = ub[:, r * in_blk:(r + 1) * in_blk]
        sre_scr[:, r * st_blk:(r + 1) * st_blk] = _dot(blk, wbre_ref[r])
        sim_scr[:, r * st_blk:(r + 1) * st_blk] = _dot(blk, wbim_ref[r])

    n_state = sre_scr.shape[1]
    for lc in range(n_state // lane_chunk):
        sl = slice(lc * lane_chunk, (lc + 1) * lane_chunk)
        ar = are_ref[:, sl]
        ai = aim_ref[:, sl]

        def body(t, carry, sl=sl, ar=ar, ai=ai):
            sr, si = carry
            rows = pl.ds(pl.multiple_of(t * batch, batch), batch)
            nr = ar * sr - ai * si + sre_scr[rows, sl]
            ni = ar * si + ai * sr + sim_scr[rows, sl]
            sre_scr[rows, sl] = nr
            sim_scr[rows, sl] = ni
            return nr, ni

        sr, si = lax.fori_loop(0, steps, body, (st_re[:, sl], st_im[:, sl]), unroll=8)
        st_re[:, sl] = sr
        st_im[:, sl] = si

    n_out_blk, k_blk, out_blk = wcre_ref.shape
    for kb in range(n_out_blk):
        ksl = slice(kb * k_blk, (kb + 1) * k_blk)
        y_scr[:, kb * out_blk:(kb + 1) * out_blk] = (
            _dot(sre_scr[:, ksl].astype(BF16), wcre_ref[kb])
            + _dot(sim_scr[:, ksl].astype(BF16), wcim_ref[kb]))
    y = _gelu(y_scr[...] + d_ref[...] * u)
    o_ref[...] = (y * jax.nn.sigmoid(_dot(y.astype(BF16), wglu_ref[...]))).astype(BF16)

    @pl.when(i == pl.num_programs(0) - 1)
    def _():
        sre_out_ref[...] = st_re[...]
        sim_out_ref[...] = st_im[...]


def _s5_params(lp):
    lam = lax.complex(lp['ssm_a_re'].astype(F32), lp['ssm_a_im'].astype(F32))
    dt = jnp.exp(lp['ssm_log_dt'].astype(F32))[:, None]
    a_bar = jnp.exp(lam * dt)
    b_mat = lax.complex(lp['ssm_b_re'].astype(F32), lp['ssm_b_im'].astype(F32))
    b_bar = ((a_bar - 1.0) / lam)[..., None] * b_mat
    g, p, c = b_bar.shape
    gb_in = 256 // c
    gb_out = 128 // c

    def b_blocks(x):
        x = x.reshape(g // gb_in, gb_in, p, c).transpose(0, 1, 3, 2)
        x = jnp.einsum('rgcp,gh->rgchp', x, jnp.eye(gb_in, dtype=F32))
        return x.reshape(g // gb_in, gb_in * c, gb_in * p).astype(BF16)

    def c_blocks(x):
        x = x.reshape(g // gb_out, gb_out, c, p).transpose(0, 1, 3, 2)
        x = jnp.einsum('kgpc,gh->kgphc', x, jnp.eye(gb_out, dtype=F32))
        return x.reshape(g // gb_out, gb_out * p, gb_out * c).astype(BF16)

    return dict(a_re=jnp.real(a_bar).reshape(1, g * p), a_im=jnp.imag(a_bar).reshape(1, g * p),
                wb_re=b_blocks(jnp.real(b_bar)), wb_im=b_blocks(jnp.imag(b_bar)),
                wc_re=c_blocks(lp['ssm_c_re'].astype(F32)), wc_im=c_blocks(-lp['ssm_c_im'].astype(F32)),
                d=lp['ssm_d'].astype(F32).reshape(1, -1), w_glu=lp['w_glu'].astype(BF16))


def _s5_layer(u_tb, s0_re, s0_im, sp, steps):
    t, b, ssm_w = u_tb.shape
    n_state = s0_re.shape[1]
    rows = steps * b
    lane_chunk = SCAN_CARRY_ELEMS // b
    a_re = jnp.broadcast_to(sp['a_re'], (b, n_state))
    a_im = jnp.broadcast_to(sp['a_im'], (b, n_state))

    def full(x):
        nd = x.ndim
        return pl.BlockSpec(x.shape, lambda i: (0,) * nd)

    consts = [s0_re, s0_im, a_re, a_im, sp['wb_re'], sp['wb_im'], sp['wc_re'], sp['wc_im'], sp['d'], sp['w_glu']]
    return pl.pallas_call(
        functools.partial(_s5_kernel, batch=b, steps=steps, lane_chunk=lane_chunk),
        grid=(t // steps,),
        in_specs=[pl.BlockSpec((rows, ssm_w), lambda i: (i, 0))] + [full(x) for x in consts],
        out_specs=[pl.BlockSpec((rows, ssm_w), lambda i: (i, 0)),
                   pl.BlockSpec((b, n_state), lambda i: (0, 0)),
                   pl.BlockSpec((b, n_state), lambda i: (0, 0))],
        out_shape=[jax.ShapeDtypeStruct((t * b, ssm_w), BF16),
                   jax.ShapeDtypeStruct((b, n_state), F32),
                   jax.ShapeDtypeStruct((b, n_state), F32)],
        scratch_shapes=[pltpu.VMEM((rows, n_state), F32), pltpu.VMEM((rows, n_state), F32),
                        pltpu.VMEM((b, n_state), F32), pltpu.VMEM((b, n_state), F32),
                        pltpu.VMEM((rows, ssm_w), F32)],
        compiler_params=_params(("arbitrary",), 48),
        name="s5",
    )(u_tb.reshape(t * b, ssm_w), *consts)


def _memattn_kernel(q_ref, k_ref, v_ref, o_ref, *, head_dim):
    q = q_ref[...]
    k = k_ref[...].astype(BF16)
    v = v_ref[...].astype(BF16)
    scale = head_dim ** -0.5
    ones = jnp.ones((v.shape[0], head_dim), BF16)
    heads = [slice(h * head_dim, (h + 1) * head_dim) for h in range(MEM_HEADS)]
    scores = [_dot_t(q[:, sl], k[:, sl]) * scale for sl in heads]
    probs = [jnp.exp(s - jnp.max(s, axis=-1, keepdims=True)).astype(BF16) for s in scores]
    sums = [_dot(p, jnp.concatenate([v[:, sl], ones], axis=1)) for p, sl in zip(probs, heads)]
    for r, sl in zip(sums, heads):
        o_ref[:, sl] = (r[:, :head_dim] / r[:, head_dim:]).astype(BF16)


def _memory_attention(qm, mem_k, mem_v, b, t, tq):
    n_mem, mem_w = mem_k.shape[1:]
    nt = t // tq
    return pl.pallas_call(
        functools.partial(_memattn_kernel, head_dim=mem_w // MEM_HEADS),
        grid=(b, nt),
        in_specs=[pl.BlockSpec((tq, mem_w), lambda bi, ti: (bi * nt + ti, 0)),
                  pl.BlockSpec((None, n_mem, mem_w), lambda bi, ti: (bi, 0, 0)),
                  pl.BlockSpec((None, n_mem, mem_w), lambda bi, ti: (bi, 0, 0))],
        out_specs=pl.BlockSpec((tq, mem_w), lambda bi, ti: (bi * nt + ti, 0)),
        out_shape=jax.ShapeDtypeStruct((b * t, mem_w), BF16),
        compiler_params=_params(("parallel", "parallel"), 32),
        name="memattn",
    )(qm, mem_k, mem_v)


def _mix_kernel(x_ref, h_ref, oa_ref, os_ref, om_ref, wg0_ref, wg1_ref, wg2_ref,
                wa_ref, ws_ref, wm_ref, gpost_ref, o_ref, ss_scr, *, tn):
    j = pl.program_id(1)

    @pl.when(j == 0)
    def _():
        ss_scr[...] = jnp.zeros_like(ss_scr)

    h = h_ref[...]
    merged = (jax.nn.sigmoid(_dot(h, wg0_ref[...])) * _dot(oa_ref[...], wa_ref[...])
              + jax.nn.sigmoid(_dot(h, wg1_ref[...])) * _dot(os_ref[...], ws_ref[...])
              + jax.nn.sigmoid(_dot(h, wg2_ref[...])) * _dot(om_ref[...], wm_ref[...]))
    o_ref[:, pl.ds(pl.multiple_of(j * tn, tn), tn)] = merged
    sq = merged * merged
    ss_scr[...] += sum(sq[:, k * LANES:(k + 1) * LANES] for k in range(tn // LANES))

    @pl.when(j == pl.num_programs(1) - 1)
    def _():
        ss = jnp.sum(ss_scr[...], axis=-1, keepdims=True)
        inv = lax.rsqrt(ss / o_ref.shape[1] + EPS)
        o_ref[...] = x_ref[...] + (o_ref[...] * inv) * gpost_ref[...]


def _mix_residual(x2d, h, o_a, o_s, o_m, w_in_bf, gate_col0, w_oa, w_out_bf, g_post, tm, tn):
    m, d = x2d.shape
    nj = d // tn
    q_w, ssm_w, mem_w = o_a.shape[1], o_s.shape[1], o_m.shape[1]
    assert gate_col0 % tn == 0 and q_w % ssm_w == 0 and (q_w + ssm_w) % mem_w == 0
    gate_blk0 = gate_col0 // tn

    def rows(w):
        mode = dict(pipeline_mode=pl.Buffered(1)) if m == tm else {}
        return pl.BlockSpec((tm, w), lambda i, j: (i, 0), **mode)

    def gate_spec(br):
        return pl.BlockSpec((d, tn), lambda i, j: (0, gate_blk0 + br * nj + j))

    vec = pl.BlockSpec((1, d), lambda i, j: (0, 0))
    return pl.pallas_call(
        functools.partial(_mix_kernel, tn=tn),
        grid=(m // tm, nj),
        in_specs=[rows(d), rows(d), rows(q_w), rows(ssm_w), rows(mem_w),
                  gate_spec(0), gate_spec(1), gate_spec(2),
                  pl.BlockSpec((q_w, tn), lambda i, j: (0, j)),
                  pl.BlockSpec((ssm_w, tn), lambda i, j: (q_w // ssm_w, j)),
                  pl.BlockSpec((mem_w, tn), lambda i, j: ((q_w + ssm_w) // mem_w, j)), vec],
        out_specs=rows(d),
        out_shape=jax.ShapeDtypeStruct((m, d), F32),
        scratch_shapes=[pltpu.VMEM((tm, LANES), F32)],
        compiler_params=_params(("parallel", "arbitrary"), 56),
        name="mix",
    )(x2d, h, o_a, o_s, o_m, w_in_bf, w_in_bf, w_in_bf, w_oa, w_out_bf, w_out_bf,
      g_post.reshape(1, d))


def _ffn_kernel(x_ref, gpre_ref, wua_ref, wub_ref, cw_ref, cb_ref, wd_ref, gpost_ref, cprev_ref,
                o_ref, tail_ref, h_scr, a_scr, b_scr, carry_scr, *, n_sub_valid):
    j = pl.program_id(2)
    nb, tt, d = x_ref.shape
    tf = a_scr.shape[2]
    rows = nb * tt

    @pl.when(j == 0)
    def _():
        h_scr[...] = _rms(x_ref[...].reshape(rows, d), gpre_ref[...]).astype(BF16)
        o_ref[...] = jnp.zeros_like(o_ref)

    n_sub = wua_ref.shape[1] // tf
    for s in range(n_sub):
        sub = j * n_sub + s
        step_cols = slice(s * tf, (s + 1) * tf)
        @pl.when(sub < n_sub_valid)
        def _(sub=sub, step_cols=step_cols):
            _ffn_sub_step(sub, step_cols, x_ref, wua_ref, wub_ref, cw_ref, cb_ref, wd_ref, cprev_ref,
                          o_ref, tail_ref, h_scr, a_scr, b_scr, carry_scr)

        @pl.when(sub >= n_sub_valid)
        def _(step_cols=step_cols):
            tail_ref[:, :, step_cols] = jnp.zeros((nb, SUBLANES, tf), F32)

    @pl.when(j == pl.num_programs(2) - 1)
    def _():
        f = _rms(o_ref[...].reshape(rows, d), gpost_ref[...]).reshape(nb, tt, d)
        o_ref[...] = x_ref[...] + f


def _ffn_sub_step(sub, step_cols, x_ref, wua_ref, wub_ref, cw_ref, cb_ref, wd_ref, cprev_ref,
                  o_ref, tail_ref, h_scr, a_scr, b_scr, carry_scr):
    i = pl.program_id(1)
    nb, tt, d = x_ref.shape
    tf = a_scr.shape[2]
    h = h_scr[...]
    a_scr[:, 0:SUBLANES, :] = jnp.where(i == 0, cprev_ref[:, :, step_cols], carry_scr[sub])
    n_slices = tf // MXU_WIDTH
    cols = [slice(s * MXU_WIDTH, (s + 1) * MXU_WIDTH) for s in range(n_slices)]
    for sl in cols:
        wsl = slice(step_cols.start + sl.start, step_cols.start + sl.stop)
        a_scr[:, SUBLANES:, sl] = _dot(h, wua_ref[:, wsl]).reshape(nb, tt, MXU_WIDTH)
        b_scr[:, sl] = _dot(h, wub_ref[:, wsl])
    cw = cw_ref[:, step_cols]
    cb = cb_ref[:, step_cols]
    for sl in cols:
        gated = []
        for bi in range(nb):
            a = a_scr[bi, SUBLANES:, sl]
            first = jnp.concatenate([a_scr[bi, 0:SUBLANES, sl], a[0:SUBLANES]], axis=0)
            prev1 = jnp.concatenate([first[SUBLANES - 1:2 * SUBLANES - 1],
                                     pltpu.roll(a, 1, 0)[SUBLANES:]], axis=0)
            prev2 = jnp.concatenate([first[SUBLANES - 2:2 * SUBLANES - 2],
                                     pltpu.roll(a, 2, 0)[SUBLANES:]], axis=0)
            conv = (prev2 * cw[0:1, sl] + prev1 * cw[1:2, sl] + a * cw[2:3, sl]) + cb[:, sl]
            gated.append(_gelu(conv) * b_scr[bi * tt:(bi + 1) * tt, sl])
        act = (gated[0] if nb == 1 else jnp.concatenate(gated, axis=0)).astype(BF16)
        wsl = slice(step_cols.start + sl.start, step_cols.start + sl.stop)
        o_ref[...] += _dot(act, wd_ref[wsl, :]).reshape(nb, tt, d)
    tail = a_scr[:, tt:tt + SUBLANES, :]
    carry_scr[sub] = tail
    tail_ref[:, :, step_cols] = tail


def _ffn_pad_weights(w_up_bf, conv_w, conv_b, w_down_bf, step_cols):
    d_ff = w_down_bf.shape[0]
    pad = -d_ff % step_cols
    w_up_p = jnp.pad(w_up_bf.reshape(w_up_bf.shape[0], 2, d_ff), ((0, 0), (0, 0), (0, pad)))
    return (w_up_p.reshape(w_up_bf.shape[0], 2 * (d_ff + pad)),
            jnp.pad(conv_w.astype(F32), ((0, 0), (0, pad))),
            jnp.pad(conv_b.astype(F32).reshape(1, d_ff), ((0, 0), (0, pad))),
            jnp.pad(w_down_bf, ((0, pad), (0, 0))))


def _conv_ffn(x3d, g_pre, padded, d_ff, g_post, conv_prev8, nb, tt, tf, step_cols):
    w_up, conv_w, conv_b, w_down = padded
    b, t, d = x3d.shape
    d_ff_pad = w_down.shape[0]
    nj = d_ff_pad // step_cols
    n_sub = d_ff_pad // tf
    nt = t // tt
    conv_prev8 = jnp.pad(conv_prev8, ((0, 0), (0, 0), (0, d_ff_pad - d_ff)))
    mode = dict(pipeline_mode=pl.Buffered(1)) if (b // nb) * nt == 1 else {}
    x2, tails = pl.pallas_call(
        functools.partial(_ffn_kernel, n_sub_valid=d_ff // tf),
        grid=(b // nb, nt, nj),
        in_specs=[pl.BlockSpec((nb, tt, d), lambda bi, i, j: (bi, i, 0), **mode),
                  pl.BlockSpec((1, d), lambda bi, i, j: (0, 0)),
                  pl.BlockSpec((d, step_cols), lambda bi, i, j: (0, j)),
                  pl.BlockSpec((d, step_cols), lambda bi, i, j: (0, nj + j)),
                  pl.BlockSpec((CONV_W, step_cols), lambda bi, i, j: (0, j)),
                  pl.BlockSpec((1, step_cols), lambda bi, i, j: (0, j)),
                  pl.BlockSpec((step_cols, d), lambda bi, i, j: (j, 0)),
                  pl.BlockSpec((1, d), lambda bi, i, j: (0, 0)),
                  pl.BlockSpec((nb, SUBLANES, step_cols), lambda bi, i, j: (bi, 0, j))],
        out_specs=[pl.BlockSpec((nb, tt, d), lambda bi, i, j: (bi, i, 0), **mode),
                   pl.BlockSpec((nb, None, SUBLANES, step_cols), lambda bi, i, j: (bi, i, 0, j))],
        out_shape=[jax.ShapeDtypeStruct((b, t, d), F32),
                   jax.ShapeDtypeStruct((b, nt, SUBLANES, d_ff_pad), F32)],
        scratch_shapes=[pltpu.VMEM((nb * tt, d), BF16),
                        pltpu.VMEM((nb, tt + SUBLANES, tf), F32), pltpu.VMEM((nb * tt, tf), F32),
                        pltpu.VMEM((n_sub, nb, SUBLANES, tf), F32)],
        compiler_params=_params(("arbitrary", "arbitrary", "arbitrary"), 60),
        name="ffn",
    )(x3d, g_pre.reshape(1, d), w_up, w_up, conv_w, conv_b, w_down, g_post.reshape(1, d), conv_prev8)
    return x2, tails[..., :d_ff]


def _tiles(b, t):
    row_tile = 1024 if b * t <= 1024 else 512
    tt = min(t, row_tile)
    return dict(
        proj_rows=512,
        attn_batch=max(1, min(b, 1024 // t)),
        s5_steps=min(t, 512 // b),
        memattn_rows=tt,
        mix_rows=512, mix_cols=512,
        ffn_batch=min(b, row_tile // tt), ffn_rows=tt,
        ffn_step_cols=FFN_COLS * (FFN_MAX_SUB_STEPS if row_tile == 512 else 1))


FFN_COLS = 2 * MXU_WIDTH
FFN_MAX_SUB_STEPS = 2


def _layer(x, attn_past, s0, conv_prev, mem_k, mem_v, bias, lw):
    b, t, d = x.shape
    m = b * t
    q_w, kv_w, ssm_w, mem_w = lw['q_w'], lw['kv_w'], lw['ssm_w'], lw['mem_w']
    n_state = lw['n_state']
    d_ff = lw['d_ff']
    tiles = _tiles(b, t)
    x2d = x.reshape(m, d)

    q, kv, u, qm, h = _norm_proj(x2d, lw['norm_pre_mix'], lw['proj_outputs'], tiles['proj_rows'])
    kv3 = kv.reshape(b, t, 2 * kv_w)
    if attn_past is None:
        hist_k = jnp.zeros((b, WINDOW, kv_w), F32)
        hist_v = hist_k
        first_valid = WINDOW
    else:
        hist_k = attn_past[0].astype(F32).reshape(b, WINDOW, kv_w)
        hist_v = attn_past[1].astype(F32).reshape(b, WINDOW, kv_w)
        first_valid = 0
    o_a = _band_attention(q.reshape(b, t, q_w), kv3, hist_k, hist_v, bias, lw['attn_sinks'],
                          tiles['attn_batch'], first_valid).reshape(m, q_w)

    if s0 is None:
        s0_re = jnp.zeros((b, n_state), F32)
        s0_im = s0_re
    else:
        s0_re = s0[0].astype(F32).reshape(b, n_state)
        s0_im = s0[1].astype(F32).reshape(b, n_state)
    u_tb = u.reshape(b, t, ssm_w).transpose(1, 0, 2)
    o_s_tb, s_re, s_im = _s5_layer(u_tb, s0_re, s0_im, lw['s5'], tiles['s5_steps'])
    o_s = o_s_tb.reshape(t, b, ssm_w).transpose(1, 0, 2).reshape(m, ssm_w)

    o_m = _memory_attention(qm, mem_k, mem_v, b, t, tiles['memattn_rows'])

    x1 = _mix_residual(x2d, h, o_a, o_s, o_m, lw['w_in_bf'], lw['gate_col0'], lw['w_oa'],
                       lw['w_out_bf'], lw['norm_post_mix'], tiles['mix_rows'], tiles['mix_cols'])

    if conv_prev is None:
        conv_prev8 = jnp.zeros((b, SUBLANES, d_ff), F32)
    else:
        conv_prev8 = jnp.pad(conv_prev.astype(F32), ((0, 0), (SUBLANES - (CONV_W - 1), 0), (0, 0)))
    x2, tails = _conv_ffn(x1.reshape(b, t, d), lw['norm_pre_ffn'], lw['ffn_padded'], d_ff, lw['norm_post_ffn'],
                          conv_prev8, tiles['ffn_batch'], tiles['ffn_rows'], FFN_COLS, tiles['ffn_step_cols'])
    conv_new = tails[:, -1, SUBLANES - (CONV_W - 1):, :]

    n_kv = kv_w // HEAD_DIM
    k_new = jnp.concatenate([hist_k, kv3[:, :, :kv_w]], axis=1)[:, -WINDOW:].reshape(b, WINDOW, n_kv, HEAD_DIM)
    v_new = jnp.concatenate([hist_v, kv3[:, :, kv_w:]], axis=1)[:, -WINDOW:].reshape(b, WINDOW, n_kv, HEAD_DIM)
    return x2, k_new, v_new, s_re, s_im, conv_new


def kernel(x_prompt, x_sample, cache_attn_k, cache_attn_v, cache_mem_k, cache_mem_v, state_ssm_re, state_ssm_im, state_conv, mem_prompt, rel_bias_table, norm_pre_mix, norm_post_mix, norm_pre_ffn, norm_post_ffn, norm_mem, w_in, attn_sinks, ssm_a_re, ssm_a_im, ssm_log_dt, ssm_b_re, ssm_b_im, ssm_c_re, ssm_c_im, ssm_d, w_glu, w_mem_kv, w_out, w_up, conv_w, conv_b, w_down):
    depth = w_in.shape[0]
    bp, _, d = x_prompt.shape
    n_mem = mem_prompt.shape[1]
    n_q = attn_sinks.shape[1]
    n_kv, hd = cache_attn_k.shape[-2:]
    assert hd == HEAD_DIM and n_kv == N_KV and cache_attn_k.shape[2] == WINDOW
    groups, p_state = ssm_a_re.shape[1:]
    q_w, kv_w = n_q * HEAD_DIM, n_kv * HEAD_DIM
    ssm_w = ssm_d.shape[1]
    mem_w = w_mem_kv.shape[2] // 2
    proj_w = q_w + 2 * kv_w + ssm_w + mem_w
    mem_hd = mem_w // MEM_HEADS

    xp, xs = x_prompt, x_sample
    outs = [[] for _ in range(12)]
    for l in range(depth):
        bias = _rel_bias(rel_bias_table)
        lp = dict(ssm_a_re=ssm_a_re[l], ssm_a_im=ssm_a_im[l], ssm_log_dt=ssm_log_dt[l],
                  ssm_b_re=ssm_b_re[l], ssm_b_im=ssm_b_im[l], ssm_c_re=ssm_c_re[l], ssm_c_im=ssm_c_im[l],
                  ssm_d=ssm_d[l], w_glu=w_glu[l])
        w_in_bf = w_in[l].astype(BF16)
        w_out_bf = w_out[l].astype(BF16)
        grp = n_q // n_kv
        w_q = w_in_bf[:, :q_w].reshape(d, n_kv, grp, HEAD_DIM).transpose(0, 2, 1, 3).reshape(d, q_w)
        w_oa = w_out_bf[:q_w].reshape(n_kv, grp, HEAD_DIM, d).transpose(1, 0, 2, 3).reshape(q_w, d)
        u_off = q_w + 2 * kv_w
        proj_outputs = [(BF16, [(w_q, q_w, 0)]),
                        (F32, _col_blocks(w_in_bf, q_w, 2 * kv_w)),
                        (F32, _col_blocks(w_in_bf, u_off, ssm_w)),
                        (BF16, _col_blocks(w_in_bf, u_off + ssm_w, mem_w)),
                        (BF16, [])]
        lw = dict(q_w=q_w, kv_w=kv_w, ssm_w=ssm_w, mem_w=mem_w, n_state=groups * p_state,
                  norm_pre_mix=norm_pre_mix[l], norm_post_mix=norm_post_mix[l],
                  norm_pre_ffn=norm_pre_ffn[l], norm_post_ffn=norm_post_ffn[l],
                  proj_outputs=proj_outputs, w_in_bf=w_in_bf, gate_col0=proj_w, w_oa=w_oa, w_out_bf=w_out_bf,
                  attn_sinks=attn_sinks[l], s5=_s5_params(lp),
                  d_ff=w_down.shape[1],
                  ffn_padded=_ffn_pad_weights(w_up[l].astype(BF16), conv_w[l], conv_b[l], w_down[l].astype(BF16),
                                              FFN_COLS * FFN_MAX_SUB_STEPS))

        w_mem_bf = w_mem_kv[l].astype(BF16)
        mk_p, mv_p = _norm_proj(mem_prompt.reshape(bp * n_mem, d), norm_mem[l],
                                [(F32, [(w_mem_bf, mem_w, 0)]), (F32, [(w_mem_bf, mem_w, 1)])],
                                _tiles(bp, n_mem)['proj_rows'])
        mk_p = mk_p.reshape(bp, n_mem, mem_w)
        mv_p = mv_p.reshape(bp, n_mem, mem_w)
        xp, k_p, v_p, sr_p, si_p, c_p = _layer(xp, None, None, None, mk_p, mv_p, bias, lw)

        bs = xs.shape[0]
        xs, k_s, v_s, sr_s, si_s, c_s = _layer(
            xs, (cache_attn_k[l], cache_attn_v[l]), (state_ssm_re[l], state_ssm_im[l]), state_conv[l],
            cache_mem_k[l].reshape(bs, n_mem, mem_w), cache_mem_v[l].reshape(bs, n_mem, mem_w), bias, lw)

        vals = (k_p, v_p, sr_p.reshape(bp, groups, p_state), si_p.reshape(bp, groups, p_state), c_p,
                mk_p.reshape(bp, n_mem, MEM_HEADS, mem_hd), mv_p.reshape(bp, n_mem, MEM_HEADS, mem_hd),
                k_s, v_s, sr_s.reshape(bs, groups, p_state), si_s.reshape(bs, groups, p_state), c_s)
        for acc, val in zip(outs, vals):
            acc.append(val)
    return (xp, xs) + tuple(jnp.stack(o) for o in outs)
```

```python
import functools
import math

import jax
import jax.numpy as jnp
from jax import lax
from jax.experimental import pallas as pl
from jax.experimental.pallas import tpu as pltpu

F32 = jnp.float32
BF16 = jnp.bfloat16

EPS = 1e-6
NEG_INF = -1e30
CHUNK = 64
WINDOW = 128
HEAD_DIM = 64
N_KV = 4
MAX_DISTANCE = 128
SSM_GROUP_CH = 16
MEM_HEADS = 4
CONV_W = 3

MIB = 1024 * 1024
LANES = 128
MXU_WIDTH = 256
SUBLANES = 8
SCAN_CARRY_ELEMS = 4096


def _params(semantics, vmem_mib):
    return pltpu.CompilerParams(dimension_semantics=semantics, vmem_limit_bytes=vmem_mib * MIB)


def _rms(x, g):
    y = x * lax.rsqrt(jnp.mean(x * x, axis=-1, keepdims=True) + EPS)
    return y * g


def _dot(a, b):
    return jnp.dot(a, b, preferred_element_type=F32)


def _dot_t(a, b):
    return lax.dot_general(a, b, (((1,), (1,)), ((), ())), preferred_element_type=F32)


def _proj_kernel(x_ref, g_ref, *refs, blocks_per_out):
    n_w = sum(blocks_per_out)
    w_refs, o_refs = refs[:n_w], refs[n_w:]
    h = _rms(x_ref[...], g_ref[...]).astype(BF16)
    first = 0
    for o_ref, count in zip(o_refs, blocks_per_out):
        if count == 0:
            o_ref[...] = h
            continue
        parts = [_dot(h, w_ref[...]) for w_ref in w_refs[first:first + count]]
        r = parts[0] if count == 1 else jnp.concatenate(parts, axis=1)
        o_ref[...] = r.astype(o_ref.dtype)
        first += count


def _col_blocks(arr, start, width):
    blk = math.gcd(start, width) if start else width
    assert blk % LANES == 0
    return [(arr, blk, start // blk + k) for k in range(width // blk)]


def _norm_proj(x2d, g, outputs, tm):
    m, d = x2d.shape
    w_args, w_specs, out_specs, out_shapes = [], [], [], []
    for dt, blocks in outputs:
        for arr, width, blk in blocks:
            assert arr.shape[0] == d and arr.shape[1] % width == 0
            w_args.append(arr)
            w_specs.append(pl.BlockSpec((d, width), lambda i, blk=blk: (0, blk)))
        total = sum(width for _, width, _ in blocks) if blocks else d
        out_specs.append(pl.BlockSpec((tm, total), lambda i: (i, 0)))
        out_shapes.append(jax.ShapeDtypeStruct((m, total), dt))
    return pl.pallas_call(
        functools.partial(_proj_kernel, blocks_per_out=tuple(len(b) for _, b in outputs)),
        grid=(m // tm,),
        in_specs=[pl.BlockSpec((tm, d), lambda i: (i, 0)),
                  pl.BlockSpec((1, d), lambda i: (0, 0))] + w_specs,
        out_specs=out_specs,
        out_shape=out_shapes,
        compiler_params=_params(("parallel",), 48),
        name="proj",
    )(x2d, g.reshape(1, d), *w_args)


def _bias_kernel(idx_ref, table_ref, o_ref, *, n_buckets, n_heads):
    idx = idx_ref[...]
    for h in range(n_heads):
        acc = jnp.zeros(idx.shape, F32)
        for b in range(n_buckets):
            acc = jnp.where(idx == b, table_ref[b, h], acc)
        o_ref[h] = acc


def _t5_bucket(rel, n_buckets):
    half = n_buckets // 2
    max_exact = half // 2
    n = jnp.abs(rel)
    large = max_exact + (jnp.log(jnp.maximum(n, 1).astype(F32) / max_exact)
                         / math.log(MAX_DISTANCE / max_exact) * (half - max_exact)).astype(jnp.int32)
    large = jnp.minimum(large, half - 1)
    return jnp.where(rel > 0, half, 0) + jnp.where(n < max_exact, n, large)


def _rel_bias(table):
    n_buckets, n_heads = table.shape
    lk = WINDOW + CHUNK
    rel = jnp.arange(lk)[None, :] - WINDOW - jnp.arange(CHUNK)[:, None]
    idx = _t5_bucket(rel, n_buckets).astype(jnp.int32)
    return pl.pallas_call(
        functools.partial(_bias_kernel, n_buckets=n_buckets, n_heads=n_heads),
        in_specs=[pl.BlockSpec((CHUNK, lk), lambda: (0, 0)),
                  pl.BlockSpec(memory_space=pltpu.SMEM)],
        out_specs=pl.BlockSpec((n_heads, CHUNK, lk), lambda: (0, 0, 0)),
        out_shape=jax.ShapeDtypeStruct((n_heads, CHUNK, lk), F32),
        name="bias",
    )(idx, table.astype(F32))


def _attn_kernel(q_ref, k_ref, v_ref, hk_ref, hv_ref, bias_ref, o_ref, kf_scr, vf_scr, s_scr, *, n_masked, grp):
    nb, t, _ = q_ref.shape
    kv_w = k_ref.shape[2]
    n_chunks = t // CHUNK
    lk = WINDOW + CHUNK
    lkp = bias_ref.shape[2]
    pair_w = 2 * HEAD_DIM
    scale = HEAD_DIM ** -0.5
    low_half = lax.broadcasted_iota(jnp.int32, (1, pair_w), 1) < HEAD_DIM
    ones = jnp.ones((lkp, pair_w), BF16)
    kv_pad = jnp.zeros((lkp - lk, kv_w), BF16)

    kf_scr[:, 0:WINDOW, :] = hk_ref[...].astype(BF16)
    kf_scr[:, WINDOW:, :] = k_ref[...].astype(BF16)
    vf_scr[:, 0:WINDOW, :] = hv_ref[...].astype(BF16)
    vf_scr[:, WINDOW:, :] = v_ref[...].astype(BF16)

    n_total = nb * n_chunks

    def locate(n):
        bi = n // n_chunks
        c = n % n_chunks
        return bi, c, pl.multiple_of(c * CHUNK, CHUNK)

    def score_chunk(n, slot):
        bi, c, r0 = locate(n)
        q = q_ref[bi, pl.ds(r0, CHUNK), :] * scale
        k = jnp.concatenate([kf_scr[bi, pl.ds(r0, lk), :], kv_pad], axis=0)
        variant = jnp.minimum(c, n_masked) * N_KV
        for h in range(N_KV):
            j, e = divmod(h, 2)
            qp = jnp.concatenate([q[:, g * kv_w + j * pair_w:g * kv_w + (j + 1) * pair_w]
                                  for g in range(grp)], axis=0)
            in_head = low_half if e == 0 else jnp.logical_not(low_half)
            qh = jnp.where(in_head, qp, jnp.zeros_like(qp))
            s_scr[slot, h] = _dot_t(qh, k[:, j * pair_w:(j + 1) * pair_w]) + bias_ref[variant + h]

    def chunk(n, slot):
        score_chunk(jnp.minimum(n + 1, n_total - 1), 1 - slot)
        bi, c, r0 = locate(n)
        v = jnp.concatenate([vf_scr[bi, pl.ds(r0, lk), :], kv_pad], axis=0)
        scores = [s_scr[slot, h] for h in range(N_KV)]
        probs = [jnp.exp(s - jnp.max(s, axis=-1, keepdims=True)).astype(BF16) for s in scores]
        v_ext = [jnp.concatenate([v[:, j * pair_w:(j + 1) * pair_w], ones], axis=1) for j in range(N_KV // 2)]
        sums = [_dot(p, v_ext[h // 2]) for h, p in enumerate(probs)]
        outs = [r[:, :pair_w] / r[:, pair_w:] for r in sums]
        for j in range(N_KV // 2):
            o_pair = jnp.where(low_half, outs[2 * j], outs[2 * j + 1])
            for g in range(grp):
                o_ref[bi, pl.ds(r0, CHUNK), g * kv_w + j * pair_w:g * kv_w + (j + 1) * pair_w] = (
                    o_pair[g * CHUNK:(g + 1) * CHUNK].astype(BF16))

    def chunk_pair(p, carry):
        chunk(2 * p, 0)
        chunk(2 * p + 1, 1)
        return carry

    assert n_total % 2 == 0
    score_chunk(0, 0)
    lax.fori_loop(0, n_total // 2, chunk_pair, 0)


def _band_attention(q3, kv3, hist_k, hist_v, bias, sinks, nb, first_valid):
    b, t, q_w = q3.shape
    kv_w = hist_k.shape[2]
    grp = q_w // HEAD_DIM // N_KV
    lk = WINDOW + CHUNK
    lkp = 2 * LANES
    n_masked = first_valid // CHUNK
    sink_col = jnp.broadcast_to(sinks.astype(F32).reshape(N_KV, grp, 1, 1), (N_KV, grp, CHUNK, 1))
    ext = jnp.concatenate([bias.reshape(N_KV, grp * CHUNK, lk), sink_col.reshape(N_KV, grp * CHUNK, 1),
                           jnp.full((N_KV, grp * CHUNK, lkp - lk - 1), NEG_INF, F32)], axis=2)
    key = jnp.arange(lkp)
    variants = [jnp.where(key < (n_masked - i) * CHUNK, NEG_INF, ext) for i in range(n_masked + 1)]
    bias_ext = jnp.concatenate(variants, axis=0)
    return pl.pallas_call(
        functools.partial(_attn_kernel, n_masked=n_masked, grp=grp),
        grid=(b // nb,),
        in_specs=[pl.BlockSpec((nb, t, q_w), lambda i: (i, 0, 0)),
                  pl.BlockSpec((nb, t, kv_w), lambda i: (i, 0, 0)),
                  pl.BlockSpec((nb, t, kv_w), lambda i: (i, 0, 1)),
                  pl.BlockSpec((nb, WINDOW, kv_w), lambda i: (i, 0, 0)),
                  pl.BlockSpec((nb, WINDOW, kv_w), lambda i: (i, 0, 0)),
                  pl.BlockSpec(bias_ext.shape, lambda i: (0, 0, 0))],
        out_specs=pl.BlockSpec((nb, t, q_w), lambda i: (i, 0, 0)),
        out_shape=jax.ShapeDtypeStruct((b, t, q_w), BF16),
        scratch_shapes=[pltpu.VMEM((nb, WINDOW + t, kv_w), BF16), pltpu.VMEM((nb, WINDOW + t, kv_w), BF16),
                        pltpu.VMEM((2, N_KV, grp * CHUNK, lkp), F32)],
        compiler_params=_params(("parallel",), 40),
        name="attn",
    )(q3, kv3, kv3, hist_k, hist_v, bias_ext)


def _gelu(x):
    return jax.nn.gelu(x)


def _s5_kernel(u_ref, s0re_ref, s0im_ref, are_ref, aim_ref, wbre_ref, wbim_ref, wcre_ref, wcim_ref,
               d_ref, wglu_ref, o_ref, sre_out_ref, sim_out_ref,
               sre_scr, sim_scr, st_re, st_im, y_scr, *, batch, steps, lane_chunk):
    i = pl.program_id(0)

    @pl.when(i == 0)
    def _():
        st_re[...] = s0re_ref[...]
        st_im[...] = s0im_ref[...]

    u = u_ref[...]
    ub = u.astype(BF16)
    n_in_blk, in_blk, st_blk = wbre_ref.shape
    for r in range(n_in_blk):
        blk = ub[:, r * in_blk:(r + 1) * in_blk]
        sre_scr[:, r * st_blk:(r + 1) * st_blk] = _dot(blk, wbre_ref[r])
        sim_scr[:, r * st_blk:(r + 1) * st_blk] = _dot(blk, wbim_ref[r])

    n_state = sre_scr.shape[1]
    for lc in range(n_state // lane_chunk):
        sl = slice(lc * lane_chunk, (lc + 1) * lane_chunk)
        ar = are_ref[:, sl]
        ai = aim_ref[:, sl]

        def body(t, carry, sl=sl, ar=ar, ai=ai):
            sr, si = carry
            rows = pl.ds(pl.multiple_of(t * batch, batch), batch)
            nr = ar * sr - ai * si + sre_scr[rows, sl]
            ni = ar * si + ai * sr + sim_scr[rows, sl]
            sre_scr[rows, sl] = nr
            sim_scr[rows, sl] = ni
            return nr, ni

        sr, si = lax.fori_loop(0, steps, body, (st_re[:, sl], st_im[:, sl]), unroll=8)
        st_re[:, sl] = sr
        st_im[:, sl] = si

    n_out_blk, k_blk, out_blk = wcre_ref.shape
    for kb in range(n_out_blk):
        ksl = slice(kb * k_blk, (kb + 1) * k_blk)
        y_scr[:, kb * out_blk:(kb + 1) * out_blk] = (
            _dot(sre_scr[:, ksl].astype(BF16), wcre_ref[kb])
            + _dot(sim_scr[:, ksl].astype(BF16), wcim_ref[kb]))
    y = _gelu(y_scr[...] + d_ref[...] * u)
    o_ref[...] = (y * jax.nn.sigmoid(_dot(y.astype(BF16), wglu_ref[...]))).astype(BF16)

    @pl.when(i == pl.num_programs(0) - 1)
    def _():
        sre_out_ref[...] = st_re[...]
        sim_out_ref[...] = st_im[...]


def _s5_params(lp):
    lam = lax.complex(lp['ssm_a_re'].astype(F32), lp['ssm_a_im'].astype(F32))
    dt = jnp.exp(lp['ssm_log_dt'].astype(F32))[:, None]
    a_bar = jnp.exp(lam * dt)
    b_mat = lax.complex(lp['ssm_b_re'].astype(F32), lp['ssm_b_im'].astype(F32))
    b_bar = ((a_bar - 1.0) / lam)[..., None] * b_mat
    g, p, c = b_bar.shape
    gb_in = 256 // c
    gb_out = 128 // c

    def b_blocks(x):
        x = x.reshape(g // gb_in, gb_in, p, c).transpose(0, 1, 3, 2)
        x = jnp.einsum('rgcp,gh->rgchp', x, jnp.eye(gb_in, dtype=F32))
        return x.reshape(g // gb_in, gb_in * c, gb_in * p).astype(BF16)

    def c_blocks(x):
        x = x.reshape(g // gb_out, gb_out, c, p).transpose(0, 1, 3, 2)
        x = jnp.einsum('kgpc,gh->kgphc', x, jnp.eye(gb_out, dtype=F32))
        return x.reshape(g // gb_out, gb_out * p, gb_out * c).astype(BF16)

    return dict(a_re=jnp.real(a_bar).reshape(1, g * p), a_im=jnp.imag(a_bar).reshape(1, g * p),
                wb_re=b_blocks(jnp.real(b_bar)), wb_im=b_blocks(jnp.imag(b_bar)),
                wc_re=c_blocks(lp['ssm_c_re'].astype(F32)), wc_im=c_blocks(-lp['ssm_c_im'].astype(F32)),
                d=lp['ssm_d'].astype(F32).reshape(1, -1), w_glu=lp['w_glu'].astype(BF16))


def _s5_layer(u_tb, s0_re, s0_im, sp, steps):
    t, b, ssm_w = u_tb.shape
    n_state = s0_re.shape[1]
    rows = steps * b
    lane_chunk = SCAN_CARRY_ELEMS // b
    a_re = jnp.broadcast_to(sp['a_re'], (b, n_state))
    a_im = jnp.broadcast_to(sp['a_im'], (b, n_state))

    def full(x):
        nd = x.ndim
        return pl.BlockSpec(x.shape, lambda i: (0,) * nd)

    consts = [s0_re, s0_im, a_re, a_im, sp['wb_re'], sp['wb_im'], sp['wc_re'], sp['wc_im'], sp['d'], sp['w_glu']]
    return pl.pallas_call(
        functools.partial(_s5_kernel, batch=b, steps=steps, lane_chunk=lane_chunk),
        grid=(t // steps,),
        in_specs=[pl.BlockSpec((rows, ssm_w), lambda i: (i, 0))] + [full(x) for x in consts],
        out_specs=[pl.BlockSpec((rows, ssm_w), lambda i: (i, 0)),
                   pl.BlockSpec((b, n_state), lambda i: (0, 0)),
                   pl.BlockSpec((b, n_state), lambda i: (0, 0))],
        out_shape=[jax.ShapeDtypeStruct((t * b, ssm_w), BF16),
                   jax.ShapeDtypeStruct((b, n_state), F32),
                   jax.ShapeDtypeStruct((b, n_state), F32)],
        scratch_shapes=[pltpu.VMEM((rows, n_state), F32), pltpu.VMEM((rows, n_state), F32),
                        pltpu.VMEM((b, n_state), F32), pltpu.VMEM((b, n_state), F32),
                        pltpu.VMEM((rows, ssm_w), F32)],
        compiler_params=_params(("arbitrary",), 48),
        name="s5",
    )(u_tb.reshape(t * b, ssm_w), *consts)


def _memattn_kernel(q_ref, k_ref, v_ref, o_ref, *, head_dim):
    q = q_ref[...]
    k = k_ref[...].astype(BF16)
    v = v_ref[...].astype(BF16)
    scale = head_dim ** -0.5
    ones = jnp.ones((v.shape[0], head_dim), BF16)
    heads = [slice(h * head_dim, (h + 1) * head_dim) for h in range(MEM_HEADS)]
    scores = [_dot_t(q[:, sl], k[:, sl]) * scale for sl in heads]
    probs = [jnp.exp(s - jnp.max(s, axis=-1, keepdims=True)).astype(BF16) for s in scores]
    sums = [_dot(p, jnp.concatenate([v[:, sl], ones], axis=1)) for p, sl in zip(probs, heads)]
    for r, sl in zip(sums, heads):
        o_ref[:, sl] = (r[:, :head_dim] / r[:, head_dim:]).astype(BF16)


def _memory_attention(qm, mem_k, mem_v, b, t, tq):
    n_mem, mem_w = mem_k.shape[1:]
    nt = t // tq
    return pl.pallas_call(
        functools.partial(_memattn_kernel, head_dim=mem_w // MEM_HEADS),
        grid=(b, nt),
        in_specs=[pl.BlockSpec((tq, mem_w), lambda bi, ti: (bi * nt + ti, 0)),
                  pl.BlockSpec((None, n_mem, mem_w), lambda bi, ti: (bi, 0, 0)),
                  pl.BlockSpec((None, n_mem, mem_w), lambda bi, ti: (bi, 0, 0))],
        out_specs=pl.BlockSpec((tq, mem_w), lambda bi, ti: (bi * nt + ti, 0)),
        out_shape=jax.ShapeDtypeStruct((b * t, mem_w), BF16),
        compiler_params=_params(("parallel", "parallel"), 32),
        name="memattn",
    )(qm, mem_k, mem_v)


def _mix_kernel(x_ref, h_ref, oa_ref, os_ref, om_ref, wg0_ref, wg1_ref, wg2_ref,
                wa_ref, ws_ref, wm_ref, gpost_ref, o_ref, ss_scr, *, tn):
    j = pl.program_id(1)

    @pl.when(j == 0)
    def _():
        ss_scr[...] = jnp.zeros_like(ss_scr)

    h = h_ref[...]
    merged = (jax.nn.sigmoid(_dot(h, wg0_ref[...])) * _dot(oa_ref[...], wa_ref[...])
              + jax.nn.sigmoid(_dot(h, wg1_ref[...])) * _dot(os_ref[...], ws_ref[...])
              + jax.nn.sigmoid(_dot(h, wg2_ref[...])) * _dot(om_ref[...], wm_ref[...]))
    o_ref[:, pl.ds(pl.multiple_of(j * tn, tn), tn)] = merged
    sq = merged * merged
    ss_scr[...] += sum(sq[:, k * LANES:(k + 1) * LANES] for k in range(tn // LANES))

    @pl.when(j == pl.num_programs(1) - 1)
    def _():
        ss = jnp.sum(ss_scr[...], axis=-1, keepdims=True)
        inv = lax.rsqrt(ss / o_ref.shape[1] + EPS)
        o_ref[...] = x_ref[...] + (o_ref[...] * inv) * gpost_ref[...]


def _mix_residual(x2d, h, o_a, o_s, o_m, w_in_bf, gate_col0, w_oa, w_out_bf, g_post, tm, tn):
    m, d = x2d.shape
    nj = d // tn
    q_w, ssm_w, mem_w = o_a.shape[1], o_s.shape[1], o_m.shape[1]
    assert gate_col0 % tn == 0 and q_w % ssm_w == 0 and (q_w + ssm_w) % mem_w == 0
    gate_blk0 = gate_col0 // tn

    def rows(w):
        mode = dict(pipeline_mode=pl.Buffered(1)) if m == tm else {}
        return pl.BlockSpec((tm, w), lambda i, j: (i, 0), **mode)

    def gate_spec(br):
        return pl.BlockSpec((d, tn), lambda i, j: (0, gate_blk0 + br * nj + j))

    vec = pl.BlockSpec((1, d), lambda i, j: (0, 0))
    return pl.pallas_call(
        functools.partial(_mix_kernel, tn=tn),
        grid=(m // tm, nj),
        in_specs=[rows(d), rows(d), rows(q_w), rows(ssm_w), rows(mem_w),
                  gate_spec(0), gate_spec(1), gate_spec(2),
                  pl.BlockSpec((q_w, tn), lambda i, j: (0, j)),
                  pl.BlockSpec((ssm_w, tn), lambda i, j: (q_w // ssm_w, j)),
                  pl.BlockSpec((mem_w, tn), lambda i, j: ((q_w + ssm_w) // mem_w, j)), vec],
        out_specs=rows(d),
        out_shape=jax.ShapeDtypeStruct((m, d), F32),
        scratch_shapes=[pltpu.VMEM((tm, LANES), F32)],
        compiler_params=_params(("parallel", "arbitrary"), 56),
        name="mix",
    )(x2d, h, o_a, o_s, o_m, w_in_bf, w_in_bf, w_in_bf, w_oa, w_out_bf, w_out_bf,
      g_post.reshape(1, d))


def _ffn_kernel(x_ref, gpre_ref, *refs, n_sub, n_sub_valid):
    w_refs = refs[:3 * n_sub]
    cw_ref, cb_ref, gpost_ref, cprev_ref, o_ref, tail_ref, h_scr, a_scr, b_scr, carry_scr = refs[3 * n_sub:]
    j = pl.program_id(2)
    nb, tt, d = x_ref.shape
    tf = a_scr.shape[2]
    rows = nb * tt

    @pl.when(j == 0)
    def _():
        h_scr[...] = _rms(x_ref[...].reshape(rows, d), gpre_ref[...]).astype(BF16)
        o_ref[...] = jnp.zeros_like(o_ref)

    for s in range(n_sub):
        sub = j * n_sub + s
        step_cols = slice(s * tf, (s + 1) * tf)
        wua_ref, wub_ref, wd_ref = w_refs[3 * s:3 * s + 3]

        @pl.when(sub < n_sub_valid)
        def _(sub=sub, step_cols=step_cols, wua_ref=wua_ref, wub_ref=wub_ref, wd_ref=wd_ref):
            _ffn_sub_step(sub, step_cols, x_ref, wua_ref, wub_ref, cw_ref, cb_ref, wd_ref, cprev_ref,
                          o_ref, tail_ref, h_scr, a_scr, b_scr, carry_scr)

        @pl.when(sub >= n_sub_valid)
        def _(step_cols=step_cols):
            tail_ref[:, :, step_cols] = jnp.zeros((nb, SUBLANES, tf), F32)

    @pl.when(j == pl.num_programs(2) - 1)
    def _():
        f = _rms(o_ref[...].reshape(rows, d), gpost_ref[...]).reshape(nb, tt, d)
        o_ref[...] = x_ref[...] + f


def _ffn_sub_step(sub, step_cols, x_ref, wua_ref, wub_ref, cw_ref, cb_ref, wd_ref, cprev_ref,
                  o_ref, tail_ref, h_scr, a_scr, b_scr, carry_scr):
    i = pl.program_id(1)
    nb, tt, d = x_ref.shape
    tf = a_scr.shape[2]
    h = h_scr[...]
    a_scr[:, 0:SUBLANES, :] = jnp.where(i == 0, cprev_ref[:, :, step_cols], carry_scr[sub])
    n_slices = tf // MXU_WIDTH
    cols = [slice(s * MXU_WIDTH, (s + 1) * MXU_WIDTH) for s in range(n_slices)]
    for sl in cols:
        a_scr[:, SUBLANES:, sl] = _dot(h, wua_ref[:, sl]).reshape(nb, tt, MXU_WIDTH)
        b_scr[:, sl] = _dot(h, wub_ref[:, sl])
    cw = cw_ref[:, step_cols]
    cb = cb_ref[:, step_cols]
    for sl in cols:
        gated = []
        for bi in range(nb):
            a = a_scr[bi, SUBLANES:, sl]
            first = jnp.concatenate([a_scr[bi, 0:SUBLANES, sl], a[0:SUBLANES]], axis=0)
            prev1 = jnp.concatenate([first[SUBLANES - 1:2 * SUBLANES - 1],
                                     pltpu.roll(a, 1, 0)[SUBLANES:]], axis=0)
            prev2 = jnp.concatenate([first[SUBLANES - 2:2 * SUBLANES - 2],
                                     pltpu.roll(a, 2, 0)[SUBLANES:]], axis=0)
            conv = (prev2 * cw[0:1, sl] + prev1 * cw[1:2, sl] + a * cw[2:3, sl]) + cb[:, sl]
            gated.append(_gelu(conv) * b_scr[bi * tt:(bi + 1) * tt, sl])
        act = (gated[0] if nb == 1 else jnp.concatenate(gated, axis=0)).astype(BF16)
        o_ref[...] += _dot(act, wd_ref[sl, :]).reshape(nb, tt, d)
    tail = a_scr[:, tt:tt + SUBLANES, :]
    carry_scr[sub] = tail
    tail_ref[:, :, step_cols] = tail


def _conv_ffn(x3d, g_pre, w_up, conv_w, conv_b, w_down, g_post, conv_prev8, nb, tt, tf, n_sub):
    b, t, d = x3d.shape
    d_ff = w_down.shape[0]
    n_valid = d_ff // tf
    step_cols = n_sub * tf
    nj = -(-n_valid // n_sub)
    d_ff_pad = nj * step_cols
    nt = t // tt
    pad = ((0, 0), (0, d_ff_pad - d_ff))
    conv_w = jnp.pad(conv_w.astype(F32), pad)
    conv_b = jnp.pad(conv_b.astype(F32).reshape(1, d_ff), pad)
    conv_prev8 = jnp.pad(conv_prev8, ((0, 0),) + pad)

    def sub_block(s):
        return lambda j: jnp.minimum(j * n_sub + s, n_valid - 1)

    w_specs = []
    for s in range(n_sub):
        blk = sub_block(s)
        w_specs += [pl.BlockSpec((d, tf), lambda bi, i, j, blk=blk: (0, blk(j))),
                    pl.BlockSpec((d, tf), lambda bi, i, j, blk=blk: (0, n_valid + blk(j))),
                    pl.BlockSpec((tf, d), lambda bi, i, j, blk=blk: (blk(j), 0))]
    mode = dict(pipeline_mode=pl.Buffered(1)) if (b // nb) * nt == 1 else {}
    x2, tails = pl.pallas_call(
        functools.partial(_ffn_kernel, n_sub=n_sub, n_sub_valid=n_valid),
        grid=(b // nb, nt, nj),
        in_specs=[pl.BlockSpec((nb, tt, d), lambda bi, i, j: (bi, i, 0), **mode),
                  pl.BlockSpec((1, d), lambda bi, i, j: (0, 0))] + w_specs + [
                  pl.BlockSpec((CONV_W, step_cols), lambda bi, i, j: (0, j)),
                  pl.BlockSpec((1, step_cols), lambda bi, i, j: (0, j)),
                  pl.BlockSpec((1, d), lambda bi, i, j: (0, 0)),
                  pl.BlockSpec((nb, SUBLANES, step_cols), lambda bi, i, j: (bi, 0, j))],
        out_specs=[pl.BlockSpec((nb, tt, d), lambda bi, i, j: (bi, i, 0), **mode),
                   pl.BlockSpec((nb, None, SUBLANES, step_cols), lambda bi, i, j: (bi, i, 0, j))],
        out_shape=[jax.ShapeDtypeStruct((b, t, d), F32),
                   jax.ShapeDtypeStruct((b, nt, SUBLANES, d_ff_pad), F32)],
        scratch_shapes=[pltpu.VMEM((nb * tt, d), BF16),
                        pltpu.VMEM((nb, tt + SUBLANES, tf), F32), pltpu.VMEM((nb * tt, tf), F32),
                        pltpu.VMEM((nj * n_sub, nb, SUBLANES, tf), F32)],
        compiler_params=_params(("arbitrary", "arbitrary", "arbitrary"), 60),
        name="ffn",
    )(x3d, g_pre.reshape(1, d), *([w_up, w_up, w_down] * n_sub), conv_w, conv_b, g_post.reshape(1, d), conv_prev8)
    return x2, tails[..., :d_ff]


def _tiles(b, t):
    row_tile = 1024 if b * t <= 1024 else 512
    tt = min(t, row_tile)
    return dict(
        proj_rows=512,
        attn_batch=max(1, min(b, 1024 // t)),
        s5_steps=min(t, 512 // b),
        memattn_rows=tt,
        mix_rows=512, mix_cols=512,
        ffn_batch=min(b, row_tile // tt), ffn_rows=tt,
        ffn_sub_steps=2 if row_tile == 512 else 1)


FFN_COLS = 2 * MXU_WIDTH


def _layer(x, attn_past, s0, conv_prev, mem_k, mem_v, bias, lw):
    b, t, d = x.shape
    m = b * t
    q_w, kv_w, ssm_w, mem_w = lw['q_w'], lw['kv_w'], lw['ssm_w'], lw['mem_w']
    n_state = lw['n_state']
    d_ff = lw['d_ff']
    tiles = _tiles(b, t)
    x2d = x.reshape(m, d)

    q, kv, u, qm, h = _norm_proj(x2d, lw['norm_pre_mix'], lw['proj_outputs'], tiles['proj_rows'])
    kv3 = kv.reshape(b, t, 2 * kv_w)
    if attn_past is None:
        hist_k = jnp.zeros((b, WINDOW, kv_w), F32)
        hist_v = hist_k
        first_valid = WINDOW
    else:
        hist_k = attn_past[0].astype(F32).reshape(b, WINDOW, kv_w)
        hist_v = attn_past[1].astype(F32).reshape(b, WINDOW, kv_w)
        first_valid = 0
    o_a = _band_attention(q.reshape(b, t, q_w), kv3, hist_k, hist_v, bias, lw['attn_sinks'],
                          tiles['attn_batch'], first_valid).reshape(m, q_w)

    if s0 is None:
        s0_re = jnp.zeros((b, n_state), F32)
        s0_im = s0_re
    else:
        s0_re = s0[0].astype(F32).reshape(b, n_state)
        s0_im = s0[1].astype(F32).reshape(b, n_state)
    u_tb = u.reshape(b, t, ssm_w).transpose(1, 0, 2)
    o_s_tb, s_re, s_im = _s5_layer(u_tb, s0_re, s0_im, lw['s5'], tiles['s5_steps'])
    o_s = o_s_tb.reshape(t, b, ssm_w).transpose(1, 0, 2).reshape(m, ssm_w)

    o_m = _memory_attention(qm, mem_k, mem_v, b, t, tiles['memattn_rows'])

    x1 = _mix_residual(x2d, h, o_a, o_s, o_m, lw['w_in_bf'], lw['gate_col0'], lw['w_oa'],
                       lw['w_out_bf'], lw['norm_post_mix'], tiles['mix_rows'], tiles['mix_cols'])

    if conv_prev is None:
        conv_prev8 = jnp.zeros((b, SUBLANES, d_ff), F32)
    else:
        conv_prev8 = jnp.pad(conv_prev.astype(F32), ((0, 0), (SUBLANES - (CONV_W - 1), 0), (0, 0)))
    x2, tails = _conv_ffn(x1.reshape(b, t, d), lw['norm_pre_ffn'], lw['w_up'], lw['conv_w'], lw['conv_b'],
                          lw['w_down'], lw['norm_post_ffn'], conv_prev8,
                          tiles['ffn_batch'], tiles['ffn_rows'], FFN_COLS, tiles['ffn_sub_steps'])
    conv_new = tails[:, -1, SUBLANES - (CONV_W - 1):, :]

    n_kv = kv_w // HEAD_DIM
    k_new = jnp.concatenate([hist_k, kv3[:, :, :kv_w]], axis=1)[:, -WINDOW:].reshape(b, WINDOW, n_kv, HEAD_DIM)
    v_new = jnp.concatenate([hist_v, kv3[:, :, kv_w:]], axis=1)[:, -WINDOW:].reshape(b, WINDOW, n_kv, HEAD_DIM)
    return x2, k_new, v_new, s_re, s_im, conv_new


def kernel(x_prompt, x_sample, cache_attn_k, cache_attn_v, cache_mem_k, cache_mem_v, state_ssm_re, state_ssm_im, state_conv, mem_prompt, rel_bias_table, norm_pre_mix, norm_post_mix, norm_pre_ffn, norm_post_ffn, norm_mem, w_in, attn_sinks, ssm_a_re, ssm_a_im, ssm_log_dt, ssm_b_re, ssm_b_im, ssm_c_re, ssm_c_im, ssm_d, w_glu, w_mem_kv, w_out, w_up, conv_w, conv_b, w_down):
    depth = w_in.shape[0]
    bp, _, d = x_prompt.shape
    n_mem = mem_prompt.shape[1]
    n_q = attn_sinks.shape[1]
    n_kv, hd = cache_attn_k.shape[-2:]
    assert hd == HEAD_DIM and n_kv == N_KV and cache_attn_k.shape[2] == WINDOW
    groups, p_state = ssm_a_re.shape[1:]
    q_w, kv_w = n_q * HEAD_DIM, n_kv * HEAD_DIM
    ssm_w = ssm_d.shape[1]
    mem_w = w_mem_kv.shape[2] // 2
    proj_w = q_w + 2 * kv_w + ssm_w + mem_w
    mem_hd = mem_w // MEM_HEADS

    xp, xs = x_prompt, x_sample
    outs = [[] for _ in range(12)]
    for l in range(depth):
        bias = _rel_bias(rel_bias_table)
        lp = dict(ssm_a_re=ssm_a_re[l], ssm_a_im=ssm_a_im[l], ssm_log_dt=ssm_log_dt[l],
                  ssm_b_re=ssm_b_re[l], ssm_b_im=ssm_b_im[l], ssm_c_re=ssm_c_re[l], ssm_c_im=ssm_c_im[l],
                  ssm_d=ssm_d[l], w_glu=w_glu[l])
        w_in_bf = w_in[l].astype(BF16)
        w_out_bf = w_out[l].astype(BF16)
        grp = n_q // n_kv
        w_q = w_in_bf[:, :q_w].reshape(d, n_kv, grp, HEAD_DIM).transpose(0, 2, 1, 3).reshape(d, q_w)
        w_oa = w_out_bf[:q_w].reshape(n_kv, grp, HEAD_DIM, d).transpose(1, 0, 2, 3).reshape(q_w, d)
        u_off = q_w + 2 * kv_w
        proj_outputs = [(BF16, [(w_q, q_w, 0)]),
                        (F32, _col_blocks(w_in_bf, q_w, 2 * kv_w)),
                        (F32, _col_blocks(w_in_bf, u_off, ssm_w)),
                        (BF16, _col_blocks(w_in_bf, u_off + ssm_w, mem_w)),
                        (BF16, [])]
        lw = dict(q_w=q_w, kv_w=kv_w, ssm_w=ssm_w, mem_w=mem_w, n_state=groups * p_state,
                  norm_pre_mix=norm_pre_mix[l], norm_post_mix=norm_post_mix[l],
                  norm_pre_ffn=norm_pre_ffn[l], norm_post_ffn=norm_post_ffn[l],
                  proj_outputs=proj_outputs, w_in_bf=w_in_bf, gate_col0=proj_w, w_oa=w_oa, w_out_bf=w_out_bf,
                  attn_sinks=attn_sinks[l], s5=_s5_params(lp),
                  d_ff=w_down.shape[1], w_up=w_up[l].astype(BF16), conv_w=conv_w[l], conv_b=conv_b[l],
                  w_down=w_down[l].astype(BF16))

        w_mem_bf = w_mem_kv[l].astype(BF16)
        mk_p, mv_p = _norm_proj(mem_prompt.reshape(bp * n_mem, d), norm_mem[l],
                                [(F32, [(w_mem_bf, mem_w, 0)]), (F32, [(w_mem_bf, mem_w, 1)])],
                                _tiles(bp, n_mem)['proj_rows'])
        mk_p = mk_p.reshape(bp, n_mem, mem_w)
        mv_p = mv_p.reshape(bp, n_mem, mem_w)
        xp, k_p, v_p, sr_p, si_p, c_p = _layer(xp, None, None, None, mk_p, mv_p, bias, lw)

        bs = xs.shape[0]
        xs, k_s, v_s, sr_s, si_s, c_s = _layer(
            xs, (cache_attn_k[l], cache_attn_v[l]), (state_ssm_re[l], state_ssm_im[l]), state_conv[l],
            cache_mem_k[l].reshape(bs, n_mem, mem_w), cache_mem_v[l].reshape(bs, n_mem, mem_w), bias, lw)

        vals = (k_p, v_p, sr_p.reshape(bp, groups, p_state), si_p.reshape(bp, groups, p_state), c_p,
                mk_p.reshape(bp, n_mem, MEM_HEADS, mem_hd), mv_p.reshape(bp, n_mem, MEM_HEADS, mem_hd),
                k_s, v_s, sr_s.reshape(bs, groups, p_state), si_s.reshape(bs, groups, p_state), c_s)
        for acc, val in zip(outs, vals):
            acc.append(val)
    return (xp, xs) + tuple(jnp.stack(o) for o in outs)
```

```python
import functools
import math

import jax
import jax.numpy as jnp
from jax import lax
from jax.experimental import pallas as pl
from jax.experimental.pallas import tpu as pltpu

F32 = jnp.float32
BF16 = jnp.bfloat16

EPS = 1e-6
NEG_INF = -1e30
CHUNK = 64
WINDOW = 128
HEAD_DIM = 64
N_KV = 4
MAX_DISTANCE = 128
SSM_GROUP_CH = 16
MEM_HEADS = 4
CONV_W = 3

MIB = 1024 * 1024
LANES = 128
MXU_WIDTH = 256
SUBLANES = 8
SCAN_CARRY_ELEMS = 4096


def _params(semantics, vmem_mib):
    return pltpu.CompilerParams(dimension_semantics=semantics, vmem_limit_bytes=vmem_mib * MIB)


def _rms(x, g):
    y = x * lax.rsqrt(jnp.mean(x * x, axis=-1, keepdims=True) + EPS)
    return y * g


def _dot(a, b):
    return jnp.dot(a, b, preferred_element_type=F32)


def _dot_t(a, b):
    return lax.dot_general(a, b, (((1,), (1,)), ((), ())), preferred_element_type=F32)


def _proj_kernel(x_ref, g_ref, *refs, blocks_per_out):
    n_w = sum(blocks_per_out)
    w_refs, o_refs = refs[:n_w], refs[n_w:]
    h = _rms(x_ref[...], g_ref[...]).astype(BF16)
    first = 0
    for o_ref, count in zip(o_refs, blocks_per_out):
        if count == 0:
            o_ref[...] = h
            continue
        parts = [_dot(h, w_ref[...]) for w_ref in w_refs[first:first + count]]
        r = parts[0] if count == 1 else jnp.concatenate(parts, axis=1)
        o_ref[...] = r.astype(o_ref.dtype)
        first += count


def _col_blocks(arr, start, width):
    blk = math.gcd(start, width) if start else width
    assert blk % LANES == 0
    return [(arr, blk, start // blk + k) for k in range(width // blk)]


def _norm_proj(x2d, g, outputs, tm):
    m, d = x2d.shape
    w_args, w_specs, out_specs, out_shapes = [], [], [], []
    for dt, blocks in outputs:
        for arr, width, blk in blocks:
            assert arr.shape[0] == d and arr.shape[1] % width == 0
            w_args.append(arr)
            w_specs.append(pl.BlockSpec((d, width), lambda i, blk=blk: (0, blk)))
        total = sum(width for _, width, _ in blocks) if blocks else d
        out_specs.append(pl.BlockSpec((tm, total), lambda i: (i, 0)))
        out_shapes.append(jax.ShapeDtypeStruct((m, total), dt))
    return pl.pallas_call(
        functools.partial(_proj_kernel, blocks_per_out=tuple(len(b) for _, b in outputs)),
        grid=(m // tm,),
        in_specs=[pl.BlockSpec((tm, d), lambda i: (i, 0)),
                  pl.BlockSpec((1, d), lambda i: (0, 0))] + w_specs,
        out_specs=out_specs,
        out_shape=out_shapes,
        compiler_params=_params(("parallel",), 48),
        name="proj",
    )(x2d, g.reshape(1, d), *w_args)


def _bias_kernel(idx_ref, table_ref, o_ref, *, n_buckets, n_heads):
    idx = idx_ref[...]
    for h in range(n_heads):
        acc = jnp.zeros(idx.shape, F32)
        for b in range(n_buckets):
            acc = jnp.where(idx == b, table_ref[b, h], acc)
        o_ref[h] = acc


def _t5_bucket(rel, n_buckets):
    half = n_buckets // 2
    max_exact = half // 2
    n = jnp.abs(rel)
    large = max_exact + (jnp.log(jnp.maximum(n, 1).astype(F32) / max_exact)
                         / math.log(MAX_DISTANCE / max_exact) * (half - max_exact)).astype(jnp.int32)
    large = jnp.minimum(large, half - 1)
    return jnp.where(rel > 0, half, 0) + jnp.where(n < max_exact, n, large)


def _rel_bias(table):
    n_buckets, n_heads = table.shape
    lk = WINDOW + CHUNK
    rel = jnp.arange(lk)[None, :] - WINDOW - jnp.arange(CHUNK)[:, None]
    idx = _t5_bucket(rel, n_buckets).astype(jnp.int32)
    return pl.pallas_call(
        functools.partial(_bias_kernel, n_buckets=n_buckets, n_heads=n_heads),
        in_specs=[pl.BlockSpec((CHUNK, lk), lambda: (0, 0)),
                  pl.BlockSpec(memory_space=pltpu.SMEM)],
        out_specs=pl.BlockSpec((n_heads, CHUNK, lk), lambda: (0, 0, 0)),
        out_shape=jax.ShapeDtypeStruct((n_heads, CHUNK, lk), F32),
        name="bias",
    )(idx, table.astype(F32))


def _attn_kernel(q_ref, k_ref, v_ref, hk_ref, hv_ref, bias_ref, o_ref, kf_scr, vf_scr, s_scr, *, n_masked, grp):
    nb, t, _ = q_ref.shape
    kv_w = k_ref.shape[2]
    n_chunks = t // CHUNK
    lk = WINDOW + CHUNK
    lkp = bias_ref.shape[2]
    pair_w = 2 * HEAD_DIM
    scale = HEAD_DIM ** -0.5
    low_half = lax.broadcasted_iota(jnp.int32, (1, pair_w), 1) < HEAD_DIM
    ones = jnp.ones((lkp, pair_w), BF16)
    kv_pad = jnp.zeros((lkp - lk, kv_w), BF16)

    kf_scr[:, 0:WINDOW, :] = hk_ref[...].astype(BF16)
    kf_scr[:, WINDOW:, :] = k_ref[...].astype(BF16)
    vf_scr[:, 0:WINDOW, :] = hv_ref[...].astype(BF16)
    vf_scr[:, WINDOW:, :] = v_ref[...].astype(BF16)

    n_total = nb * n_chunks

    def locate(n):
        bi = n // n_chunks
        c = n % n_chunks
        return bi, c, pl.multiple_of(c * CHUNK, CHUNK)

    def score_chunk(n, slot):
        bi, c, r0 = locate(n)
        q = q_ref[bi, pl.ds(r0, CHUNK), :] * scale
        k = jnp.concatenate([kf_scr[bi, pl.ds(r0, lk), :], kv_pad], axis=0)
        variant = jnp.minimum(c, n_masked) * N_KV
        for h in range(N_KV):
            j, e = divmod(h, 2)
            qp = jnp.concatenate([q[:, g * kv_w + j * pair_w:g * kv_w + (j + 1) * pair_w]
                                  for g in range(grp)], axis=0)
            in_head = low_half if e == 0 else jnp.logical_not(low_half)
            qh = jnp.where(in_head, qp, jnp.zeros_like(qp))
            s_scr[slot, h] = _dot_t(qh, k[:, j * pair_w:(j + 1) * pair_w]) + bias_ref[variant + h]

    def chunk(n, slot):
        score_chunk(jnp.minimum(n + 1, n_total - 1), 1 - slot)
        bi, c, r0 = locate(n)
        v = jnp.concatenate([vf_scr[bi, pl.ds(r0, lk), :], kv_pad], axis=0)
        scores = [s_scr[slot, h] for h in range(N_KV)]
        probs = [jnp.exp(s - jnp.max(s, axis=-1, keepdims=True)).astype(BF16) for s in scores]
        v_ext = [jnp.concatenate([v[:, j * pair_w:(j + 1) * pair_w], ones], axis=1) for j in range(N_KV // 2)]
        sums = [_dot(p, v_ext[h // 2]) for h, p in enumerate(probs)]
        outs = [r[:, :pair_w] / r[:, pair_w:] for r in sums]
        for j in range(N_KV // 2):
            o_pair = jnp.where(low_half, outs[2 * j], outs[2 * j + 1])
            for g in range(grp):
                o_ref[bi, pl.ds(r0, CHUNK), g * kv_w + j * pair_w:g * kv_w + (j + 1) * pair_w] = (
                    o_pair[g * CHUNK:(g + 1) * CHUNK].astype(BF16))

    def chunk_pair(p, carry):
        chunk(2 * p, 0)
        chunk(2 * p + 1, 1)
        return carry

    assert n_total % 2 == 0
    score_chunk(0, 0)
    lax.fori_loop(0, n_total // 2, chunk_pair, 0)


def _band_attention(q3, kv3, hist_k, hist_v, bias, sinks, nb, first_valid):
    b, t, q_w = q3.shape
    kv_w = hist_k.shape[2]
    grp = q_w // HEAD_DIM // N_KV
    lk = WINDOW + CHUNK
    lkp = 2 * LANES
    n_masked = first_valid // CHUNK
    sink_col = jnp.broadcast_to(sinks.astype(F32).reshape(N_KV, grp, 1, 1), (N_KV, grp, CHUNK, 1))
    ext = jnp.concatenate([bias.reshape(N_KV, grp * CHUNK, lk), sink_col.reshape(N_KV, grp * CHUNK, 1),
                           jnp.full((N_KV, grp * CHUNK, lkp - lk - 1), NEG_INF, F32)], axis=2)
    key = jnp.arange(lkp)
    variants = [jnp.where(key < (n_masked - i) * CHUNK, NEG_INF, ext) for i in range(n_masked + 1)]
    bias_ext = jnp.concatenate(variants, axis=0)
    return pl.pallas_call(
        functools.partial(_attn_kernel, n_masked=n_masked, grp=grp),
        grid=(b // nb,),
        in_specs=[pl.BlockSpec((nb, t, q_w), lambda i: (i, 0, 0)),
                  pl.BlockSpec((nb, t, kv_w), lambda i: (i, 0, 0)),
                  pl.BlockSpec((nb, t, kv_w), lambda i: (i, 0, 1)),
                  pl.BlockSpec((nb, WINDOW, kv_w), lambda i: (i, 0, 0)),
                  pl.BlockSpec((nb, WINDOW, kv_w), lambda i: (i, 0, 0)),
                  pl.BlockSpec(bias_ext.shape, lambda i: (0, 0, 0))],
        out_specs=pl.BlockSpec((nb, t, q_w), lambda i: (i, 0, 0)),
        out_shape=jax.ShapeDtypeStruct((b, t, q_w), BF16),
        scratch_shapes=[pltpu.VMEM((nb, WINDOW + t, kv_w), BF16), pltpu.VMEM((nb, WINDOW + t, kv_w), BF16),
                        pltpu.VMEM((2, N_KV, grp * CHUNK, lkp), F32)],
        compiler_params=_params(("parallel",), 40),
        name="attn",
    )(q3, kv3, kv3, hist_k, hist_v, bias_ext)


def _gelu(x):
    return jax.nn.gelu(x)


def _s5_kernel(u_ref, s0re_ref, s0im_ref, are_ref, aim_ref, wbre_ref, wbim_ref, wcre_ref, wcim_ref,
               d_ref, wglu_ref, o_ref, sre_out_ref, sim_out_ref,
               sre_scr, sim_scr, st_re, st_im, y_scr, *, batch, steps, lane_chunk):
    i = pl.program_id(0)

    @pl.when(i == 0)
    def _():
        st_re[...] = s0re_ref[...]
        st_im[...] = s0im_ref[...]

    u = u_ref[...]
    ub = u.astype(BF16)
    n_in_blk, in_blk, st_blk = wbre_ref.shape
    for r in range(n_in_blk):
        blk = ub[:, r * in_blk:(r + 1) * in_blk]
        sre_scr[:, r * st_blk:(r + 1) * st_blk] = _dot(blk, wbre_ref[r])
        sim_scr[:, r * st_blk:(r + 1) * st_blk] = _dot(blk, wbim_ref[r])

    n_state = sre_scr.shape[1]
    for lc in range(n_state // lane_chunk):
        sl = slice(lc * lane_chunk, (lc + 1) * lane_chunk)
        ar = are_ref[:, sl]
        ai = aim_ref[:, sl]

        def body(t, carry, sl=sl, ar=ar, ai=ai):
            sr, si = carry
            rows = pl.ds(pl.multiple_of(t * batch, batch), batch)
            nr = ar * sr - ai * si + sre_scr[rows, sl]
            ni = ar * si + ai * sr + sim_scr[rows, sl]
            sre_scr[rows, sl] = nr
            sim_scr[rows, sl] = ni
            return nr, ni

        sr, si = lax.fori_loop(0, steps, body, (st_re[:, sl], st_im[:, sl]), unroll=8)
        st_re[:, sl] = sr
        st_im[:, sl] = si

    n_out_blk, k_blk, out_blk = wcre_ref.shape
    for kb in range(n_out_blk):
        ksl = slice(kb * k_blk, (kb + 1) * k_blk)
        y_scr[:, kb * out_blk:(kb + 1) * out_blk] = (
            _dot(sre_scr[:, ksl].astype(BF16), wcre_ref[kb])
            + _dot(sim_scr[:, ksl].astype(BF16), wcim_ref[kb]))
    y = _gelu(y_scr[...] + d_ref[...] * u)
    o_ref[...] = (y * jax.nn.sigmoid(_dot(y.astype(BF16), wglu_ref[...]))).astype(BF16)

    @pl.when(i == pl.num_programs(0) - 1)
    def _():
        sre_out_ref[...] = st_re[...]
        sim_out_ref[...] = st_im[...]


def _s5_params(lp):
    lam = lax.complex(lp['ssm_a_re'].astype(F32), lp['ssm_a_im'].astype(F32))
    dt = jnp.exp(lp['ssm_log_dt'].astype(F32))[:, None]
    a_bar = jnp.exp(lam * dt)
    b_mat = lax.complex(lp['ssm_b_re'].astype(F32), lp['ssm_b_im'].astype(F32))
    b_bar = ((a_bar - 1.0) / lam)[..., None] * b_mat
    g, p, c = b_bar.shape
    gb_in = 256 // c
    gb_out = 128 // c

    def b_blocks(x):
        x = x.reshape(g // gb_in, gb_in, p, c).transpose(0, 1, 3, 2)
        x = jnp.einsum('rgcp,gh->rgchp', x, jnp.eye(gb_in, dtype=F32))
        return x.reshape(g // gb_in, gb_in * c, gb_in * p).astype(BF16)

    def c_blocks(x):
        x = x.reshape(g // gb_out, gb_out, c, p).transpose(0, 1, 3, 2)
        x = jnp.einsum('kgpc,gh->kgphc', x, jnp.eye(gb_out, dtype=F32))
        return x.reshape(g // gb_out, gb_out * p, gb_out * c).astype(BF16)

    return dict(a_re=jnp.real(a_bar).reshape(1, g * p), a_im=jnp.imag(a_bar).reshape(1, g * p),
                wb_re=b_blocks(jnp.real(b_bar)), wb_im=b_blocks(jnp.imag(b_bar)),
                wc_re=c_blocks(lp['ssm_c_re'].astype(F32)), wc_im=c_blocks(-lp['ssm_c_im'].astype(F32)),
                d=lp['ssm_d'].astype(F32).reshape(1, -1), w_glu=lp['w_glu'].astype(BF16))


def _s5_layer(u_tb, s0_re, s0_im, sp, steps):
    t, b, ssm_w = u_tb.shape
    n_state = s0_re.shape[1]
    rows = steps * b
    lane_chunk = SCAN_CARRY_ELEMS // b
    a_re = jnp.broadcast_to(sp['a_re'], (b, n_state))
    a_im = jnp.broadcast_to(sp['a_im'], (b, n_state))

    def full(x):
        nd = x.ndim
        return pl.BlockSpec(x.shape, lambda i: (0,) * nd)

    consts = [s0_re, s0_im, a_re, a_im, sp['wb_re'], sp['wb_im'], sp['wc_re'], sp['wc_im'], sp['d'], sp['w_glu']]
    return pl.pallas_call(
        functools.partial(_s5_kernel, batch=b, steps=steps, lane_chunk=lane_chunk),
        grid=(t // steps,),
        in_specs=[pl.BlockSpec((rows, ssm_w), lambda i: (i, 0))] + [full(x) for x in consts],
        out_specs=[pl.BlockSpec((rows, ssm_w), lambda i: (i, 0)),
                   pl.BlockSpec((b, n_state), lambda i: (0, 0)),
                   pl.BlockSpec((b, n_state), lambda i: (0, 0))],
        out_shape=[jax.ShapeDtypeStruct((t * b, ssm_w), BF16),
                   jax.ShapeDtypeStruct((b, n_state), F32),
                   jax.ShapeDtypeStruct((b, n_state), F32)],
        scratch_shapes=[pltpu.VMEM((rows, n_state), F32), pltpu.VMEM((rows, n_state), F32),
                        pltpu.VMEM((b, n_state), F32), pltpu.VMEM((b, n_state), F32),
                        pltpu.VMEM((rows, ssm_w), F32)],
        compiler_params=_params(("arbitrary",), 48),
        name="s5",
    )(u_tb.reshape(t * b, ssm_w), *consts)


def _memattn_kernel(q_ref, k_ref, v_ref, o_ref, *, head_dim):
    q = q_ref[...]
    k = k_ref[...].astype(BF16)
    v = v_ref[...].astype(BF16)
    scale = head_dim ** -0.5
    ones = jnp.ones((v.shape[0], head_dim), BF16)
    heads = [slice(h * head_dim, (h + 1) * head_dim) for h in range(MEM_HEADS)]
    scores = [_dot_t(q[:, sl], k[:, sl]) * scale for sl in heads]
    probs = [jnp.exp(s - jnp.max(s, axis=-1, keepdims=True)).astype(BF16) for s in scores]
    sums = [_dot(p, jnp.concatenate([v[:, sl], ones], axis=1)) for p, sl in zip(probs, heads)]
    for r, sl in zip(sums, heads):
        o_ref[:, sl] = (r[:, :head_dim] / r[:, head_dim:]).astype(BF16)


def _memory_attention(qm, mem_k, mem_v, b, t, tq):
    n_mem, mem_w = mem_k.shape[1:]
    nt = t // tq
    return pl.pallas_call(
        functools.partial(_memattn_kernel, head_dim=mem_w // MEM_HEADS),
        grid=(b, nt),
        in_specs=[pl.BlockSpec((tq, mem_w), lambda bi, ti: (bi * nt + ti, 0)),
                  pl.BlockSpec((None, n_mem, mem_w), lambda bi, ti: (bi, 0, 0)),
                  pl.BlockSpec((None, n_mem, mem_w), lambda bi, ti: (bi, 0, 0))],
        out_specs=pl.BlockSpec((tq, mem_w), lambda bi, ti: (bi * nt + ti, 0)),
        out_shape=jax.ShapeDtypeStruct((b * t, mem_w), BF16),
        compiler_params=_params(("parallel", "parallel"), 32),
        name="memattn",
    )(qm, mem_k, mem_v)


def _mix_kernel(x_ref, h_ref, oa_ref, os_ref, om_ref, wg0_ref, wg1_ref, wg2_ref,
                wa_ref, ws_ref, wm_ref, gpost_ref, o_ref, ss_scr, *, tn):
    j = pl.program_id(1)

    @pl.when(j == 0)
    def _():
        ss_scr[...] = jnp.zeros_like(ss_scr)

    h = h_ref[...]
    oa, os_, om = oa_ref[...], os_ref[...], om_ref[...]
    partial = jnp.zeros(ss_scr.shape, F32)
    for c in range(tn // MXU_WIDTH):
        sl = slice(c * MXU_WIDTH, (c + 1) * MXU_WIDTH)
        merged = (jax.nn.sigmoid(_dot(h, wg0_ref[:, sl])) * _dot(oa, wa_ref[:, sl])
                  + jax.nn.sigmoid(_dot(h, wg1_ref[:, sl])) * _dot(os_, ws_ref[:, sl])
                  + jax.nn.sigmoid(_dot(h, wg2_ref[:, sl])) * _dot(om, wm_ref[:, sl]))
        o_ref[:, pl.ds(pl.multiple_of(j * tn + c * MXU_WIDTH, MXU_WIDTH), MXU_WIDTH)] = merged
        sq = merged * merged
        partial += sum(sq[:, k * LANES:(k + 1) * LANES] for k in range(MXU_WIDTH // LANES))
    ss_scr[...] += partial

    @pl.when(j == pl.num_programs(1) - 1)
    def _():
        ss = jnp.sum(ss_scr[...], axis=-1, keepdims=True)
        inv = lax.rsqrt(ss / o_ref.shape[1] + EPS)
        o_ref[...] = x_ref[...] + (o_ref[...] * inv) * gpost_ref[...]


def _mix_residual(x2d, h, o_a, o_s, o_m, w_in_bf, gate_col0, w_oa, w_out_bf, g_post, tm, tn):
    m, d = x2d.shape
    nj = d // tn
    q_w, ssm_w, mem_w = o_a.shape[1], o_s.shape[1], o_m.shape[1]
    assert gate_col0 % tn == 0 and q_w % ssm_w == 0 and (q_w + ssm_w) % mem_w == 0
    gate_blk0 = gate_col0 // tn

    def rows(w):
        mode = dict(pipeline_mode=pl.Buffered(1)) if m == tm else {}
        return pl.BlockSpec((tm, w), lambda i, j: (i, 0), **mode)

    def gate_spec(br):
        return pl.BlockSpec((d, tn), lambda i, j: (0, gate_blk0 + br * nj + j))

    vec = pl.BlockSpec((1, d), lambda i, j: (0, 0))
    return pl.pallas_call(
        functools.partial(_mix_kernel, tn=tn),
        grid=(m // tm, nj),
        in_specs=[rows(d), rows(d), rows(q_w), rows(ssm_w), rows(mem_w),
                  gate_spec(0), gate_spec(1), gate_spec(2),
                  pl.BlockSpec((q_w, tn), lambda i, j: (0, j)),
                  pl.BlockSpec((ssm_w, tn), lambda i, j: (q_w // ssm_w, j)),
                  pl.BlockSpec((mem_w, tn), lambda i, j: ((q_w + ssm_w) // mem_w, j)), vec],
        out_specs=rows(d),
        out_shape=jax.ShapeDtypeStruct((m, d), F32),
        scratch_shapes=[pltpu.VMEM((tm, LANES), F32)],
        compiler_params=_params(("parallel", "arbitrary"), 56),
        name="mix",
    )(x2d, h, o_a, o_s, o_m, w_in_bf, w_in_bf, w_in_bf, w_oa, w_out_bf, w_out_bf,
      g_post.reshape(1, d))


def _ffn_kernel(x_ref, gpre_ref, *refs, n_sub, n_sub_valid):
    w_refs = refs[:3 * n_sub]
    cw_ref, cb_ref, gpost_ref, cprev_ref, o_ref, tail_ref, h_scr, a_scr, b_scr, carry_scr = refs[3 * n_sub:]
    j = pl.program_id(2)
    nb, tt, d = x_ref.shape
    tf = a_scr.shape[2]
    rows = nb * tt

    @pl.when(j == 0)
    def _():
        h_scr[...] = _rms(x_ref[...].reshape(rows, d), gpre_ref[...]).astype(BF16)
        o_ref[...] = jnp.zeros_like(o_ref)

    for s in range(n_sub):
        sub = j * n_sub + s
        step_cols = slice(s * tf, (s + 1) * tf)
        wua_ref, wub_ref, wd_ref = w_refs[3 * s:3 * s + 3]

        @pl.when(sub < n_sub_valid)
        def _(sub=sub, step_cols=step_cols, wua_ref=wua_ref, wub_ref=wub_ref, wd_ref=wd_ref):
            _ffn_sub_step(sub, step_cols, x_ref, wua_ref, wub_ref, cw_ref, cb_ref, wd_ref, cprev_ref,
                          o_ref, tail_ref, h_scr, a_scr, b_scr, carry_scr)

        @pl.when(sub >= n_sub_valid)
        def _(step_cols=step_cols):
            tail_ref[:, :, step_cols] = jnp.zeros((nb, SUBLANES, tf), F32)

    @pl.when(j == pl.num_programs(2) - 1)
    def _():
        f = _rms(o_ref[...].reshape(rows, d), gpost_ref[...]).reshape(nb, tt, d)
        o_ref[...] = x_ref[...] + f


def _ffn_sub_step(sub, step_cols, x_ref, wua_ref, wub_ref, cw_ref, cb_ref, wd_ref, cprev_ref,
                  o_ref, tail_ref, h_scr, a_scr, b_scr, carry_scr):
    i = pl.program_id(1)
    nb, tt, d = x_ref.shape
    tf = a_scr.shape[2]
    h = h_scr[...]
    a_scr[:, 0:SUBLANES, :] = jnp.where(i == 0, cprev_ref[:, :, step_cols], carry_scr[sub])
    n_slices = tf // MXU_WIDTH
    cols = [slice(s * MXU_WIDTH, (s + 1) * MXU_WIDTH) for s in range(n_slices)]
    for sl in cols:
        a_scr[:, SUBLANES:, sl] = _dot(h, wua_ref[:, sl]).reshape(nb, tt, MXU_WIDTH)
        b_scr[:, sl] = _dot(h, wub_ref[:, sl])
    cw = cw_ref[:, step_cols]
    cb = cb_ref[:, step_cols]
    for sl in cols:
        gated = []
        for bi in range(nb):
            a = a_scr[bi, SUBLANES:, sl]
            first = jnp.concatenate([a_scr[bi, 0:SUBLANES, sl], a[0:SUBLANES]], axis=0)
            prev1 = jnp.concatenate([first[SUBLANES - 1:2 * SUBLANES - 1],
                                     pltpu.roll(a, 1, 0)[SUBLANES:]], axis=0)
            prev2 = jnp.concatenate([first[SUBLANES - 2:2 * SUBLANES - 2],
                                     pltpu.roll(a, 2, 0)[SUBLANES:]], axis=0)
            conv = (prev2 * cw[0:1, sl] + prev1 * cw[1:2, sl] + a * cw[2:3, sl]) + cb[:, sl]
            gated.append(_gelu(conv) * b_scr[bi * tt:(bi + 1) * tt, sl])
        act = (gated[0] if nb == 1 else jnp.concatenate(gated, axis=0)).astype(BF16)
        o_ref[...] += _dot(act, wd_ref[sl, :]).reshape(nb, tt, d)
    tail = a_scr[:, tt:tt + SUBLANES, :]
    carry_scr[sub] = tail
    tail_ref[:, :, step_cols] = tail


def _conv_ffn(x3d, g_pre, w_up, conv_w, conv_b, w_down, g_post, conv_prev8, nb, tt, tf, n_sub):
    b, t, d = x3d.shape
    d_ff = w_down.shape[0]
    n_valid = d_ff // tf
    step_cols = n_sub * tf
    nj = -(-n_valid // n_sub)
    d_ff_pad = nj * step_cols
    nt = t // tt
    pad = ((0, 0), (0, d_ff_pad - d_ff))
    conv_w = jnp.pad(conv_w.astype(F32), pad)
    conv_b = jnp.pad(conv_b.astype(F32).reshape(1, d_ff), pad)
    conv_prev8 = jnp.pad(conv_prev8, ((0, 0),) + pad)

    def sub_block(s):
        return lambda j: jnp.minimum(j * n_sub + s, n_valid - 1)

    w_specs = []
    for s in range(n_sub):
        blk = sub_block(s)
        w_specs += [pl.BlockSpec((d, tf), lambda bi, i, j, blk=blk: (0, blk(j))),
                    pl.BlockSpec((d, tf), lambda bi, i, j, blk=blk: (0, n_valid + blk(j))),
                    pl.BlockSpec((tf, d), lambda bi, i, j, blk=blk: (blk(j), 0))]
    mode = dict(pipeline_mode=pl.Buffered(1)) if (b // nb) * nt == 1 else {}
    x2, tails = pl.pallas_call(
        functools.partial(_ffn_kernel, n_sub=n_sub, n_sub_valid=n_valid),
        grid=(b // nb, nt, nj),
        in_specs=[pl.BlockSpec((nb, tt, d), lambda bi, i, j: (bi, i, 0), **mode),
                  pl.BlockSpec((1, d), lambda bi, i, j: (0, 0))] + w_specs + [
                  pl.BlockSpec((CONV_W, step_cols), lambda bi, i, j: (0, j)),
                  pl.BlockSpec((1, step_cols), lambda bi, i, j: (0, j)),
                  pl.BlockSpec((1, d), lambda bi, i, j: (0, 0)),
                  pl.BlockSpec((nb, SUBLANES, step_cols), lambda bi, i, j: (bi, 0, j))],
        out_specs=[pl.BlockSpec((nb, tt, d), lambda bi, i, j: (bi, i, 0), **mode),
                   pl.BlockSpec((nb, None, SUBLANES, step_cols), lambda bi, i, j: (bi, i, 0, j))],
        out_shape=[jax.ShapeDtypeStruct((b, t, d), F32),
                   jax.ShapeDtypeStruct((b, nt, SUBLANES, d_ff_pad), F32)],
        scratch_shapes=[pltpu.VMEM((nb * tt, d), BF16),
                        pltpu.VMEM((nb, tt + SUBLANES, tf), F32), pltpu.VMEM((nb * tt, tf), F32),
                        pltpu.VMEM((nj * n_sub, nb, SUBLANES, tf), F32)],
        compiler_params=_params(("arbitrary", "arbitrary", "arbitrary"), 60),
        name="ffn",
    )(x3d, g_pre.reshape(1, d), *([w_up, w_up, w_down] * n_sub), conv_w, conv_b, g_post.reshape(1, d), conv_prev8)
    return x2, tails[..., :d_ff]


def _tiles(b, t):
    row_tile = 512
    tt = min(t, row_tile)
    return dict(
        proj_rows=512,
        attn_batch=max(1, min(b, 1024 // t)),
        s5_steps=min(t, 512 // b),
        memattn_rows=tt,
        mix_rows=512, mix_cols=512,
        ffn_batch=min(b, row_tile // tt), ffn_rows=tt,
        ffn_sub_steps=2)


FFN_COLS = 2 * MXU_WIDTH


def _layer(x, attn_past, s0, conv_prev, mem_k, mem_v, bias, lw):
    b, t, d = x.shape
    m = b * t
    q_w, kv_w, ssm_w, mem_w = lw['q_w'], lw['kv_w'], lw['ssm_w'], lw['mem_w']
    n_state = lw['n_state']
    d_ff = lw['d_ff']
    tiles = _tiles(b, t)
    x2d = x.reshape(m, d)

    q, kv, u, qm, h = _norm_proj(x2d, lw['norm_pre_mix'], lw['proj_outputs'], tiles['proj_rows'])
    kv3 = kv.reshape(b, t, 2 * kv_w)
    if attn_past is None:
        hist_k = jnp.zeros((b, WINDOW, kv_w), F32)
        hist_v = hist_k
        first_valid = WINDOW
    else:
        hist_k = attn_past[0].astype(F32).reshape(b, WINDOW, kv_w)
        hist_v = attn_past[1].astype(F32).reshape(b, WINDOW, kv_w)
        first_valid = 0
    o_a = _band_attention(q.reshape(b, t, q_w), kv3, hist_k, hist_v, bias, lw['attn_sinks'],
                          tiles['attn_batch'], first_valid).reshape(m, q_w)

    if s0 is None:
        s0_re = jnp.zeros((b, n_state), F32)
        s0_im = s0_re
    else:
        s0_re = s0[0].astype(F32).reshape(b, n_state)
        s0_im = s0[1].astype(F32).reshape(b, n_state)
    u_tb = u.reshape(b, t, ssm_w).transpose(1, 0, 2)
    o_s_tb, s_re, s_im = _s5_layer(u_tb, s0_re, s0_im, lw['s5'], tiles['s5_steps'])
    o_s = o_s_tb.reshape(t, b, ssm_w).transpose(1, 0, 2).reshape(m, ssm_w)

    o_m = _memory_attention(qm, mem_k, mem_v, b, t, tiles['memattn_rows'])

    x1 = _mix_residual(x2d, h, o_a, o_s, o_m, lw['w_in_bf'], lw['gate_col0'], lw['w_oa'],
                       lw['w_out_bf'], lw['norm_post_mix'], tiles['mix_rows'], tiles['mix_cols'])

    if conv_prev is None:
        conv_prev8 = jnp.zeros((b, SUBLANES, d_ff), F32)
    else:
        conv_prev8 = jnp.pad(conv_prev.astype(F32), ((0, 0), (SUBLANES - (CONV_W - 1), 0), (0, 0)))
    x2, tails = _conv_ffn(x1.reshape(b, t, d), lw['norm_pre_ffn'], lw['w_up'], lw['conv_w'], lw['conv_b'],
                          lw['w_down'], lw['norm_post_ffn'], conv_prev8,
                          tiles['ffn_batch'], tiles['ffn_rows'], FFN_COLS, tiles['ffn_sub_steps'])
    conv_new = tails[:, -1, SUBLANES - (CONV_W - 1):, :]

    n_kv = kv_w // HEAD_DIM
    k_new = jnp.concatenate([hist_k, kv3[:, :, :kv_w]], axis=1)[:, -WINDOW:].reshape(b, WINDOW, n_kv, HEAD_DIM)
    v_new = jnp.concatenate([hist_v, kv3[:, :, kv_w:]], axis=1)[:, -WINDOW:].reshape(b, WINDOW, n_kv, HEAD_DIM)
    return x2, k_new, v_new, s_re, s_im, conv_new


def kernel(x_prompt, x_sample, cache_attn_k, cache_attn_v, cache_mem_k, cache_mem_v, state_ssm_re, state_ssm_im, state_conv, mem_prompt, rel_bias_table, norm_pre_mix, norm_post_mix, norm_pre_ffn, norm_post_ffn, norm_mem, w_in, attn_sinks, ssm_a_re, ssm_a_im, ssm_log_dt, ssm_b_re, ssm_b_im, ssm_c_re, ssm_c_im, ssm_d, w_glu, w_mem_kv, w_out, w_up, conv_w, conv_b, w_down):
    depth = w_in.shape[0]
    bp, _, d = x_prompt.shape
    n_mem = mem_prompt.shape[1]
    n_q = attn_sinks.shape[1]
    n_kv, hd = cache_attn_k.shape[-2:]
    assert hd == HEAD_DIM and n_kv == N_KV and cache_attn_k.shape[2] == WINDOW
    groups, p_state = ssm_a_re.shape[1:]
    q_w, kv_w = n_q * HEAD_DIM, n_kv * HEAD_DIM
    ssm_w = ssm_d.shape[1]
    mem_w = w_mem_kv.shape[2] // 2
    proj_w = q_w + 2 * kv_w + ssm_w + mem_w
    mem_hd = mem_w // MEM_HEADS

    xp, xs = x_prompt, x_sample
    outs = [[] for _ in range(12)]
    for l in range(depth):
        bias = _rel_bias(rel_bias_table)
        lp = dict(ssm_a_re=ssm_a_re[l], ssm_a_im=ssm_a_im[l], ssm_log_dt=ssm_log_dt[l],
                  ssm_b_re=ssm_b_re[l], ssm_b_im=ssm_b_im[l], ssm_c_re=ssm_c_re[l], ssm_c_im=ssm_c_im[l],
                  ssm_d=ssm_d[l], w_glu=w_glu[l])
        w_in_bf = w_in[l].astype(BF16)
        w_out_bf = w_out[l].astype(BF16)
        grp = n_q // n_kv
        w_q = w_in_bf[:, :q_w].reshape(d, n_kv, grp, HEAD_DIM).transpose(0, 2, 1, 3).reshape(d, q_w)
        w_oa = w_out_bf[:q_w].reshape(n_kv, grp, HEAD_DIM, d).transpose(1, 0, 2, 3).reshape(q_w, d)
        u_off = q_w + 2 * kv_w
        proj_outputs = [(BF16, [(w_q, q_w, 0)]),
                        (F32, _col_blocks(w_in_bf, q_w, 2 * kv_w)),
                        (F32, _col_blocks(w_in_bf, u_off, ssm_w)),
                        (BF16, _col_blocks(w_in_bf, u_off + ssm_w, mem_w)),
                        (BF16, [])]
        lw = dict(q_w=q_w, kv_w=kv_w, ssm_w=ssm_w, mem_w=mem_w, n_state=groups * p_state,
                  norm_pre_mix=norm_pre_mix[l], norm_post_mix=norm_post_mix[l],
                  norm_pre_ffn=norm_pre_ffn[l], norm_post_ffn=norm_post_ffn[l],
                  proj_outputs=proj_outputs, w_in_bf=w_in_bf, gate_col0=proj_w, w_oa=w_oa, w_out_bf=w_out_bf,
                  attn_sinks=attn_sinks[l], s5=_s5_params(lp),
                  d_ff=w_down.shape[1], w_up=w_up[l].astype(BF16), conv_w=conv_w[l], conv_b=conv_b[l],
                  w_down=w_down[l].astype(BF16))

        w_mem_bf = w_mem_kv[l].astype(BF16)
        mk_p, mv_p = _norm_proj(mem_prompt.reshape(bp * n_mem, d), norm_mem[l],
                                [(F32, [(w_mem_bf, mem_w, 0)]), (F32, [(w_mem_bf, mem_w, 1)])],
                                _tiles(bp, n_mem)['proj_rows'])
        mk_p = mk_p.reshape(bp, n_mem, mem_w)
        mv_p = mv_p.reshape(bp, n_mem, mem_w)
        xp, k_p, v_p, sr_p, si_p, c_p = _layer(xp, None, None, None, mk_p, mv_p, bias, lw)

        bs = xs.shape[0]
        xs, k_s, v_s, sr_s, si_s, c_s = _layer(
            xs, (cache_attn_k[l], cache_attn_v[l]), (state_ssm_re[l], state_ssm_im[l]), state_conv[l],
            cache_mem_k[l].reshape(bs, n_mem, mem_w), cache_mem_v[l].reshape(bs, n_mem, mem_w), bias, lw)

        vals = (k_p, v_p, sr_p.reshape(bp, groups, p_state), si_p.reshape(bp, groups, p_state), c_p,
                mk_p.reshape(bp, n_mem, MEM_HEADS, mem_hd), mv_p.reshape(bp, n_mem, MEM_HEADS, mem_hd),
                k_s, v_s, sr_s.reshape(bs, groups, p_state), si_s.reshape(bs, groups, p_state), c_s)
        for acc, val in zip(outs, vals):
            acc.append(val)
    return (xp, xs) + tuple(jnp.stack(o) for o in outs)
```

```python
import functools
import math

import jax
import jax.numpy as jnp
from jax import lax
from jax.experimental import pallas as pl
from jax.experimental.pallas import tpu as pltpu

F32 = jnp.float32
BF16 = jnp.bfloat16

EPS = 1e-6
NEG_INF = -1e30
CHUNK = 64
WINDOW = 128
HEAD_DIM = 64
N_KV = 4
MAX_DISTANCE = 128
SSM_GROUP_CH = 16
MEM_HEADS = 4
CONV_W = 3

MIB = 1024 * 1024
LANES = 128
MXU_WIDTH = 256
SUBLANES = 8
SCAN_CARRY_ELEMS = 4096


def _params(semantics, vmem_mib):
    return pltpu.CompilerParams(dimension_semantics=semantics, vmem_limit_bytes=vmem_mib * MIB)


def _rms(x, g):
    y = x * lax.rsqrt(jnp.mean(x * x, axis=-1, keepdims=True) + EPS)
    return y * g


def _dot(a, b):
    return jnp.dot(a, b, preferred_element_type=F32)


def _dot_t(a, b):
    return lax.dot_general(a, b, (((1,), (1,)), ((), ())), preferred_element_type=F32)


def _proj_kernel(x_ref, g_ref, *refs, blocks_per_out):
    n_w = sum(blocks_per_out)
    w_refs, o_refs = refs[:n_w], refs[n_w:]
    h = _rms(x_ref[...], g_ref[...]).astype(BF16)
    first = 0
    for o_ref, count in zip(o_refs, blocks_per_out):
        if count == 0:
            o_ref[...] = h
            continue
        parts = [_dot(h, w_ref[...]) for w_ref in w_refs[first:first + count]]
        r = parts[0] if count == 1 else jnp.concatenate(parts, axis=1)
        o_ref[...] = r.astype(o_ref.dtype)
        first += count


def _col_blocks(arr, start, width):
    blk = math.gcd(start, width) if start else width
    assert blk % LANES == 0
    return [(arr, blk, start // blk + k) for k in range(width // blk)]


def _norm_proj(x2d, g, outputs, tm):
    m, d = x2d.shape
    w_args, w_specs, out_specs, out_shapes = [], [], [], []
    for dt, blocks in outputs:
        for arr, width, blk in blocks:
            assert arr.shape[0] == d and arr.shape[1] % width == 0
            w_args.append(arr)
            w_specs.append(pl.BlockSpec((d, width), lambda i, blk=blk: (0, blk)))
        total = sum(width for _, width, _ in blocks) if blocks else d
        out_specs.append(pl.BlockSpec((tm, total), lambda i: (i, 0)))
        out_shapes.append(jax.ShapeDtypeStruct((m, total), dt))
    return pl.pallas_call(
        functools.partial(_proj_kernel, blocks_per_out=tuple(len(b) for _, b in outputs)),
        grid=(m // tm,),
        in_specs=[pl.BlockSpec((tm, d), lambda i: (i, 0)),
                  pl.BlockSpec((1, d), lambda i: (0, 0))] + w_specs,
        out_specs=out_specs,
        out_shape=out_shapes,
        compiler_params=_params(("parallel",), 48),
        name="proj",
    )(x2d, g.reshape(1, d), *w_args)


def _bias_kernel(idx_ref, table_ref, o_ref, *, n_buckets, n_heads):
    idx = idx_ref[...]
    for h in range(n_heads):
        acc = jnp.zeros(idx.shape, F32)
        for b in range(n_buckets):
            acc = jnp.where(idx == b, table_ref[b, h], acc)
        o_ref[h] = acc


def _t5_bucket(rel, n_buckets):
    half = n_buckets // 2
    max_exact = half // 2
    n = jnp.abs(rel)
    large = max_exact + (jnp.log(jnp.maximum(n, 1).astype(F32) / max_exact)
                         / math.log(MAX_DISTANCE / max_exact) * (half - max_exact)).astype(jnp.int32)
    large = jnp.minimum(large, half - 1)
    return jnp.where(rel > 0, half, 0) + jnp.where(n < max_exact, n, large)


def _rel_bias(table):
    n_buckets, n_heads = table.shape
    lk = WINDOW + CHUNK
    rel = jnp.arange(lk)[None, :] - WINDOW - jnp.arange(CHUNK)[:, None]
    idx = _t5_bucket(rel, n_buckets).astype(jnp.int32)
    return pl.pallas_call(
        functools.partial(_bias_kernel, n_buckets=n_buckets, n_heads=n_heads),
        in_specs=[pl.BlockSpec((CHUNK, lk), lambda: (0, 0)),
                  pl.BlockSpec(memory_space=pltpu.SMEM)],
        out_specs=pl.BlockSpec((n_heads, CHUNK, lk), lambda: (0, 0, 0)),
        out_shape=jax.ShapeDtypeStruct((n_heads, CHUNK, lk), F32),
        name="bias",
    )(idx, table.astype(F32))


def _attn_kernel(q_ref, k_ref, v_ref, hk_ref, hv_ref, bias_ref, o_ref, kf_scr, vf_scr, s_scr, *, n_masked, grp):
    nb, t, _ = q_ref.shape
    kv_w = k_ref.shape[2]
    n_chunks = t // CHUNK
    lk = WINDOW + CHUNK
    lkp = bias_ref.shape[2]
    pair_w = 2 * HEAD_DIM
    scale = HEAD_DIM ** -0.5
    low_half = lax.broadcasted_iota(jnp.int32, (1, pair_w), 1) < HEAD_DIM
    ones = jnp.ones((lkp, pair_w), BF16)
    kv_pad = jnp.zeros((lkp - lk, kv_w), BF16)

    kf_scr[:, 0:WINDOW, :] = hk_ref[...].astype(BF16)
    kf_scr[:, WINDOW:, :] = k_ref[...].astype(BF16)
    vf_scr[:, 0:WINDOW, :] = hv_ref[...].astype(BF16)
    vf_scr[:, WINDOW:, :] = v_ref[...].astype(BF16)

    n_total = nb * n_chunks

    def locate(n):
        bi = n // n_chunks
        c = n % n_chunks
        return bi, c, pl.multiple_of(c * CHUNK, CHUNK)

    def score_chunk(n, slot):
        bi, c, r0 = locate(n)
        q = q_ref[bi, pl.ds(r0, CHUNK), :] * scale
        k = jnp.concatenate([kf_scr[bi, pl.ds(r0, lk), :], kv_pad], axis=0)
        variant = jnp.minimum(c, n_masked) * N_KV
        for h in range(N_KV):
            j, e = divmod(h, 2)
            qp = jnp.concatenate([q[:, g * kv_w + j * pair_w:g * kv_w + (j + 1) * pair_w]
                                  for g in range(grp)], axis=0)
            in_head = low_half if e == 0 else jnp.logical_not(low_half)
            qh = jnp.where(in_head, qp, jnp.zeros_like(qp))
            s_scr[slot, h] = _dot_t(qh, k[:, j * pair_w:(j + 1) * pair_w]) + bias_ref[variant + h]

    def chunk(n, slot):
        score_chunk(jnp.minimum(n + 1, n_total - 1), 1 - slot)
        bi, c, r0 = locate(n)
        v = jnp.concatenate([vf_scr[bi, pl.ds(r0, lk), :], kv_pad], axis=0)
        scores = [s_scr[slot, h] for h in range(N_KV)]
        probs = [jnp.exp(s - jnp.max(s, axis=-1, keepdims=True)).astype(BF16) for s in scores]
        v_ext = [jnp.concatenate([v[:, j * pair_w:(j + 1) * pair_w], ones], axis=1) for j in range(N_KV // 2)]
        sums = [_dot(p, v_ext[h // 2]) for h, p in enumerate(probs)]
        outs = [r[:, :pair_w] / r[:, pair_w:] for r in sums]
        for j in range(N_KV // 2):
            o_pair = jnp.where(low_half, outs[2 * j], outs[2 * j + 1])
            for g in range(grp):
                o_ref[bi, pl.ds(r0, CHUNK), g * kv_w + j * pair_w:g * kv_w + (j + 1) * pair_w] = (
                    o_pair[g * CHUNK:(g + 1) * CHUNK].astype(BF16))

    def chunk_pair(p, carry):
        chunk(2 * p, 0)
        chunk(2 * p + 1, 1)
        return carry

    assert n_total % 2 == 0
    score_chunk(0, 0)
    lax.fori_loop(0, n_total // 2, chunk_pair, 0)


def _band_attention(q3, kv3, hist_k, hist_v, bias, sinks, nb, first_valid):
    b, t, q_w = q3.shape
    kv_w = hist_k.shape[2]
    grp = q_w // HEAD_DIM // N_KV
    lk = WINDOW + CHUNK
    lkp = 2 * LANES
    n_masked = first_valid // CHUNK
    sink_col = jnp.broadcast_to(sinks.astype(F32).reshape(N_KV, grp, 1, 1), (N_KV, grp, CHUNK, 1))
    ext = jnp.concatenate([bias.reshape(N_KV, grp * CHUNK, lk), sink_col.reshape(N_KV, grp * CHUNK, 1),
                           jnp.full((N_KV, grp * CHUNK, lkp - lk - 1), NEG_INF, F32)], axis=2)
    key = jnp.arange(lkp)
    variants = [jnp.where(key < (n_masked - i) * CHUNK, NEG_INF, ext) for i in range(n_masked + 1)]
    bias_ext = jnp.concatenate(variants, axis=0)
    return pl.pallas_call(
        functools.partial(_attn_kernel, n_masked=n_masked, grp=grp),
        grid=(b // nb,),
        in_specs=[pl.BlockSpec((nb, t, q_w), lambda i: (i, 0, 0)),
                  pl.BlockSpec((nb, t, kv_w), lambda i: (i, 0, 0)),
                  pl.BlockSpec((nb, t, kv_w), lambda i: (i, 0, 1)),
                  pl.BlockSpec((nb, WINDOW, kv_w), lambda i: (i, 0, 0)),
                  pl.BlockSpec((nb, WINDOW, kv_w), lambda i: (i, 0, 0)),
                  pl.BlockSpec(bias_ext.shape, lambda i: (0, 0, 0))],
        out_specs=pl.BlockSpec((nb, t, q_w), lambda i: (i, 0, 0)),
        out_shape=jax.ShapeDtypeStruct((b, t, q_w), BF16),
        scratch_shapes=[pltpu.VMEM((nb, WINDOW + t, kv_w), BF16), pltpu.VMEM((nb, WINDOW + t, kv_w), BF16),
                        pltpu.VMEM((2, N_KV, grp * CHUNK, lkp), F32)],
        compiler_params=_params(("parallel",), 40),
        name="attn",
    )(q3, kv3, kv3, hist_k, hist_v, bias_ext)


def _gelu(x):
    return jax.nn.gelu(x)


def _s5_kernel(u_ref, s0re_ref, s0im_ref, are_ref, aim_ref, wbre_ref, wbim_ref, wcre_ref, wcim_ref,
               d_ref, wglu_ref, o_ref, sre_out_ref, sim_out_ref,
               sre_scr, sim_scr, st_re, st_im, y_scr, *, batch, steps, lane_chunk):
    i = pl.program_id(0)

    @pl.when(i == 0)
    def _():
        st_re[...] = s0re_ref[...]
        st_im[...] = s0im_ref[...]

    u = u_ref[...]
    ub = u.astype(BF16)
    n_in_blk, in_blk, st_blk = wbre_ref.shape
    n_out_blk, k_blk, out_blk = wcre_ref.shape
    n_state = sre_scr.shape[1]

    def project_in(r):
        blk = ub[:, r * in_blk:(r + 1) * in_blk]
        sre_scr[:, r * st_blk:(r + 1) * st_blk] = _dot(blk, wbre_ref[r])
        sim_scr[:, r * st_blk:(r + 1) * st_blk] = _dot(blk, wbim_ref[r])

    def scan(lc):
        sl = slice(lc * lane_chunk, (lc + 1) * lane_chunk)
        ar = are_ref[:, sl]
        ai = aim_ref[:, sl]
        sr = st_re[:, sl]
        si = st_im[:, sl]
        for t in range(steps):
            rows = slice(t * batch, (t + 1) * batch)
            sr, si = (ar * sr - ai * si + sre_scr[rows, sl], ar * si + ai * sr + sim_scr[rows, sl])
            sre_scr[rows, sl] = sr
            sim_scr[rows, sl] = si
        st_re[:, sl] = sr
        st_im[:, sl] = si

    def project_out(kb):
        ksl = slice(kb * k_blk, (kb + 1) * k_blk)
        y_scr[:, kb * out_blk:(kb + 1) * out_blk] = (
            _dot(sre_scr[:, ksl].astype(BF16), wcre_ref[kb])
            + _dot(sim_scr[:, ksl].astype(BF16), wcim_ref[kb]))

    chunks_per_in = st_blk // lane_chunk
    done_out = 0
    project_in(0)
    for r in range(n_in_blk):
        if r + 1 < n_in_blk:
            project_in(r + 1)
        for lc in range(r * chunks_per_in, (r + 1) * chunks_per_in):
            scan(lc)
        while done_out < n_out_blk and (done_out + 1) * k_blk <= (r + 1) * st_blk:
            project_out(done_out)
            done_out += 1
    assert done_out == n_out_blk and n_in_blk * st_blk == n_state
    y = _gelu(y_scr[...] + d_ref[...] * u)
    o_ref[...] = (y * jax.nn.sigmoid(_dot(y.astype(BF16), wglu_ref[...]))).astype(BF16)

    @pl.when(i == pl.num_programs(0) - 1)
    def _():
        sre_out_ref[...] = st_re[...]
        sim_out_ref[...] = st_im[...]


def _s5_params(lp):
    lam = lax.complex(lp['ssm_a_re'].astype(F32), lp['ssm_a_im'].astype(F32))
    dt = jnp.exp(lp['ssm_log_dt'].astype(F32))[:, None]
    a_bar = jnp.exp(lam * dt)
    b_mat = lax.complex(lp['ssm_b_re'].astype(F32), lp['ssm_b_im'].astype(F32))
    b_bar = ((a_bar - 1.0) / lam)[..., None] * b_mat
    g, p, c = b_bar.shape
    gb_in = 256 // c
    gb_out = 128 // c

    def b_blocks(x):
        x = x.reshape(g // gb_in, gb_in, p, c).transpose(0, 1, 3, 2)
        x = jnp.einsum('rgcp,gh->rgchp', x, jnp.eye(gb_in, dtype=F32))
        return x.reshape(g // gb_in, gb_in * c, gb_in * p).astype(BF16)

    def c_blocks(x):
        x = x.reshape(g // gb_out, gb_out, c, p).transpose(0, 1, 3, 2)
        x = jnp.einsum('kgpc,gh->kgphc', x, jnp.eye(gb_out, dtype=F32))
        return x.reshape(g // gb_out, gb_out * p, gb_out * c).astype(BF16)

    return dict(a_re=jnp.real(a_bar).reshape(1, g * p), a_im=jnp.imag(a_bar).reshape(1, g * p),
                wb_re=b_blocks(jnp.real(b_bar)), wb_im=b_blocks(jnp.imag(b_bar)),
                wc_re=c_blocks(lp['ssm_c_re'].astype(F32)), wc_im=c_blocks(-lp['ssm_c_im'].astype(F32)),
                d=lp['ssm_d'].astype(F32).reshape(1, -1), w_glu=lp['w_glu'].astype(BF16))


def _s5_layer(u_tb, s0_re, s0_im, sp, steps):
    t, b, ssm_w = u_tb.shape
    n_state = s0_re.shape[1]
    rows = steps * b
    lane_chunk = SCAN_CARRY_ELEMS // b
    a_re = jnp.broadcast_to(sp['a_re'], (b, n_state))
    a_im = jnp.broadcast_to(sp['a_im'], (b, n_state))

    def full(x):
        nd = x.ndim
        return pl.BlockSpec(x.shape, lambda i: (0,) * nd)

    consts = [s0_re, s0_im, a_re, a_im, sp['wb_re'], sp['wb_im'], sp['wc_re'], sp['wc_im'], sp['d'], sp['w_glu']]
    return pl.pallas_call(
        functools.partial(_s5_kernel, batch=b, steps=steps, lane_chunk=lane_chunk),
        grid=(t // steps,),
        in_specs=[pl.BlockSpec((rows, ssm_w), lambda i: (i, 0))] + [full(x) for x in consts],
        out_specs=[pl.BlockSpec((rows, ssm_w), lambda i: (i, 0)),
                   pl.BlockSpec((b, n_state), lambda i: (0, 0)),
                   pl.BlockSpec((b, n_state), lambda i: (0, 0))],
        out_shape=[jax.ShapeDtypeStruct((t * b, ssm_w), BF16),
                   jax.ShapeDtypeStruct((b, n_state), F32),
                   jax.ShapeDtypeStruct((b, n_state), F32)],
        scratch_shapes=[pltpu.VMEM((rows, n_state), F32), pltpu.VMEM((rows, n_state), F32),
                        pltpu.VMEM((b, n_state), F32), pltpu.VMEM((b, n_state), F32),
                        pltpu.VMEM((rows, ssm_w), F32)],
        compiler_params=_params(("arbitrary",), 48),
        name="s5",
    )(u_tb.reshape(t * b, ssm_w), *consts)


def _memattn_kernel(q_ref, k_ref, v_ref, o_ref, *, head_dim):
    q = q_ref[...]
    k = k_ref[...].astype(BF16)
    v = v_ref[...].astype(BF16)
    scale = head_dim ** -0.5
    ones = jnp.ones((v.shape[0], head_dim), BF16)
    heads = [slice(h * head_dim, (h + 1) * head_dim) for h in range(MEM_HEADS)]
    scores = [_dot_t(q[:, sl], k[:, sl]) * scale for sl in heads]
    probs = [jnp.exp(s - jnp.max(s, axis=-1, keepdims=True)).astype(BF16) for s in scores]
    sums = [_dot(p, jnp.concatenate([v[:, sl], ones], axis=1)) for p, sl in zip(probs, heads)]
    for r, sl in zip(sums, heads):
        o_ref[:, sl] = (r[:, :head_dim] / r[:, head_dim:]).astype(BF16)


def _memory_attention(qm, mem_k, mem_v, b, t, tq):
    n_mem, mem_w = mem_k.shape[1:]
    nt = t // tq
    return pl.pallas_call(
        functools.partial(_memattn_kernel, head_dim=mem_w // MEM_HEADS),
        grid=(b, nt),
        in_specs=[pl.BlockSpec((tq, mem_w), lambda bi, ti: (bi * nt + ti, 0)),
                  pl.BlockSpec((None, n_mem, mem_w), lambda bi, ti: (bi, 0, 0)),
                  pl.BlockSpec((None, n_mem, mem_w), lambda bi, ti: (bi, 0, 0))],
        out_specs=pl.BlockSpec((tq, mem_w), lambda bi, ti: (bi * nt + ti, 0)),
        out_shape=jax.ShapeDtypeStruct((b * t, mem_w), BF16),
        compiler_params=_params(("parallel", "parallel"), 32),
        name="memattn",
    )(qm, mem_k, mem_v)


def _mix_kernel(x_ref, h_ref, oa_ref, os_ref, om_ref, wg0_ref, wg1_ref, wg2_ref,
                wa_ref, ws_ref, wm_ref, gpost_ref, o_ref, ss_scr, *, tn):
    j = pl.program_id(1)

    @pl.when(j == 0)
    def _():
        ss_scr[...] = jnp.zeros_like(ss_scr)

    h = h_ref[...]
    oa, os_, om = oa_ref[...], os_ref[...], om_ref[...]
    partial = jnp.zeros(ss_scr.shape, F32)
    for c in range(tn // MXU_WIDTH):
        sl = slice(c * MXU_WIDTH, (c + 1) * MXU_WIDTH)
        merged = (jax.nn.sigmoid(_dot(h, wg0_ref[:, sl])) * _dot(oa, wa_ref[:, sl])
                  + jax.nn.sigmoid(_dot(h, wg1_ref[:, sl])) * _dot(os_, ws_ref[:, sl])
                  + jax.nn.sigmoid(_dot(h, wg2_ref[:, sl])) * _dot(om, wm_ref[:, sl]))
        o_ref[:, pl.ds(pl.multiple_of(j * tn + c * MXU_WIDTH, MXU_WIDTH), MXU_WIDTH)] = merged
        sq = merged * merged
        partial += sum(sq[:, k * LANES:(k + 1) * LANES] for k in range(MXU_WIDTH // LANES))
    ss_scr[...] += partial

    @pl.when(j == pl.num_programs(1) - 1)
    def _():
        ss = jnp.sum(ss_scr[...], axis=-1, keepdims=True)
        inv = lax.rsqrt(ss / o_ref.shape[1] + EPS)
        o_ref[...] = x_ref[...] + (o_ref[...] * inv) * gpost_ref[...]


def _mix_residual(x2d, h, o_a, o_s, o_m, w_in_bf, gate_col0, w_oa, w_out_bf, g_post, tm, tn):
    m, d = x2d.shape
    nj = d // tn
    q_w, ssm_w, mem_w = o_a.shape[1], o_s.shape[1], o_m.shape[1]
    assert gate_col0 % tn == 0 and q_w % ssm_w == 0 and (q_w + ssm_w) % mem_w == 0
    gate_blk0 = gate_col0 // tn

    def rows(w):
        mode = dict(pipeline_mode=pl.Buffered(1)) if m == tm else {}
        return pl.BlockSpec((tm, w), lambda i, j: (i, 0), **mode)

    def gate_spec(br):
        return pl.BlockSpec((d, tn), lambda i, j: (0, gate_blk0 + br * nj + j))

    vec = pl.BlockSpec((1, d), lambda i, j: (0, 0))
    return pl.pallas_call(
        functools.partial(_mix_kernel, tn=tn),
        grid=(m // tm, nj),
        in_specs=[rows(d), rows(d), rows(q_w), rows(ssm_w), rows(mem_w),
                  gate_spec(0), gate_spec(1), gate_spec(2),
                  pl.BlockSpec((q_w, tn), lambda i, j: (0, j)),
                  pl.BlockSpec((ssm_w, tn), lambda i, j: (q_w // ssm_w, j)),
                  pl.BlockSpec((mem_w, tn), lambda i, j: ((q_w + ssm_w) // mem_w, j)), vec],
        out_specs=rows(d),
        out_shape=jax.ShapeDtypeStruct((m, d), F32),
        scratch_shapes=[pltpu.VMEM((tm, LANES), F32)],
        compiler_params=_params(("parallel", "arbitrary"), 56),
        name="mix",
    )(x2d, h, o_a, o_s, o_m, w_in_bf, w_in_bf, w_in_bf, w_oa, w_out_bf, w_out_bf,
      g_post.reshape(1, d))


def _ffn_kernel(x_ref, gpre_ref, *refs, n_sub, n_sub_valid):
    w_refs = refs[:3 * n_sub]
    cw_ref, cb_ref, gpost_ref, cprev_ref, o_ref, tail_ref, h_scr, a_scr, b_scr, carry_scr = refs[3 * n_sub:]
    j = pl.program_id(2)
    nb, tt, d = x_ref.shape
    tf = a_scr.shape[2]
    rows = nb * tt

    @pl.when(j == 0)
    def _():
        h_scr[...] = _rms(x_ref[...].reshape(rows, d), gpre_ref[...]).astype(BF16)
        o_ref[...] = jnp.zeros_like(o_ref)

    for s in range(n_sub):
        sub = j * n_sub + s
        step_cols = slice(s * tf, (s + 1) * tf)
        wua_ref, wub_ref, wd_ref = w_refs[3 * s:3 * s + 3]

        @pl.when(sub < n_sub_valid)
        def _(sub=sub, step_cols=step_cols, wua_ref=wua_ref, wub_ref=wub_ref, wd_ref=wd_ref):
            _ffn_sub_step(sub, step_cols, x_ref, wua_ref, wub_ref, cw_ref, cb_ref, wd_ref, cprev_ref,
                          o_ref, tail_ref, h_scr, a_scr, b_scr, carry_scr)

        @pl.when(sub >= n_sub_valid)
        def _(step_cols=step_cols):
            tail_ref[:, :, step_cols] = jnp.zeros((nb, SUBLANES, tf), F32)

    @pl.when(j == pl.num_programs(2) - 1)
    def _():
        f = _rms(o_ref[...].reshape(rows, d), gpost_ref[...]).reshape(nb, tt, d)
        o_ref[...] = x_ref[...] + f


def _ffn_sub_step(sub, step_cols, x_ref, wua_ref, wub_ref, cw_ref, cb_ref, wd_ref, cprev_ref,
                  o_ref, tail_ref, h_scr, a_scr, b_scr, carry_scr):
    i = pl.program_id(1)
    nb, tt, d = x_ref.shape
    tf = a_scr.shape[2]
    h = h_scr[...]
    a_scr[:, 0:SUBLANES, :] = jnp.where(i == 0, cprev_ref[:, :, step_cols], carry_scr[sub])
    n_slices = tf // MXU_WIDTH
    cols = [slice(s * MXU_WIDTH, (s + 1) * MXU_WIDTH) for s in range(n_slices)]
    for sl in cols:
        a_scr[:, SUBLANES:, sl] = _dot(h, wua_ref[:, sl]).reshape(nb, tt, MXU_WIDTH)
        b_scr[:, sl] = _dot(h, wub_ref[:, sl])
    cw = cw_ref[:, step_cols]
    cb = cb_ref[:, step_cols]
    for sl in cols:
        gated = []
        for bi in range(nb):
            a = a_scr[bi, SUBLANES:, sl]
            first = jnp.concatenate([a_scr[bi, 0:SUBLANES, sl], a[0:SUBLANES]], axis=0)
            prev1 = jnp.concatenate([first[SUBLANES - 1:2 * SUBLANES - 1],
                                     pltpu.roll(a, 1, 0)[SUBLANES:]], axis=0)
            prev2 = jnp.concatenate([first[SUBLANES - 2:2 * SUBLANES - 2],
                                     pltpu.roll(a, 2, 0)[SUBLANES:]], axis=0)
            conv = (prev2 * cw[0:1, sl] + prev1 * cw[1:2, sl] + a * cw[2:3, sl]) + cb[:, sl]
            gated.append(_gelu(conv) * b_scr[bi * tt:(bi + 1) * tt, sl])
        act = (gated[0] if nb == 1 else jnp.concatenate(gated, axis=0)).astype(BF16)
        o_ref[...] += _dot(act, wd_ref[sl, :]).reshape(nb, tt, d)
    tail = a_scr[:, tt:tt + SUBLANES, :]
    carry_scr[sub] = tail
    tail_ref[:, :, step_cols] = tail


def _conv_ffn(x3d, g_pre, w_up, conv_w, conv_b, w_down, g_post, conv_prev8, nb, tt, tf, n_sub):
    b, t, d = x3d.shape
    d_ff = w_down.shape[0]
    n_valid = d_ff // tf
    step_cols = n_sub * tf
    nj = -(-n_valid // n_sub)
    d_ff_pad = nj * step_cols
    nt = t // tt
    pad = ((0, 0), (0, d_ff_pad - d_ff))
    conv_w = jnp.pad(conv_w.astype(F32), pad)
    conv_b = jnp.pad(conv_b.astype(F32).reshape(1, d_ff), pad)
    conv_prev8 = jnp.pad(conv_prev8, ((0, 0),) + pad)

    def sub_block(s):
        return lambda j: jnp.minimum(j * n_sub + s, n_valid - 1)

    w_specs = []
    for s in range(n_sub):
        blk = sub_block(s)
        w_specs += [pl.BlockSpec((d, tf), lambda bi, i, j, blk=blk: (0, blk(j))),
                    pl.BlockSpec((d, tf), lambda bi, i, j, blk=blk: (0, n_valid + blk(j))),
                    pl.BlockSpec((tf, d), lambda bi, i, j, blk=blk: (blk(j), 0))]
    mode = dict(pipeline_mode=pl.Buffered(1)) if (b // nb) * nt == 1 else {}
    x2, tails = pl.pallas_call(
        functools.partial(_ffn_kernel, n_sub=n_sub, n_sub_valid=n_valid),
        grid=(b // nb, nt, nj),
        in_specs=[pl.BlockSpec((nb, tt, d), lambda bi, i, j: (bi, i, 0), **mode),
                  pl.BlockSpec((1, d), lambda bi, i, j: (0, 0))] + w_specs + [
                  pl.BlockSpec((CONV_W, step_cols), lambda bi, i, j: (0, j)),
                  pl.BlockSpec((1, step_cols), lambda bi, i, j: (0, j)),
                  pl.BlockSpec((1, d), lambda bi, i, j: (0, 0)),
                  pl.BlockSpec((nb, SUBLANES, step_cols), lambda bi, i, j: (bi, 0, j))],
        out_specs=[pl.BlockSpec((nb, tt, d), lambda bi, i, j: (bi, i, 0), **mode),
                   pl.BlockSpec((nb, None, SUBLANES, step_cols), lambda bi, i, j: (bi, i, 0, j))],
        out_shape=[jax.ShapeDtypeStruct((b, t, d), F32),
                   jax.ShapeDtypeStruct((b, nt, SUBLANES, d_ff_pad), F32)],
        scratch_shapes=[pltpu.VMEM((nb * tt, d), BF16),
                        pltpu.VMEM((nb, tt + SUBLANES, tf), F32), pltpu.VMEM((nb * tt, tf), F32),
                        pltpu.VMEM((nj * n_sub, nb, SUBLANES, tf), F32)],
        compiler_params=_params(("arbitrary", "arbitrary", "arbitrary"), 60),
        name="ffn",
    )(x3d, g_pre.reshape(1, d), *([w_up, w_up, w_down] * n_sub), conv_w, conv_b, g_post.reshape(1, d), conv_prev8)
    return x2, tails[..., :d_ff]


def _tiles(b, t):
    row_tile = 512
    tt = min(t, row_tile)
    return dict(
        proj_rows=512,
        attn_batch=max(1, min(b, 1024 // t)),
        s5_steps=min(t, 512 // b),
        memattn_rows=tt,
        mix_rows=512, mix_cols=512,
        ffn_batch=min(b, row_tile // tt), ffn_rows=tt,
        ffn_sub_steps=2)


FFN_COLS = 2 * MXU_WIDTH


def _layer(x, attn_past, s0, conv_prev, mem_k, mem_v, bias, lw):
    b, t, d = x.shape
    m = b * t
    q_w, kv_w, ssm_w, mem_w = lw['q_w'], lw['kv_w'], lw['ssm_w'], lw['mem_w']
    n_state = lw['n_state']
    d_ff = lw['d_ff']
    tiles = _tiles(b, t)
    x2d = x.reshape(m, d)

    q, kv, u, qm, h = _norm_proj(x2d, lw['norm_pre_mix'], lw['proj_outputs'], tiles['proj_rows'])
    kv3 = kv.reshape(b, t, 2 * kv_w)
    if attn_past is None:
        hist_k = jnp.zeros((b, WINDOW, kv_w), F32)
        hist_v = hist_k
        first_valid = WINDOW
    else:
        hist_k = attn_past[0].astype(F32).reshape(b, WINDOW, kv_w)
        hist_v = attn_past[1].astype(F32).reshape(b, WINDOW, kv_w)
        first_valid = 0
    o_a = _band_attention(q.reshape(b, t, q_w), kv3, hist_k, hist_v, bias, lw['attn_sinks'],
                          tiles['attn_batch'], first_valid).reshape(m, q_w)

    if s0 is None:
        s0_re = jnp.zeros((b, n_state), F32)
        s0_im = s0_re
    else:
        s0_re = s0[0].astype(F32).reshape(b, n_state)
        s0_im = s0[1].astype(F32).reshape(b, n_state)
    u_tb = u.reshape(b, t, ssm_w).transpose(1, 0, 2)
    o_s_tb, s_re, s_im = _s5_layer(u_tb, s0_re, s0_im, lw['s5'], tiles['s5_steps'])
    o_s = o_s_tb.reshape(t, b, ssm_w).transpose(1, 0, 2).reshape(m, ssm_w)

    o_m = _memory_attention(qm, mem_k, mem_v, b, t, tiles['memattn_rows'])

    x1 = _mix_residual(x2d, h, o_a, o_s, o_m, lw['w_in_bf'], lw['gate_col0'], lw['w_oa'],
                       lw['w_out_bf'], lw['norm_post_mix'], tiles['mix_rows'], tiles['mix_cols'])

    if conv_prev is None:
        conv_prev8 = jnp.zeros((b, SUBLANES, d_ff), F32)
    else:
        conv_prev8 = jnp.pad(conv_prev.astype(F32), ((0, 0), (SUBLANES - (CONV_W - 1), 0), (0, 0)))
    x2, tails = _conv_ffn(x1.reshape(b, t, d), lw['norm_pre_ffn'], lw['w_up'], lw['conv_w'], lw['conv_b'],
                          lw['w_down'], lw['norm_post_ffn'], conv_prev8,
                          tiles['ffn_batch'], tiles['ffn_rows'], FFN_COLS, tiles['ffn_sub_steps'])
    conv_new = tails[:, -1, SUBLANES - (CONV_W - 1):, :]

    n_kv = kv_w // HEAD_DIM
    k_new = jnp.concatenate([hist_k, kv3[:, :, :kv_w]], axis=1)[:, -WINDOW:].reshape(b, WINDOW, n_kv, HEAD_DIM)
    v_new = jnp.concatenate([hist_v, kv3[:, :, kv_w:]], axis=1)[:, -WINDOW:].reshape(b, WINDOW, n_kv, HEAD_DIM)
    return x2, k_new, v_new, s_re, s_im, conv_new


def kernel(x_prompt, x_sample, cache_attn_k, cache_attn_v, cache_mem_k, cache_mem_v, state_ssm_re, state_ssm_im, state_conv, mem_prompt, rel_bias_table, norm_pre_mix, norm_post_mix, norm_pre_ffn, norm_post_ffn, norm_mem, w_in, attn_sinks, ssm_a_re, ssm_a_im, ssm_log_dt, ssm_b_re, ssm_b_im, ssm_c_re, ssm_c_im, ssm_d, w_glu, w_mem_kv, w_out, w_up, conv_w, conv_b, w_down):
    depth = w_in.shape[0]
    bp, _, d = x_prompt.shape
    n_mem = mem_prompt.shape[1]
    n_q = attn_sinks.shape[1]
    n_kv, hd = cache_attn_k.shape[-2:]
    assert hd == HEAD_DIM and n_kv == N_KV and cache_attn_k.shape[2] == WINDOW
    groups, p_state = ssm_a_re.shape[1:]
    q_w, kv_w = n_q * HEAD_DIM, n_kv * HEAD_DIM
    ssm_w = ssm_d.shape[1]
    mem_w = w_mem_kv.shape[2] // 2
    proj_w = q_w + 2 * kv_w + ssm_w + mem_w
    mem_hd = mem_w // MEM_HEADS

    xp, xs = x_prompt, x_sample
    outs = [[] for _ in range(12)]
    for l in range(depth):
        bias = _rel_bias(rel_bias_table)
        lp = dict(ssm_a_re=ssm_a_re[l], ssm_a_im=ssm_a_im[l], ssm_log_dt=ssm_log_dt[l],
                  ssm_b_re=ssm_b_re[l], ssm_b_im=ssm_b_im[l], ssm_c_re=ssm_c_re[l], ssm_c_im=ssm_c_im[l],
                  ssm_d=ssm_d[l], w_glu=w_glu[l])
        w_in_bf = w_in[l].astype(BF16)
        w_out_bf = w_out[l].astype(BF16)
        grp = n_q // n_kv
        w_q = w_in_bf[:, :q_w].reshape(d, n_kv, grp, HEAD_DIM).transpose(0, 2, 1, 3).reshape(d, q_w)
        w_oa = w_out_bf[:q_w].reshape(n_kv, grp, HEAD_DIM, d).transpose(1, 0, 2, 3).reshape(q_w, d)
        u_off = q_w + 2 * kv_w
        proj_outputs = [(BF16, [(w_q, q_w, 0)]),
                        (F32, _col_blocks(w_in_bf, q_w, 2 * kv_w)),
                        (F32, _col_blocks(w_in_bf, u_off, ssm_w)),
                        (BF16, _col_blocks(w_in_bf, u_off + ssm_w, mem_w)),
                        (BF16, [])]
        lw = dict(q_w=q_w, kv_w=kv_w, ssm_w=ssm_w, mem_w=mem_w, n_state=groups * p_state,
                  norm_pre_mix=norm_pre_mix[l], norm_post_mix=norm_post_mix[l],
                  norm_pre_ffn=norm_pre_ffn[l], norm_post_ffn=norm_post_ffn[l],
                  proj_outputs=proj_outputs, w_in_bf=w_in_bf, gate_col0=proj_w, w_oa=w_oa, w_out_bf=w_out_bf,
                  attn_sinks=attn_sinks[l], s5=_s5_params(lp),
                  d_ff=w_down.shape[1], w_up=w_up[l].astype(BF16), conv_w=conv_w[l], conv_b=conv_b[l],
                  w_down=w_down[l].astype(BF16))

        w_mem_bf = w_mem_kv[l].astype(BF16)
        mk_p, mv_p = _norm_proj(mem_prompt.reshape(bp * n_mem, d), norm_mem[l],
                                [(F32, [(w_mem_bf, mem_w, 0)]), (F32, [(w_mem_bf, mem_w, 1)])],
                                _tiles(bp, n_mem)['proj_rows'])
        mk_p = mk_p.reshape(bp, n_mem, mem_w)
        mv_p = mv_p.reshape(bp, n_mem, mem_w)
        xp, k_p, v_p, sr_p, si_p, c_p = _layer(xp, None, None, None, mk_p, mv_p, bias, lw)

        bs = xs.shape[0]
        xs, k_s, v_s, sr_s, si_s, c_s = _layer(
            xs, (cache_attn_k[l], cache_attn_v[l]), (state_ssm_re[l], state_ssm_im[l]), state_conv[l],
            cache_mem_k[l].reshape(bs, n_mem, mem_w), cache_mem_v[l].reshape(bs, n_mem, mem_w), bias, lw)

        vals = (k_p, v_p, sr_p.reshape(bp, groups, p_state), si_p.reshape(bp, groups, p_state), c_p,
                mk_p.reshape(bp, n_mem, MEM_HEADS, mem_hd), mv_p.reshape(bp, n_mem, MEM_HEADS, mem_hd),
                k_s, v_s, sr_s.reshape(bs, groups, p_state), si_s.reshape(bs, groups, p_state), c_s)
        for acc, val in zip(outs, vals):
            acc.append(val)
    return (xp, xs) + tuple(jnp.stack(o) for o in outs)
```

```python
import functools
import math

import jax
import jax.numpy as jnp
from jax import lax
from jax.experimental import pallas as pl
from jax.experimental.pallas import tpu as pltpu

F32 = jnp.float32
BF16 = jnp.bfloat16

EPS = 1e-6
NEG_INF = -1e30
CHUNK = 64
WINDOW = 128
HEAD_DIM = 64
N_KV = 4
MAX_DISTANCE = 128
SSM_GROUP_CH = 16
MEM_HEADS = 4
CONV_W = 3

MIB = 1024 * 1024
LANES = 128
MXU_WIDTH = 256
SUBLANES = 8
SCAN_CARRY_ELEMS = 4096


def _params(semantics, vmem_mib):
    return pltpu.CompilerParams(dimension_semantics=semantics, vmem_limit_bytes=vmem_mib * MIB)


def _rms(x, g):
    y = x * lax.rsqrt(jnp.mean(x * x, axis=-1, keepdims=True) + EPS)
    return y * g


def _dot(a, b):
    return jnp.dot(a, b, preferred_element_type=F32)


def _dot_t(a, b):
    return lax.dot_general(a, b, (((1,), (1,)), ((), ())), preferred_element_type=F32)


def _proj_kernel(x_ref, g_ref, *refs, blocks_per_out):
    n_w = sum(blocks_per_out)
    w_refs, o_refs = refs[:n_w], refs[n_w:]
    h = _rms(x_ref[...], g_ref[...]).astype(BF16)
    first = 0
    for o_ref, count in zip(o_refs, blocks_per_out):
        if count == 0:
            o_ref[...] = h
            continue
        parts = [_dot(h, w_ref[...]) for w_ref in w_refs[first:first + count]]
        r = parts[0] if count == 1 else jnp.concatenate(parts, axis=1)
        o_ref[...] = r.astype(o_ref.dtype)
        first += count


def _col_blocks(arr, start, width):
    blk = math.gcd(start, width) if start else width
    assert blk % LANES == 0
    return [(arr, blk, start // blk + k) for k in range(width // blk)]


def _norm_proj(x2d, g, outputs, tm):
    m, d = x2d.shape
    w_args, w_specs, out_specs, out_shapes = [], [], [], []
    for dt, blocks in outputs:
        for arr, width, blk in blocks:
            assert arr.shape[0] == d and arr.shape[1] % width == 0
            w_args.append(arr)
            w_specs.append(pl.BlockSpec((d, width), lambda i, blk=blk: (0, blk)))
        total = sum(width for _, width, _ in blocks) if blocks else d
        out_specs.append(pl.BlockSpec((tm, total), lambda i: (i, 0)))
        out_shapes.append(jax.ShapeDtypeStruct((m, total), dt))
    return pl.pallas_call(
        functools.partial(_proj_kernel, blocks_per_out=tuple(len(b) for _, b in outputs)),
        grid=(m // tm,),
        in_specs=[pl.BlockSpec((tm, d), lambda i: (i, 0)),
                  pl.BlockSpec((1, d), lambda i: (0, 0))] + w_specs,
        out_specs=out_specs,
        out_shape=out_shapes,
        compiler_params=_params(("parallel",), 48),
        name="proj",
    )(x2d, g.reshape(1, d), *w_args)


def _bias_kernel(idx_ref, table_ref, o_ref, *, n_buckets, n_heads):
    idx = idx_ref[...]
    for h in range(n_heads):
        acc = jnp.zeros(idx.shape, F32)
        for b in range(n_buckets):
            acc = jnp.where(idx == b, table_ref[b, h], acc)
        o_ref[h] = acc


def _t5_bucket(rel, n_buckets):
    half = n_buckets // 2
    max_exact = half // 2
    n = jnp.abs(rel)
    large = max_exact + (jnp.log(jnp.maximum(n, 1).astype(F32) / max_exact)
                         / math.log(MAX_DISTANCE / max_exact) * (half - max_exact)).astype(jnp.int32)
    large = jnp.minimum(large, half - 1)
    return jnp.where(rel > 0, half, 0) + jnp.where(n < max_exact, n, large)


def _rel_bias(table):
    n_buckets, n_heads = table.shape
    lk = WINDOW + CHUNK
    rel = jnp.arange(lk)[None, :] - WINDOW - jnp.arange(CHUNK)[:, None]
    idx = _t5_bucket(rel, n_buckets).astype(jnp.int32)
    return pl.pallas_call(
        functools.partial(_bias_kernel, n_buckets=n_buckets, n_heads=n_heads),
        in_specs=[pl.BlockSpec((CHUNK, lk), lambda: (0, 0)),
                  pl.BlockSpec(memory_space=pltpu.SMEM)],
        out_specs=pl.BlockSpec((n_heads, CHUNK, lk), lambda: (0, 0, 0)),
        out_shape=jax.ShapeDtypeStruct((n_heads, CHUNK, lk), F32),
        name="bias",
    )(idx, table.astype(F32))


def _attn_kernel(q_ref, k_ref, v_ref, hk_ref, hv_ref, bias_ref, o_ref, kf_scr, vf_scr, s_scr, *, n_masked, grp):
    nb, t, _ = q_ref.shape
    kv_w = k_ref.shape[2]
    n_chunks = t // CHUNK
    lk = WINDOW + CHUNK
    lkp = bias_ref.shape[2]
    pair_w = 2 * HEAD_DIM
    scale = HEAD_DIM ** -0.5
    low_half = lax.broadcasted_iota(jnp.int32, (1, pair_w), 1) < HEAD_DIM
    ones = jnp.ones((lkp, pair_w), BF16)
    kv_pad = jnp.zeros((lkp - lk, kv_w), BF16)

    kf_scr[:, 0:WINDOW, :] = hk_ref[...].astype(BF16)
    kf_scr[:, WINDOW:, :] = k_ref[...].astype(BF16)
    vf_scr[:, 0:WINDOW, :] = hv_ref[...].astype(BF16)
    vf_scr[:, WINDOW:, :] = v_ref[...].astype(BF16)

    n_total = nb * n_chunks

    def locate(n):
        bi = n // n_chunks
        c = n % n_chunks
        return bi, c, pl.multiple_of(c * CHUNK, CHUNK)

    def score_chunk(n, slot):
        bi, c, r0 = locate(n)
        q = q_ref[bi, pl.ds(r0, CHUNK), :] * scale
        k = jnp.concatenate([kf_scr[bi, pl.ds(r0, lk), :], kv_pad], axis=0)
        variant = jnp.minimum(c, n_masked) * N_KV
        for h in range(N_KV):
            j, e = divmod(h, 2)
            qp = jnp.concatenate([q[:, g * kv_w + j * pair_w:g * kv_w + (j + 1) * pair_w]
                                  for g in range(grp)], axis=0)
            in_head = low_half if e == 0 else jnp.logical_not(low_half)
            qh = jnp.where(in_head, qp, jnp.zeros_like(qp))
            s_scr[slot, h] = _dot_t(qh, k[:, j * pair_w:(j + 1) * pair_w]) + bias_ref[variant + h]

    def chunk(n, slot):
        score_chunk(jnp.minimum(n + 1, n_total - 1), 1 - slot)
        bi, c, r0 = locate(n)
        v = jnp.concatenate([vf_scr[bi, pl.ds(r0, lk), :], kv_pad], axis=0)
        scores = [s_scr[slot, h] for h in range(N_KV)]
        probs = [jnp.exp(s - jnp.max(s, axis=-1, keepdims=True)).astype(BF16) for s in scores]
        v_ext = [jnp.concatenate([v[:, j * pair_w:(j + 1) * pair_w], ones], axis=1) for j in range(N_KV // 2)]
        sums = [_dot(p, v_ext[h // 2]) for h, p in enumerate(probs)]
        outs = [r[:, :pair_w] / r[:, pair_w:] for r in sums]
        for j in range(N_KV // 2):
            o_pair = jnp.where(low_half, outs[2 * j], outs[2 * j + 1])
            for g in range(grp):
                o_ref[bi, pl.ds(r0, CHUNK), g * kv_w + j * pair_w:g * kv_w + (j + 1) * pair_w] = (
                    o_pair[g * CHUNK:(g + 1) * CHUNK].astype(BF16))

    def chunk_pair(p, carry):
        chunk(2 * p, 0)
        chunk(2 * p + 1, 1)
        return carry

    assert n_total % 2 == 0
    score_chunk(0, 0)
    lax.fori_loop(0, n_total // 2, chunk_pair, 0, unroll=2)


def _band_attention(q3, kv3, hist_k, hist_v, bias, sinks, nb, first_valid):
    b, t, q_w = q3.shape
    kv_w = hist_k.shape[2]
    grp = q_w // HEAD_DIM // N_KV
    lk = WINDOW + CHUNK
    lkp = 2 * LANES
    n_masked = first_valid // CHUNK
    sink_col = jnp.broadcast_to(sinks.astype(F32).reshape(N_KV, grp, 1, 1), (N_KV, grp, CHUNK, 1))
    ext = jnp.concatenate([bias.reshape(N_KV, grp * CHUNK, lk), sink_col.reshape(N_KV, grp * CHUNK, 1),
                           jnp.full((N_KV, grp * CHUNK, lkp - lk - 1), NEG_INF, F32)], axis=2)
    key = jnp.arange(lkp)
    variants = [jnp.where(key < (n_masked - i) * CHUNK, NEG_INF, ext) for i in range(n_masked + 1)]
    bias_ext = jnp.concatenate(variants, axis=0)
    return pl.pallas_call(
        functools.partial(_attn_kernel, n_masked=n_masked, grp=grp),
        grid=(b // nb,),
        in_specs=[pl.BlockSpec((nb, t, q_w), lambda i: (i, 0, 0)),
                  pl.BlockSpec((nb, t, kv_w), lambda i: (i, 0, 0)),
                  pl.BlockSpec((nb, t, kv_w), lambda i: (i, 0, 1)),
                  pl.BlockSpec((nb, WINDOW, kv_w), lambda i: (i, 0, 0)),
                  pl.BlockSpec((nb, WINDOW, kv_w), lambda i: (i, 0, 0)),
                  pl.BlockSpec(bias_ext.shape, lambda i: (0, 0, 0))],
        out_specs=pl.BlockSpec((nb, t, q_w), lambda i: (i, 0, 0)),
        out_shape=jax.ShapeDtypeStruct((b, t, q_w), BF16),
        scratch_shapes=[pltpu.VMEM((nb, WINDOW + t, kv_w), BF16), pltpu.VMEM((nb, WINDOW + t, kv_w), BF16),
                        pltpu.VMEM((2, N_KV, grp * CHUNK, lkp), F32)],
        compiler_params=_params(("parallel",), 40),
        name="attn",
    )(q3, kv3, kv3, hist_k, hist_v, bias_ext)


def _gelu(x):
    return jax.nn.gelu(x)


def _s5_kernel(u_ref, s0re_ref, s0im_ref, are_ref, aim_ref, wbre_ref, wbim_ref, wcre_ref, wcim_ref,
               d_ref, wglu_ref, o_ref, sre_out_ref, sim_out_ref,
               sre_scr, sim_scr, st_re, st_im, y_scr, *, batch, steps, lane_chunk):
    i = pl.program_id(0)

    @pl.when(i == 0)
    def _():
        st_re[...] = s0re_ref[...]
        st_im[...] = s0im_ref[...]

    u = u_ref[...]
    ub = u.astype(BF16)
    n_in_blk, in_blk, st_blk = wbre_ref.shape
    n_out_blk, k_blk, out_blk = wcre_ref.shape
    n_state = sre_scr.shape[1]

    def project_in(r):
        blk = ub[:, r * in_blk:(r + 1) * in_blk]
        sre_scr[:, r * st_blk:(r + 1) * st_blk] = _dot(blk, wbre_ref[r])
        sim_scr[:, r * st_blk:(r + 1) * st_blk] = _dot(blk, wbim_ref[r])

    def scan(lc):
        sl = slice(lc * lane_chunk, (lc + 1) * lane_chunk)
        ar = are_ref[:, sl]
        ai = aim_ref[:, sl]
        sr = st_re[:, sl]
        si = st_im[:, sl]
        for t in range(steps):
            rows = slice(t * batch, (t + 1) * batch)
            sr, si = (ar * sr - ai * si + sre_scr[rows, sl], ar * si + ai * sr + sim_scr[rows, sl])
            sre_scr[rows, sl] = sr
            sim_scr[rows, sl] = si
        st_re[:, sl] = sr
        st_im[:, sl] = si

    def project_out(kb):
        ksl = slice(kb * k_blk, (kb + 1) * k_blk)
        y_scr[:, kb * out_blk:(kb + 1) * out_blk] = (
            _dot(sre_scr[:, ksl].astype(BF16), wcre_ref[kb])
            + _dot(sim_scr[:, ksl].astype(BF16), wcim_ref[kb]))

    chunks_per_in = st_blk // lane_chunk
    done_out = 0
    project_in(0)
    for r in range(n_in_blk):
        if r + 1 < n_in_blk:
            project_in(r + 1)
        for lc in range(r * chunks_per_in, (r + 1) * chunks_per_in):
            scan(lc)
        while done_out < n_out_blk and (done_out + 1) * k_blk <= (r + 1) * st_blk:
            project_out(done_out)
            done_out += 1
    assert done_out == n_out_blk and n_in_blk * st_blk == n_state
    y = _gelu(y_scr[...] + d_ref[...] * u)
    o_ref[...] = (y * jax.nn.sigmoid(_dot(y.astype(BF16), wglu_ref[...]))).astype(BF16)

    @pl.when(i == pl.num_programs(0) - 1)
    def _():
        sre_out_ref[...] = st_re[...]
        sim_out_ref[...] = st_im[...]


def _s5_params(lp):
    lam = lax.complex(lp['ssm_a_re'].astype(F32), lp['ssm_a_im'].astype(F32))
    dt = jnp.exp(lp['ssm_log_dt'].astype(F32))[:, None]
    a_bar = jnp.exp(lam * dt)
    b_mat = lax.complex(lp['ssm_b_re'].astype(F32), lp['ssm_b_im'].astype(F32))
    b_bar = ((a_bar - 1.0) / lam)[..., None] * b_mat
    g, p, c = b_bar.shape
    gb_in = 256 // c
    gb_out = 128 // c

    def b_blocks(x):
        x = x.reshape(g // gb_in, gb_in, p, c).transpose(0, 1, 3, 2)
        x = jnp.einsum('rgcp,gh->rgchp', x, jnp.eye(gb_in, dtype=F32))
        return x.reshape(g // gb_in, gb_in * c, gb_in * p).astype(BF16)

    def c_blocks(x):
        x = x.reshape(g // gb_out, gb_out, c, p).transpose(0, 1, 3, 2)
        x = jnp.einsum('kgpc,gh->kgphc', x, jnp.eye(gb_out, dtype=F32))
        return x.reshape(g // gb_out, gb_out * p, gb_out * c).astype(BF16)

    return dict(a_re=jnp.real(a_bar).reshape(1, g * p), a_im=jnp.imag(a_bar).reshape(1, g * p),
                wb_re=b_blocks(jnp.real(b_bar)), wb_im=b_blocks(jnp.imag(b_bar)),
                wc_re=c_blocks(lp['ssm_c_re'].astype(F32)), wc_im=c_blocks(-lp['ssm_c_im'].astype(F32)),
                d=lp['ssm_d'].astype(F32).reshape(1, -1), w_glu=lp['w_glu'].astype(BF16))


def _s5_layer(u_tb, s0_re, s0_im, sp, steps):
    t, b, ssm_w = u_tb.shape
    n_state = s0_re.shape[1]
    rows = steps * b
    lane_chunk = SCAN_CARRY_ELEMS // b
    a_re = jnp.broadcast_to(sp['a_re'], (b, n_state))
    a_im = jnp.broadcast_to(sp['a_im'], (b, n_state))

    def full(x):
        nd = x.ndim
        return pl.BlockSpec(x.shape, lambda i: (0,) * nd)

    consts = [s0_re, s0_im, a_re, a_im, sp['wb_re'], sp['wb_im'], sp['wc_re'], sp['wc_im'], sp['d'], sp['w_glu']]
    return pl.pallas_call(
        functools.partial(_s5_kernel, batch=b, steps=steps, lane_chunk=lane_chunk),
        grid=(t // steps,),
        in_specs=[pl.BlockSpec((rows, ssm_w), lambda i: (i, 0))] + [full(x) for x in consts],
        out_specs=[pl.BlockSpec((rows, ssm_w), lambda i: (i, 0)),
                   pl.BlockSpec((b, n_state), lambda i: (0, 0)),
                   pl.BlockSpec((b, n_state), lambda i: (0, 0))],
        out_shape=[jax.ShapeDtypeStruct((t * b, ssm_w), BF16),
                   jax.ShapeDtypeStruct((b, n_state), F32),
                   jax.ShapeDtypeStruct((b, n_state), F32)],
        scratch_shapes=[pltpu.VMEM((rows, n_state), F32), pltpu.VMEM((rows, n_state), F32),
                        pltpu.VMEM((b, n_state), F32), pltpu.VMEM((b, n_state), F32),
                        pltpu.VMEM((rows, ssm_w), F32)],
        compiler_params=_params(("arbitrary",), 48),
        name="s5",
    )(u_tb.reshape(t * b, ssm_w), *consts)


def _memattn_kernel(q_ref, k_ref, v_ref, o_ref, *, head_dim):
    q = q_ref[...]
    k = k_ref[...].astype(BF16)
    v = v_ref[...].astype(BF16)
    scale = head_dim ** -0.5
    ones = jnp.ones((v.shape[0], head_dim), BF16)
    heads = [slice(h * head_dim, (h + 1) * head_dim) for h in range(MEM_HEADS)]
    scores = [_dot_t(q[:, sl], k[:, sl]) * scale for sl in heads]
    probs = [jnp.exp(s - jnp.max(s, axis=-1, keepdims=True)).astype(BF16) for s in scores]
    sums = [_dot(p, jnp.concatenate([v[:, sl], ones], axis=1)) for p, sl in zip(probs, heads)]
    for r, sl in zip(sums, heads):
        o_ref[:, sl] = (r[:, :head_dim] / r[:, head_dim:]).astype(BF16)


def _memory_attention(qm, mem_k, mem_v, b, t, tq):
    n_mem, mem_w = mem_k.shape[1:]
    nt = t // tq
    return pl.pallas_call(
        functools.partial(_memattn_kernel, head_dim=mem_w // MEM_HEADS),
        grid=(b, nt),
        in_specs=[pl.BlockSpec((tq, mem_w), lambda bi, ti: (bi * nt + ti, 0)),
                  pl.BlockSpec((None, n_mem, mem_w), lambda bi, ti: (bi, 0, 0)),
                  pl.BlockSpec((None, n_mem, mem_w), lambda bi, ti: (bi, 0, 0))],
        out_specs=pl.BlockSpec((tq, mem_w), lambda bi, ti: (bi * nt + ti, 0)),
        out_shape=jax.ShapeDtypeStruct((b * t, mem_w), BF16),
        compiler_params=_params(("parallel", "parallel"), 32),
        name="memattn",
    )(qm, mem_k, mem_v)


def _mix_kernel(x_ref, h_ref, oa_ref, os_ref, om_ref, wg0_ref, wg1_ref, wg2_ref,
                wa_ref, ws_ref, wm_ref, gpost_ref, o_ref, ss_scr, *, tn):
    j = pl.program_id(1)

    @pl.when(j == 0)
    def _():
        ss_scr[...] = jnp.zeros_like(ss_scr)

    h = h_ref[...]
    oa, os_, om = oa_ref[...], os_ref[...], om_ref[...]
    partial = jnp.zeros(ss_scr.shape, F32)
    for c in range(tn // MXU_WIDTH):
        sl = slice(c * MXU_WIDTH, (c + 1) * MXU_WIDTH)
        merged = (jax.nn.sigmoid(_dot(h, wg0_ref[:, sl])) * _dot(oa, wa_ref[:, sl])
                  + jax.nn.sigmoid(_dot(h, wg1_ref[:, sl])) * _dot(os_, ws_ref[:, sl])
                  + jax.nn.sigmoid(_dot(h, wg2_ref[:, sl])) * _dot(om, wm_ref[:, sl]))
        o_ref[:, pl.ds(pl.multiple_of(j * tn + c * MXU_WIDTH, MXU_WIDTH), MXU_WIDTH)] = merged
        sq = merged * merged
        partial += sum(sq[:, k * LANES:(k + 1) * LANES] for k in range(MXU_WIDTH // LANES))
    ss_scr[...] += partial

    @pl.when(j == pl.num_programs(1) - 1)
    def _():
        ss = jnp.sum(ss_scr[...], axis=-1, keepdims=True)
        inv = lax.rsqrt(ss / o_ref.shape[1] + EPS)
        o_ref[...] = x_ref[...] + (o_ref[...] * inv) * gpost_ref[...]


def _mix_residual(x2d, h, o_a, o_s, o_m, w_in_bf, gate_col0, w_oa, w_out_bf, g_post, tm, tn):
    m, d = x2d.shape
    nj = d // tn
    q_w, ssm_w, mem_w = o_a.shape[1], o_s.shape[1], o_m.shape[1]
    assert gate_col0 % tn == 0 and q_w % ssm_w == 0 and (q_w + ssm_w) % mem_w == 0
    gate_blk0 = gate_col0 // tn

    def rows(w):
        mode = dict(pipeline_mode=pl.Buffered(1)) if m == tm else {}
        return pl.BlockSpec((tm, w), lambda i, j: (i, 0), **mode)

    def gate_spec(br):
        return pl.BlockSpec((d, tn), lambda i, j: (0, gate_blk0 + br * nj + j))

    vec = pl.BlockSpec((1, d), lambda i, j: (0, 0))
    return pl.pallas_call(
        functools.partial(_mix_kernel, tn=tn),
        grid=(m // tm, nj),
        in_specs=[rows(d), rows(d), rows(q_w), rows(ssm_w), rows(mem_w),
                  gate_spec(0), gate_spec(1), gate_spec(2),
                  pl.BlockSpec((q_w, tn), lambda i, j: (0, j)),
                  pl.BlockSpec((ssm_w, tn), lambda i, j: (q_w // ssm_w, j)),
                  pl.BlockSpec((mem_w, tn), lambda i, j: ((q_w + ssm_w) // mem_w, j)), vec],
        out_specs=rows(d),
        out_shape=jax.ShapeDtypeStruct((m, d), F32),
        scratch_shapes=[pltpu.VMEM((tm, LANES), F32)],
        compiler_params=_params(("parallel", "arbitrary"), 56),
        name="mix",
    )(x2d, h, o_a, o_s, o_m, w_in_bf, w_in_bf, w_in_bf, w_oa, w_out_bf, w_out_bf,
      g_post.reshape(1, d))


def _ffn_kernel(x_ref, gpre_ref, *refs, n_sub, n_sub_valid):
    w_refs = refs[:3 * n_sub]
    cw_ref, cb_ref, gpost_ref, cprev_ref, o_ref, tail_ref, h_scr, a_scr, b_scr, carry_scr = refs[3 * n_sub:]
    j = pl.program_id(2)
    nb, tt, d = x_ref.shape
    tf = a_scr.shape[2]
    rows = nb * tt

    @pl.when(j == 0)
    def _():
        h_scr[...] = _rms(x_ref[...].reshape(rows, d), gpre_ref[...]).astype(BF16)
        o_ref[...] = jnp.zeros_like(o_ref)

    for s in range(n_sub):
        sub = j * n_sub + s
        step_cols = slice(s * tf, (s + 1) * tf)
        wua_ref, wub_ref, wd_ref = w_refs[3 * s:3 * s + 3]

        @pl.when(sub < n_sub_valid)
        def _(sub=sub, step_cols=step_cols, wua_ref=wua_ref, wub_ref=wub_ref, wd_ref=wd_ref):
            _ffn_sub_step(sub, step_cols, x_ref, wua_ref, wub_ref, cw_ref, cb_ref, wd_ref, cprev_ref,
                          o_ref, tail_ref, h_scr, a_scr, b_scr, carry_scr)

        @pl.when(sub >= n_sub_valid)
        def _(step_cols=step_cols):
            tail_ref[:, :, step_cols] = jnp.zeros((nb, SUBLANES, tf), F32)

    @pl.when(j == pl.num_programs(2) - 1)
    def _():
        f = _rms(o_ref[...].reshape(rows, d), gpost_ref[...]).reshape(nb, tt, d)
        o_ref[...] = x_ref[...] + f


def _ffn_sub_step(sub, step_cols, x_ref, wua_ref, wub_ref, cw_ref, cb_ref, wd_ref, cprev_ref,
                  o_ref, tail_ref, h_scr, a_scr, b_scr, carry_scr):
    i = pl.program_id(1)
    nb, tt, d = x_ref.shape
    tf = a_scr.shape[2]
    h = h_scr[...]
    a_scr[:, 0:SUBLANES, :] = jnp.where(i == 0, cprev_ref[:, :, step_cols], carry_scr[sub])
    n_slices = tf // MXU_WIDTH
    cols = [slice(s * MXU_WIDTH, (s + 1) * MXU_WIDTH) for s in range(n_slices)]
    for sl in cols:
        a_scr[:, SUBLANES:, sl] = _dot(h, wua_ref[:, sl]).reshape(nb, tt, MXU_WIDTH)
        b_scr[:, sl] = _dot(h, wub_ref[:, sl])
    cw = cw_ref[:, step_cols]
    cb = cb_ref[:, step_cols]
    for sl in cols:
        gated = []
        for bi in range(nb):
            a = a_scr[bi, SUBLANES:, sl]
            first = jnp.concatenate([a_scr[bi, 0:SUBLANES, sl], a[0:SUBLANES]], axis=0)
            prev1 = jnp.concatenate([first[SUBLANES - 1:2 * SUBLANES - 1],
                                     pltpu.roll(a, 1, 0)[SUBLANES:]], axis=0)
            prev2 = jnp.concatenate([first[SUBLANES - 2:2 * SUBLANES - 2],
                                     pltpu.roll(a, 2, 0)[SUBLANES:]], axis=0)
            conv = (prev2 * cw[0:1, sl] + prev1 * cw[1:2, sl] + a * cw[2:3, sl]) + cb[:, sl]
            gated.append(_gelu(conv) * b_scr[bi * tt:(bi + 1) * tt, sl])
        act = (gated[0] if nb == 1 else jnp.concatenate(gated, axis=0)).astype(BF16)
        o_ref[...] += _dot(act, wd_ref[sl, :]).reshape(nb, tt, d)
    tail = a_scr[:, tt:tt + SUBLANES, :]
    carry_scr[sub] = tail
    tail_ref[:, :, step_cols] = tail


def _conv_ffn(x3d, g_pre, w_up, conv_w, conv_b, w_down, g_post, conv_prev8, nb, tt, tf, n_sub):
    b, t, d = x3d.shape
    d_ff = w_down.shape[0]
    n_valid = d_ff // tf
    step_cols = n_sub * tf
    nj = -(-n_valid // n_sub)
    d_ff_pad = nj * step_cols
    nt = t // tt
    pad = ((0, 0), (0, d_ff_pad - d_ff))
    conv_w = jnp.pad(conv_w.astype(F32), pad)
    conv_b = jnp.pad(conv_b.astype(F32).reshape(1, d_ff), pad)
    conv_prev8 = jnp.pad(conv_prev8, ((0, 0),) + pad)

    def sub_block(s):
        return lambda j: jnp.minimum(j * n_sub + s, n_valid - 1)

    w_specs = []
    for s in range(n_sub):
        blk = sub_block(s)
        w_specs += [pl.BlockSpec((d, tf), lambda bi, i, j, blk=blk: (0, blk(j))),
                    pl.BlockSpec((d, tf), lambda bi, i, j, blk=blk: (0, n_valid + blk(j))),
                    pl.BlockSpec((tf, d), lambda bi, i, j, blk=blk: (blk(j), 0))]
    mode = dict(pipeline_mode=pl.Buffered(1)) if (b // nb) * nt == 1 else {}
    x2, tails = pl.pallas_call(
        functools.partial(_ffn_kernel, n_sub=n_sub, n_sub_valid=n_valid),
        grid=(b // nb, nt, nj),
        in_specs=[pl.BlockSpec((nb, tt, d), lambda bi, i, j: (bi, i, 0), **mode),
                  pl.BlockSpec((1, d), lambda bi, i, j: (0, 0))] + w_specs + [
                  pl.BlockSpec((CONV_W, step_cols), lambda bi, i, j: (0, j)),
                  pl.BlockSpec((1, step_cols), lambda bi, i, j: (0, j)),
                  pl.BlockSpec((1, d), lambda bi, i, j: (0, 0)),
                  pl.BlockSpec((nb, SUBLANES, step_cols), lambda bi, i, j: (bi, 0, j))],
        out_specs=[pl.BlockSpec((nb, tt, d), lambda bi, i, j: (bi, i, 0), **mode),
                   pl.BlockSpec((nb, None, SUBLANES, step_cols), lambda bi, i, j: (bi, i, 0, j))],
        out_shape=[jax.ShapeDtypeStruct((b, t, d), F32),
                   jax.ShapeDtypeStruct((b, nt, SUBLANES, d_ff_pad), F32)],
        scratch_shapes=[pltpu.VMEM((nb * tt, d), BF16),
                        pltpu.VMEM((nb, tt + SUBLANES, tf), F32), pltpu.VMEM((nb * tt, tf), F32),
                        pltpu.VMEM((nj * n_sub, nb, SUBLANES, tf), F32)],
        compiler_params=_params(("arbitrary", "arbitrary", "arbitrary"), 60),
        name="ffn",
    )(x3d, g_pre.reshape(1, d), *([w_up, w_up, w_down] * n_sub), conv_w, conv_b, g_post.reshape(1, d), conv_prev8)
    return x2, tails[..., :d_ff]


def _tiles(b, t):
    row_tile = 512
    tt = min(t, row_tile)
    return dict(
        proj_rows=512,
        attn_batch=max(1, min(b, 1024 // t)),
        s5_steps=min(t, 1024 // b),
        memattn_rows=tt,
        mix_rows=512, mix_cols=512,
        ffn_batch=min(b, row_tile // tt), ffn_rows=tt,
        ffn_sub_steps=2)


FFN_COLS = 2 * MXU_WIDTH


def _layer(x, attn_past, s0, conv_prev, mem_k, mem_v, bias, lw):
    b, t, d = x.shape
    m = b * t
    q_w, kv_w, ssm_w, mem_w = lw['q_w'], lw['kv_w'], lw['ssm_w'], lw['mem_w']
    n_state = lw['n_state']
    d_ff = lw['d_ff']
    tiles = _tiles(b, t)
    x2d = x.reshape(m, d)

    q, kv, u, qm, h = _norm_proj(x2d, lw['norm_pre_mix'], lw['proj_outputs'], tiles['proj_rows'])
    kv3 = kv.reshape(b, t, 2 * kv_w)
    if attn_past is None:
        hist_k = jnp.zeros((b, WINDOW, kv_w), F32)
        hist_v = hist_k
        first_valid = WINDOW
    else:
        hist_k = attn_past[0].astype(F32).reshape(b, WINDOW, kv_w)
        hist_v = attn_past[1].astype(F32).reshape(b, WINDOW, kv_w)
        first_valid = 0
    o_a = _band_attention(q.reshape(b, t, q_w), kv3, hist_k, hist_v, bias, lw['attn_sinks'],
                          tiles['attn_batch'], first_valid).reshape(m, q_w)

    if s0 is None:
        s0_re = jnp.zeros((b, n_state), F32)
        s0_im = s0_re
    else:
        s0_re = s0[0].astype(F32).reshape(b, n_state)
        s0_im = s0[1].astype(F32).reshape(b, n_state)
    u_tb = u.reshape(b, t, ssm_w).transpose(1, 0, 2)
    o_s_tb, s_re, s_im = _s5_layer(u_tb, s0_re, s0_im, lw['s5'], tiles['s5_steps'])
    o_s = o_s_tb.reshape(t, b, ssm_w).transpose(1, 0, 2).reshape(m, ssm_w)

    o_m = _memory_attention(qm, mem_k, mem_v, b, t, tiles['memattn_rows'])

    x1 = _mix_residual(x2d, h, o_a, o_s, o_m, lw['w_in_bf'], lw['gate_col0'], lw['w_oa'],
                       lw['w_out_bf'], lw['norm_post_mix'], tiles['mix_rows'], tiles['mix_cols'])

    if conv_prev is None:
        conv_prev8 = jnp.zeros((b, SUBLANES, d_ff), F32)
    else:
        conv_prev8 = jnp.pad(conv_prev.astype(F32), ((0, 0), (SUBLANES - (CONV_W - 1), 0), (0, 0)))
    x2, tails = _conv_ffn(x1.reshape(b, t, d), lw['norm_pre_ffn'], lw['w_up'], lw['conv_w'], lw['conv_b'],
                          lw['w_down'], lw['norm_post_ffn'], conv_prev8,
                          tiles['ffn_batch'], tiles['ffn_rows'], FFN_COLS, tiles['ffn_sub_steps'])
    conv_new = tails[:, -1, SUBLANES - (CONV_W - 1):, :]

    n_kv = kv_w // HEAD_DIM
    k_new = jnp.concatenate([hist_k, kv3[:, :, :kv_w]], axis=1)[:, -WINDOW:].reshape(b, WINDOW, n_kv, HEAD_DIM)
    v_new = jnp.concatenate([hist_v, kv3[:, :, kv_w:]], axis=1)[:, -WINDOW:].reshape(b, WINDOW, n_kv, HEAD_DIM)
    return x2, k_new, v_new, s_re, s_im, conv_new


def kernel(x_prompt, x_sample, cache_attn_k, cache_attn_v, cache_mem_k, cache_mem_v, state_ssm_re, state_ssm_im, state_conv, mem_prompt, rel_bias_table, norm_pre_mix, norm_post_mix, norm_pre_ffn, norm_post_ffn, norm_mem, w_in, attn_sinks, ssm_a_re, ssm_a_im, ssm_log_dt, ssm_b_re, ssm_b_im, ssm_c_re, ssm_c_im, ssm_d, w_glu, w_mem_kv, w_out, w_up, conv_w, conv_b, w_down):
    depth = w_in.shape[0]
    bp, _, d = x_prompt.shape
    n_mem = mem_prompt.shape[1]
    n_q = attn_sinks.shape[1]
    n_kv, hd = cache_attn_k.shape[-2:]
    assert hd == HEAD_DIM and n_kv == N_KV and cache_attn_k.shape[2] == WINDOW
    groups, p_state = ssm_a_re.shape[1:]
    q_w, kv_w = n_q * HEAD_DIM, n_kv * HEAD_DIM
    ssm_w = ssm_d.shape[1]
    mem_w = w_mem_kv.shape[2] // 2
    proj_w = q_w + 2 * kv_w + ssm_w + mem_w
    mem_hd = mem_w // MEM_HEADS

    xp, xs = x_prompt, x_sample
    outs = [[] for _ in range(12)]
    for l in range(depth):
        bias = _rel_bias(rel_bias_table)
        lp = dict(ssm_a_re=ssm_a_re[l], ssm_a_im=ssm_a_im[l], ssm_log_dt=ssm_log_dt[l],
                  ssm_b_re=ssm_b_re[l], ssm_b_im=ssm_b_im[l], ssm_c_re=ssm_c_re[l], ssm_c_im=ssm_c_im[l],
                  ssm_d=ssm_d[l], w_glu=w_glu[l])
        w_in_bf = w_in[l].astype(BF16)
        w_out_bf = w_out[l].astype(BF16)
        grp = n_q // n_kv
        w_q = w_in_bf[:, :q_w].reshape(d, n_kv, grp, HEAD_DIM).transpose(0, 2, 1, 3).reshape(d, q_w)
        w_oa = w_out_bf[:q_w].reshape(n_kv, grp, HEAD_DIM, d).transpose(1, 0, 2, 3).reshape(q_w, d)
        u_off = q_w + 2 * kv_w
        proj_outputs = [(BF16, [(w_q, q_w, 0)]),
                        (F32, _col_blocks(w_in_bf, q_w, 2 * kv_w)),
                        (F32, _col_blocks(w_in_bf, u_off, ssm_w)),
                        (BF16, _col_blocks(w_in_bf, u_off + ssm_w, mem_w)),
                        (BF16, [])]
        lw = dict(q_w=q_w, kv_w=kv_w, ssm_w=ssm_w, mem_w=mem_w, n_state=groups * p_state,
                  norm_pre_mix=norm_pre_mix[l], norm_post_mix=norm_post_mix[l],
                  norm_pre_ffn=norm_pre_ffn[l], norm_post_ffn=norm_post_ffn[l],
                  proj_outputs=proj_outputs, w_in_bf=w_in_bf, gate_col0=proj_w, w_oa=w_oa, w_out_bf=w_out_bf,
                  attn_sinks=attn_sinks[l], s5=_s5_params(lp),
                  d_ff=w_down.shape[1], w_up=w_up[l].astype(BF16), conv_w=conv_w[l], conv_b=conv_b[l],
                  w_down=w_down[l].astype(BF16))

        w_mem_bf = w_mem_kv[l].astype(BF16)
        mk_p, mv_p = _norm_proj(mem_prompt.reshape(bp * n_mem, d), norm_mem[l],
                                [(F32, [(w_mem_bf, mem_w, 0)]), (F32, [(w_mem_bf, mem_w, 1)])],
                                _tiles(bp, n_mem)['proj_rows'])
        mk_p = mk_p.reshape(bp, n_mem, mem_w)
        mv_p = mv_p.reshape(bp, n_mem, mem_w)
        xp, k_p, v_p, sr_p, si_p, c_p = _layer(xp, None, None, None, mk_p, mv_p, bias, lw)

        bs = xs.shape[0]
        xs, k_s, v_s, sr_s, si_s, c_s = _layer(
            xs, (cache_attn_k[l], cache_attn_v[l]), (state_ssm_re[l], state_ssm_im[l]), state_conv[l],
            cache_mem_k[l].reshape(bs, n_mem, mem_w), cache_mem_v[l].reshape(bs, n_mem, mem_w), bias, lw)

        vals = (k_p, v_p, sr_p.reshape(bp, groups, p_state), si_p.reshape(bp, groups, p_state), c_p,
                mk_p.reshape(bp, n_mem, MEM_HEADS, mem_hd), mv_p.reshape(bp, n_mem, MEM_HEADS, mem_hd),
                k_s, v_s, sr_s.reshape(bs, groups, p_state), si_s.reshape(bs, groups, p_state), c_s)
        for acc, val in zip(outs, vals):
            acc.append(val)
    return (xp, xs) + tuple(jnp.stack(o) for o in outs)
```

```python
import functools
import math

import jax
import jax.numpy as jnp
from jax import lax
from jax.experimental import pallas as pl
from jax.experimental.pallas import tpu as pltpu

F32 = jnp.float32
BF16 = jnp.bfloat16

EPS = 1e-6
NEG_INF = -1e30
CHUNK = 64
WINDOW = 128
HEAD_DIM = 64
N_KV = 4
MAX_DISTANCE = 128
MEM_HEADS = 4
CONV_W = 3

MIB = 1024 * 1024
LANES = 128
MXU_WIDTH = 256
SUBLANES = 8
SCAN_CARRY_ELEMS = 4096


VMEM_LIMIT_MIB = dict(proj=48, attn=40, s5=48, memattn=32, mix=56, ffn=60)


def _params(semantics, call):
    return pltpu.CompilerParams(dimension_semantics=semantics, vmem_limit_bytes=VMEM_LIMIT_MIB[call] * MIB)


def _rms(x, g):
    y = x * lax.rsqrt(jnp.mean(x * x, axis=-1, keepdims=True) + EPS)
    return y * g


def _dot(a, b):
    return jnp.dot(a, b, preferred_element_type=F32)


def _dot_t(a, b):
    return lax.dot_general(a, b, (((1,), (1,)), ((), ())), preferred_element_type=F32)


def _proj_kernel(x_ref, g_ref, *refs, blocks_per_out):
    n_w = sum(blocks_per_out)
    w_refs, o_refs = refs[:n_w], refs[n_w:]
    h = _rms(x_ref[...], g_ref[...]).astype(BF16)
    first = 0
    for o_ref, count in zip(o_refs, blocks_per_out):
        if count == 0:
            o_ref[...] = h
            continue
        parts = [_dot(h, w_ref[...]) for w_ref in w_refs[first:first + count]]
        r = parts[0] if count == 1 else jnp.concatenate(parts, axis=1)
        o_ref[...] = r.astype(o_ref.dtype)
        first += count


def _col_blocks(arr, start, width):
    blk = math.gcd(start, width) if start else width
    assert blk % LANES == 0
    return [(arr, blk, start // blk + k) for k in range(width // blk)]


def _norm_proj(x2d, g, outputs, tm):
    m, d = x2d.shape
    w_args, w_specs, out_specs, out_shapes = [], [], [], []
    for dt, blocks in outputs:
        for arr, width, blk in blocks:
            assert arr.shape[0] == d and arr.shape[1] % width == 0
            w_args.append(arr)
            w_specs.append(pl.BlockSpec((d, width), lambda i, blk=blk: (0, blk)))
        total = sum(width for _, width, _ in blocks) if blocks else d
        out_specs.append(pl.BlockSpec((tm, total), lambda i: (i, 0)))
        out_shapes.append(jax.ShapeDtypeStruct((m, total), dt))
    return pl.pallas_call(
        functools.partial(_proj_kernel, blocks_per_out=tuple(len(b) for _, b in outputs)),
        grid=(m // tm,),
        in_specs=[pl.BlockSpec((tm, d), lambda i: (i, 0)),
                  pl.BlockSpec((1, d), lambda i: (0, 0))] + w_specs,
        out_specs=out_specs,
        out_shape=out_shapes,
        compiler_params=_params(("parallel",), "proj"),
        name="proj",
    )(x2d, g.reshape(1, d), *w_args)


def _bias_kernel(idx_ref, table_ref, o_ref, *, n_buckets, n_heads):
    idx = idx_ref[...]
    for h in range(n_heads):
        acc = jnp.zeros(idx.shape, F32)
        for b in range(n_buckets):
            acc = jnp.where(idx == b, table_ref[b, h], acc)
        o_ref[h] = acc


def _t5_bucket(rel, n_buckets):
    half = n_buckets // 2
    max_exact = half // 2
    n = jnp.abs(rel)
    large = max_exact + (jnp.log(jnp.maximum(n, 1).astype(F32) / max_exact)
                         / math.log(MAX_DISTANCE / max_exact) * (half - max_exact)).astype(jnp.int32)
    large = jnp.minimum(large, half - 1)
    return jnp.where(rel > 0, half, 0) + jnp.where(n < max_exact, n, large)


def _rel_bias(table):
    n_buckets, n_heads = table.shape
    lk = WINDOW + CHUNK
    rel = jnp.arange(lk)[None, :] - WINDOW - jnp.arange(CHUNK)[:, None]
    idx = _t5_bucket(rel, n_buckets).astype(jnp.int32)
    return pl.pallas_call(
        functools.partial(_bias_kernel, n_buckets=n_buckets, n_heads=n_heads),
        in_specs=[pl.BlockSpec((CHUNK, lk), lambda: (0, 0)),
                  pl.BlockSpec(memory_space=pltpu.SMEM)],
        out_specs=pl.BlockSpec((n_heads, CHUNK, lk), lambda: (0, 0, 0)),
        out_shape=jax.ShapeDtypeStruct((n_heads, CHUNK, lk), F32),
        name="bias",
    )(idx, table.astype(F32))


def _attn_kernel(q_ref, k_ref, v_ref, hk_ref, hv_ref, bias_ref, o_ref, kf_scr, vf_scr, s_scr, *, n_masked, grp):
    nb, t, _ = q_ref.shape
    kv_w = k_ref.shape[2]
    n_chunks = t // CHUNK
    lk = WINDOW + CHUNK
    lkp = bias_ref.shape[2]
    pair_w = 2 * HEAD_DIM
    scale = HEAD_DIM ** -0.5
    low_half = lax.broadcasted_iota(jnp.int32, (1, pair_w), 1) < HEAD_DIM
    ones = jnp.ones((lkp, pair_w), BF16)
    kv_pad = jnp.zeros((lkp - lk, kv_w), BF16)

    kf_scr[:, 0:WINDOW, :] = hk_ref[...].astype(BF16)
    kf_scr[:, WINDOW:, :] = k_ref[...].astype(BF16)
    vf_scr[:, 0:WINDOW, :] = hv_ref[...].astype(BF16)
    vf_scr[:, WINDOW:, :] = v_ref[...].astype(BF16)

    n_total = nb * n_chunks

    def locate(n):
        bi = n // n_chunks
        c = n % n_chunks
        return bi, c, pl.multiple_of(c * CHUNK, CHUNK)

    def score_chunk(n, slot):
        bi, c, r0 = locate(n)
        q = q_ref[bi, pl.ds(r0, CHUNK), :] * scale
        k = jnp.concatenate([kf_scr[bi, pl.ds(r0, lk), :], kv_pad], axis=0)
        variant = jnp.minimum(c, n_masked) * N_KV
        for h in range(N_KV):
            j, e = divmod(h, 2)
            qp = jnp.concatenate([q[:, g * kv_w + j * pair_w:g * kv_w + (j + 1) * pair_w]
                                  for g in range(grp)], axis=0)
            in_head = low_half if e == 0 else jnp.logical_not(low_half)
            qh = jnp.where(in_head, qp, jnp.zeros_like(qp))
            s_scr[slot, h] = _dot_t(qh, k[:, j * pair_w:(j + 1) * pair_w]) + bias_ref[variant + h]

    def chunk(n, slot):
        score_chunk(jnp.minimum(n + 1, n_total - 1), 1 - slot)
        bi, c, r0 = locate(n)
        v = jnp.concatenate([vf_scr[bi, pl.ds(r0, lk), :], kv_pad], axis=0)
        scores = [s_scr[slot, h] for h in range(N_KV)]
        probs = [jnp.exp(s - jnp.max(s, axis=-1, keepdims=True)).astype(BF16) for s in scores]
        v_ext = [jnp.concatenate([v[:, j * pair_w:(j + 1) * pair_w], ones], axis=1) for j in range(N_KV // 2)]
        sums = [_dot(p, v_ext[h // 2]) for h, p in enumerate(probs)]
        outs = [r[:, :pair_w] / r[:, pair_w:] for r in sums]
        for j in range(N_KV // 2):
            o_pair = jnp.where(low_half, outs[2 * j], outs[2 * j + 1])
            for g in range(grp):
                o_ref[bi, pl.ds(r0, CHUNK), g * kv_w + j * pair_w:g * kv_w + (j + 1) * pair_w] = (
                    o_pair[g * CHUNK:(g + 1) * CHUNK].astype(BF16))

    def chunk_pair(p, carry):
        chunk(2 * p, 0)
        chunk(2 * p + 1, 1)
        return carry

    assert n_total % 2 == 0
    score_chunk(0, 0)
    lax.fori_loop(0, n_total // 2, chunk_pair, 0, unroll=4)


def _band_attention(q3, kv3, hist_k, hist_v, bias, sinks, nb, first_valid):
    b, t, q_w = q3.shape
    kv_w = hist_k.shape[2]
    grp = q_w // HEAD_DIM // N_KV
    lk = WINDOW + CHUNK
    lkp = 2 * LANES
    n_masked = first_valid // CHUNK
    sink_col = jnp.broadcast_to(sinks.astype(F32).reshape(N_KV, grp, 1, 1), (N_KV, grp, CHUNK, 1))
    ext = jnp.concatenate([bias.reshape(N_KV, grp * CHUNK, lk), sink_col.reshape(N_KV, grp * CHUNK, 1),
                           jnp.full((N_KV, grp * CHUNK, lkp - lk - 1), NEG_INF, F32)], axis=2)
    key = jnp.arange(lkp)
    variants = [jnp.where(key < (n_masked - i) * CHUNK, NEG_INF, ext) for i in range(n_masked + 1)]
    bias_ext = jnp.concatenate(variants, axis=0)
    return pl.pallas_call(
        functools.partial(_attn_kernel, n_masked=n_masked, grp=grp),
        grid=(b // nb,),
        in_specs=[pl.BlockSpec((nb, t, q_w), lambda i: (i, 0, 0)),
                  pl.BlockSpec((nb, t, kv_w), lambda i: (i, 0, 0)),
                  pl.BlockSpec((nb, t, kv_w), lambda i: (i, 0, 1)),
                  pl.BlockSpec((nb, WINDOW, kv_w), lambda i: (i, 0, 0)),
                  pl.BlockSpec((nb, WINDOW, kv_w), lambda i: (i, 0, 0)),
                  pl.BlockSpec(bias_ext.shape, lambda i: (0, 0, 0))],
        out_specs=pl.BlockSpec((nb, t, q_w), lambda i: (i, 0, 0)),
        out_shape=jax.ShapeDtypeStruct((b, t, q_w), BF16),
        scratch_shapes=[pltpu.VMEM((nb, WINDOW + t, kv_w), BF16), pltpu.VMEM((nb, WINDOW + t, kv_w), BF16),
                        pltpu.VMEM((2, N_KV, grp * CHUNK, lkp), F32)],
        compiler_params=_params(("parallel",), "attn"),
        name="attn",
    )(q3, kv3, kv3, hist_k, hist_v, bias_ext)


def _gelu(x):
    return jax.nn.gelu(x)


def _s5_kernel(u_ref, s0re_ref, s0im_ref, are_ref, aim_ref, wbre_ref, wbim_ref, wcre_ref, wcim_ref,
               d_ref, wglu_ref, o_ref, sre_out_ref, sim_out_ref,
               sre_scr, sim_scr, st_re, st_im, y_scr, *, batch, steps, lane_chunk):
    i = pl.program_id(0)

    @pl.when(i == 0)
    def _():
        st_re[...] = s0re_ref[...]
        st_im[...] = s0im_ref[...]

    u = u_ref[...]
    ub = u.astype(BF16)
    n_in_blk, in_blk, st_blk = wbre_ref.shape
    n_out_blk, k_blk, out_blk = wcre_ref.shape
    n_state = sre_scr.shape[1]

    def project_in(r):
        blk = ub[:, r * in_blk:(r + 1) * in_blk]
        sre_scr[:, r * st_blk:(r + 1) * st_blk] = _dot(blk, wbre_ref[r])
        sim_scr[:, r * st_blk:(r + 1) * st_blk] = _dot(blk, wbim_ref[r])

    def scan(lc):
        sl = slice(lc * lane_chunk, (lc + 1) * lane_chunk)
        ar = are_ref[:, sl]
        ai = aim_ref[:, sl]
        sr = st_re[:, sl]
        si = st_im[:, sl]
        for t in range(steps):
            rows = slice(t * batch, (t + 1) * batch)
            sr, si = (ar * sr - ai * si + sre_scr[rows, sl], ar * si + ai * sr + sim_scr[rows, sl])
            sre_scr[rows, sl] = sr
            sim_scr[rows, sl] = si
        st_re[:, sl] = sr
        st_im[:, sl] = si

    def project_out(kb):
        ksl = slice(kb * k_blk, (kb + 1) * k_blk)
        y_scr[:, kb * out_blk:(kb + 1) * out_blk] = (
            _dot(sre_scr[:, ksl].astype(BF16), wcre_ref[kb])
            + _dot(sim_scr[:, ksl].astype(BF16), wcim_ref[kb]))

    chunks_per_in = st_blk // lane_chunk
    done_out = 0
    project_in(0)
    for r in range(n_in_blk):
        if r + 1 < n_in_blk:
            project_in(r + 1)
        for lc in range(r * chunks_per_in, (r + 1) * chunks_per_in):
            scan(lc)
        while done_out < n_out_blk and (done_out + 1) * k_blk <= (r + 1) * st_blk:
            project_out(done_out)
            done_out += 1
    assert done_out == n_out_blk and n_in_blk * st_blk == n_state
    y = _gelu(y_scr[...] + d_ref[...] * u)
    o_ref[...] = (y * jax.nn.sigmoid(_dot(y.astype(BF16), wglu_ref[...]))).astype(BF16)

    @pl.when(i == pl.num_programs(0) - 1)
    def _():
        sre_out_ref[...] = st_re[...]
        sim_out_ref[...] = st_im[...]


def _s5_params(lp):
    lam = lax.complex(lp['ssm_a_re'].astype(F32), lp['ssm_a_im'].astype(F32))
    dt = jnp.exp(lp['ssm_log_dt'].astype(F32))[:, None]
    a_bar = jnp.exp(lam * dt)
    b_mat = lax.complex(lp['ssm_b_re'].astype(F32), lp['ssm_b_im'].astype(F32))
    b_bar = ((a_bar - 1.0) / lam)[..., None] * b_mat
    g, p, c = b_bar.shape
    gb_in = MXU_WIDTH // c
    gb_out = LANES // c

    def b_blocks(x):
        x = x.reshape(g // gb_in, gb_in, p, c).transpose(0, 1, 3, 2)
        x = jnp.einsum('rgcp,gh->rgchp', x, jnp.eye(gb_in, dtype=F32))
        return x.reshape(g // gb_in, gb_in * c, gb_in * p).astype(BF16)

    def c_blocks(x):
        x = x.reshape(g // gb_out, gb_out, c, p).transpose(0, 1, 3, 2)
        x = jnp.einsum('kgpc,gh->kgphc', x, jnp.eye(gb_out, dtype=F32))
        return x.reshape(g // gb_out, gb_out * p, gb_out * c).astype(BF16)

    return dict(a_re=jnp.real(a_bar).reshape(1, g * p), a_im=jnp.imag(a_bar).reshape(1, g * p),
                wb_re=b_blocks(jnp.real(b_bar)), wb_im=b_blocks(jnp.imag(b_bar)),
                wc_re=c_blocks(lp['ssm_c_re'].astype(F32)), wc_im=c_blocks(-lp['ssm_c_im'].astype(F32)),
                d=lp['ssm_d'].astype(F32).reshape(1, -1), w_glu=lp['w_glu'].astype(BF16))


def _s5_layer(u_tb, s0_re, s0_im, sp, steps):
    t, b, ssm_w = u_tb.shape
    n_state = s0_re.shape[1]
    rows = steps * b
    lane_chunk = SCAN_CARRY_ELEMS // b
    a_re = jnp.broadcast_to(sp['a_re'], (b, n_state))
    a_im = jnp.broadcast_to(sp['a_im'], (b, n_state))

    def full(x):
        nd = x.ndim
        return pl.BlockSpec(x.shape, lambda i: (0,) * nd)

    consts = [s0_re, s0_im, a_re, a_im, sp['wb_re'], sp['wb_im'], sp['wc_re'], sp['wc_im'], sp['d'], sp['w_glu']]
    return pl.pallas_call(
        functools.partial(_s5_kernel, batch=b, steps=steps, lane_chunk=lane_chunk),
        grid=(t // steps,),
        in_specs=[pl.BlockSpec((rows, ssm_w), lambda i: (i, 0))] + [full(x) for x in consts],
        out_specs=[pl.BlockSpec((rows, ssm_w), lambda i: (i, 0)),
                   pl.BlockSpec((b, n_state), lambda i: (0, 0)),
                   pl.BlockSpec((b, n_state), lambda i: (0, 0))],
        out_shape=[jax.ShapeDtypeStruct((t * b, ssm_w), BF16),
                   jax.ShapeDtypeStruct((b, n_state), F32),
                   jax.ShapeDtypeStruct((b, n_state), F32)],
        scratch_shapes=[pltpu.VMEM((rows, n_state), F32), pltpu.VMEM((rows, n_state), F32),
                        pltpu.VMEM((b, n_state), F32), pltpu.VMEM((b, n_state), F32),
                        pltpu.VMEM((rows, ssm_w), F32)],
        compiler_params=_params(("arbitrary",), "s5"),
        name="s5",
    )(u_tb.reshape(t * b, ssm_w), *consts)


def _memattn_kernel(q_ref, k_ref, v_ref, o_ref, *, head_dim):
    q = q_ref[...]
    k = k_ref[...].astype(BF16)
    v = v_ref[...].astype(BF16)
    scale = head_dim ** -0.5
    ones = jnp.ones((v.shape[0], head_dim), BF16)
    heads = [slice(h * head_dim, (h + 1) * head_dim) for h in range(MEM_HEADS)]
    scores = [_dot_t(q[:, sl], k[:, sl]) * scale for sl in heads]
    probs = [jnp.exp(s - jnp.max(s, axis=-1, keepdims=True)).astype(BF16) for s in scores]
    sums = [_dot(p, jnp.concatenate([v[:, sl], ones], axis=1)) for p, sl in zip(probs, heads)]
    for r, sl in zip(sums, heads):
        o_ref[:, sl] = (r[:, :head_dim] / r[:, head_dim:]).astype(BF16)


def _memory_attention(qm, mem_k, mem_v, b, t, tq):
    n_mem, mem_w = mem_k.shape[1:]
    nt = t // tq
    return pl.pallas_call(
        functools.partial(_memattn_kernel, head_dim=mem_w // MEM_HEADS),
        grid=(b, nt),
        in_specs=[pl.BlockSpec((tq, mem_w), lambda bi, ti: (bi * nt + ti, 0)),
                  pl.BlockSpec((None, n_mem, mem_w), lambda bi, ti: (bi, 0, 0)),
                  pl.BlockSpec((None, n_mem, mem_w), lambda bi, ti: (bi, 0, 0))],
        out_specs=pl.BlockSpec((tq, mem_w), lambda bi, ti: (bi * nt + ti, 0)),
        out_shape=jax.ShapeDtypeStruct((b * t, mem_w), BF16),
        compiler_params=_params(("parallel", "parallel"), "memattn"),
        name="memattn",
    )(qm, mem_k, mem_v)


def _mix_kernel(x_ref, h_ref, oa_ref, os_ref, om_ref, wg0_ref, wg1_ref, wg2_ref,
                wa_ref, ws_ref, wm_ref, gpost_ref, o_ref, ss_scr, *, tn):
    j = pl.program_id(1)

    @pl.when(j == 0)
    def _():
        ss_scr[...] = jnp.zeros_like(ss_scr)

    h = h_ref[...]
    oa, os_, om = oa_ref[...], os_ref[...], om_ref[...]
    partial = jnp.zeros(ss_scr.shape, F32)
    for c in range(tn // MXU_WIDTH):
        sl = slice(c * MXU_WIDTH, (c + 1) * MXU_WIDTH)
        merged = (jax.nn.sigmoid(_dot(h, wg0_ref[:, sl])) * _dot(oa, wa_ref[:, sl])
                  + jax.nn.sigmoid(_dot(h, wg1_ref[:, sl])) * _dot(os_, ws_ref[:, sl])
                  + jax.nn.sigmoid(_dot(h, wg2_ref[:, sl])) * _dot(om, wm_ref[:, sl]))
        o_ref[:, pl.ds(pl.multiple_of(j * tn + c * MXU_WIDTH, MXU_WIDTH), MXU_WIDTH)] = merged
        sq = merged * merged
        partial += sum(sq[:, k * LANES:(k + 1) * LANES] for k in range(MXU_WIDTH // LANES))
    ss_scr[...] += partial

    @pl.when(j == pl.num_programs(1) - 1)
    def _():
        ss = jnp.sum(ss_scr[...], axis=-1, keepdims=True)
        inv = lax.rsqrt(ss / o_ref.shape[1] + EPS)
        o_ref[...] = x_ref[...] + (o_ref[...] * inv) * gpost_ref[...]


def _mix_residual(x2d, h, o_a, o_s, o_m, w_in_bf, gate_col0, w_oa, w_out_bf, g_post, tm, tn):
    m, d = x2d.shape
    nj = d // tn
    q_w, ssm_w, mem_w = o_a.shape[1], o_s.shape[1], o_m.shape[1]
    assert gate_col0 % tn == 0 and q_w % ssm_w == 0 and (q_w + ssm_w) % mem_w == 0
    gate_blk0 = gate_col0 // tn

    def rows(w):
        mode = dict(pipeline_mode=pl.Buffered(1)) if m == tm else {}
        return pl.BlockSpec((tm, w), lambda i, j: (i, 0), **mode)

    def gate_spec(br):
        return pl.BlockSpec((d, tn), lambda i, j: (0, gate_blk0 + br * nj + j))

    vec = pl.BlockSpec((1, d), lambda i, j: (0, 0))
    return pl.pallas_call(
        functools.partial(_mix_kernel, tn=tn),
        grid=(m // tm, nj),
        in_specs=[rows(d), rows(d), rows(q_w), rows(ssm_w), rows(mem_w),
                  gate_spec(0), gate_spec(1), gate_spec(2),
                  pl.BlockSpec((q_w, tn), lambda i, j: (0, j)),
                  pl.BlockSpec((ssm_w, tn), lambda i, j: (q_w // ssm_w, j)),
                  pl.BlockSpec((mem_w, tn), lambda i, j: ((q_w + ssm_w) // mem_w, j)), vec],
        out_specs=rows(d),
        out_shape=jax.ShapeDtypeStruct((m, d), F32),
        scratch_shapes=[pltpu.VMEM((tm, LANES), F32)],
        compiler_params=_params(("parallel", "arbitrary"), "mix"),
        name="mix",
    )(x2d, h, o_a, o_s, o_m, w_in_bf, w_in_bf, w_in_bf, w_oa, w_out_bf, w_out_bf,
      g_post.reshape(1, d))


def _ffn_kernel(x_ref, gpre_ref, *refs, n_sub, n_sub_valid):
    w_refs = refs[:3 * n_sub]
    cw_ref, cb_ref, gpost_ref, cprev_ref, o_ref, tail_ref, h_scr, a_scr, b_scr, carry_scr = refs[3 * n_sub:]
    j = pl.program_id(2)
    nb, tt, d = x_ref.shape
    tf = a_scr.shape[2]
    rows = nb * tt

    @pl.when(j == 0)
    def _():
        h_scr[...] = _rms(x_ref[...].reshape(rows, d), gpre_ref[...]).astype(BF16)
        o_ref[...] = jnp.zeros_like(o_ref)

    for s in range(n_sub):
        sub = j * n_sub + s
        step_cols = slice(s * tf, (s + 1) * tf)
        wua_ref, wub_ref, wd_ref = w_refs[3 * s:3 * s + 3]

        @pl.when(sub < n_sub_valid)
        def _(sub=sub, step_cols=step_cols, wua_ref=wua_ref, wub_ref=wub_ref, wd_ref=wd_ref):
            _ffn_sub_step(sub, step_cols, x_ref, wua_ref, wub_ref, cw_ref, cb_ref, wd_ref, cprev_ref,
                          o_ref, tail_ref, h_scr, a_scr, b_scr, carry_scr)

        @pl.when(sub >= n_sub_valid)
        def _(step_cols=step_cols):
            tail_ref[:, :, step_cols] = jnp.zeros((nb, SUBLANES, tf), F32)

    @pl.when(j == pl.num_programs(2) - 1)
    def _():
        f = _rms(o_ref[...].reshape(rows, d), gpost_ref[...]).reshape(nb, tt, d)
        o_ref[...] = x_ref[...] + f


def _ffn_sub_step(sub, step_cols, x_ref, wua_ref, wub_ref, cw_ref, cb_ref, wd_ref, cprev_ref,
                  o_ref, tail_ref, h_scr, a_scr, b_scr, carry_scr):
    i = pl.program_id(1)
    nb, tt, d = x_ref.shape
    tf = a_scr.shape[2]
    h = h_scr[...]
    a_scr[:, 0:SUBLANES, :] = jnp.where(i == 0, cprev_ref[:, :, step_cols], carry_scr[sub])
    n_slices = tf // MXU_WIDTH
    cols = [slice(s * MXU_WIDTH, (s + 1) * MXU_WIDTH) for s in range(n_slices)]
    for sl in cols:
        a_scr[:, SUBLANES:, sl] = _dot(h, wua_ref[:, sl]).reshape(nb, tt, MXU_WIDTH)
        b_scr[:, sl] = _dot(h, wub_ref[:, sl])
    cw = cw_ref[:, step_cols]
    cb = cb_ref[:, step_cols]
    for sl in cols:
        gated = []
        for bi in range(nb):
            a = a_scr[bi, SUBLANES:, sl]
            first = jnp.concatenate([a_scr[bi, 0:SUBLANES, sl], a[0:SUBLANES]], axis=0)
            prev1 = jnp.concatenate([first[SUBLANES - 1:2 * SUBLANES - 1],
                                     pltpu.roll(a, 1, 0)[SUBLANES:]], axis=0)
            prev2 = jnp.concatenate([first[SUBLANES - 2:2 * SUBLANES - 2],
                                     pltpu.roll(a, 2, 0)[SUBLANES:]], axis=0)
            conv = (prev2 * cw[0:1, sl] + prev1 * cw[1:2, sl] + a * cw[2:3, sl]) + cb[:, sl]
            gated.append(_gelu(conv) * b_scr[bi * tt:(bi + 1) * tt, sl])
        act = (gated[0] if nb == 1 else jnp.concatenate(gated, axis=0)).astype(BF16)
        o_ref[...] += _dot(act, wd_ref[sl, :]).reshape(nb, tt, d)
    tail = a_scr[:, tt:tt + SUBLANES, :]
    carry_scr[sub] = tail
    tail_ref[:, :, step_cols] = tail


def _conv_ffn(x3d, g_pre, w_up, conv_w, conv_b, w_down, g_post, conv_prev8, nb, tt, tf, n_sub):
    b, t, d = x3d.shape
    d_ff = w_down.shape[0]
    n_valid = d_ff // tf
    step_cols = n_sub * tf
    nj = -(-n_valid // n_sub)
    d_ff_pad = nj * step_cols
    nt = t // tt
    pad = ((0, 0), (0, d_ff_pad - d_ff))
    conv_w = jnp.pad(conv_w.astype(F32), pad)
    conv_b = jnp.pad(conv_b.astype(F32).reshape(1, d_ff), pad)
    conv_prev8 = jnp.pad(conv_prev8, ((0, 0),) + pad)

    def sub_block(s):
        return lambda j: jnp.minimum(j * n_sub + s, n_valid - 1)

    w_specs = []
    for s in range(n_sub):
        blk = sub_block(s)
        w_specs += [pl.BlockSpec((d, tf), lambda bi, i, j, blk=blk: (0, blk(j))),
                    pl.BlockSpec((d, tf), lambda bi, i, j, blk=blk: (0, n_valid + blk(j))),
                    pl.BlockSpec((tf, d), lambda bi, i, j, blk=blk: (blk(j), 0))]
    mode = dict(pipeline_mode=pl.Buffered(1)) if (b // nb) * nt == 1 else {}
    x2, tails = pl.pallas_call(
        functools.partial(_ffn_kernel, n_sub=n_sub, n_sub_valid=n_valid),
        grid=(b // nb, nt, nj),
        in_specs=[pl.BlockSpec((nb, tt, d), lambda bi, i, j: (bi, i, 0), **mode),
                  pl.BlockSpec((1, d), lambda bi, i, j: (0, 0))] + w_specs + [
                  pl.BlockSpec((CONV_W, step_cols), lambda bi, i, j: (0, j)),
                  pl.BlockSpec((1, step_cols), lambda bi, i, j: (0, j)),
                  pl.BlockSpec((1, d), lambda bi, i, j: (0, 0)),
                  pl.BlockSpec((nb, SUBLANES, step_cols), lambda bi, i, j: (bi, 0, j))],
        out_specs=[pl.BlockSpec((nb, tt, d), lambda bi, i, j: (bi, i, 0), **mode),
                   pl.BlockSpec((nb, None, SUBLANES, step_cols), lambda bi, i, j: (bi, i, 0, j))],
        out_shape=[jax.ShapeDtypeStruct((b, t, d), F32),
                   jax.ShapeDtypeStruct((b, nt, SUBLANES, d_ff_pad), F32)],
        scratch_shapes=[pltpu.VMEM((nb * tt, d), BF16),
                        pltpu.VMEM((nb, tt + SUBLANES, tf), F32), pltpu.VMEM((nb * tt, tf), F32),
                        pltpu.VMEM((nj * n_sub, nb, SUBLANES, tf), F32)],
        compiler_params=_params(("arbitrary", "arbitrary", "arbitrary"), "ffn"),
        name="ffn",
    )(x3d, g_pre.reshape(1, d), *([w_up, w_up, w_down] * n_sub), conv_w, conv_b, g_post.reshape(1, d), conv_prev8)
    return x2, tails[..., :d_ff]


def _tiles(b, t):
    row_tile = 512
    tt = min(t, row_tile)
    return dict(
        proj_rows=512,
        attn_batch=max(1, min(b, 1024 // t)),
        s5_steps=min(t, 1024 // b),
        memattn_rows=tt,
        mix_rows=512, mix_cols=512,
        ffn_batch=min(b, row_tile // tt), ffn_rows=tt,
        ffn_sub_steps=2)


FFN_COLS = 2 * MXU_WIDTH


def _layer(x, attn_past, s0, conv_prev, mem_k, mem_v, bias, lw):
    b, t, d = x.shape
    m = b * t
    q_w, kv_w, ssm_w, mem_w = lw['q_w'], lw['kv_w'], lw['ssm_w'], lw['mem_w']
    n_state = lw['n_state']
    d_ff = lw['d_ff']
    tiles = _tiles(b, t)
    x2d = x.reshape(m, d)

    q, kv, u, qm, h = _norm_proj(x2d, lw['norm_pre_mix'], lw['proj_outputs'], tiles['proj_rows'])
    kv3 = kv.reshape(b, t, 2 * kv_w)
    if attn_past is None:
        hist_k = jnp.zeros((b, WINDOW, kv_w), F32)
        hist_v = hist_k
        first_valid = WINDOW
    else:
        hist_k = attn_past[0].astype(F32).reshape(b, WINDOW, kv_w)
        hist_v = attn_past[1].astype(F32).reshape(b, WINDOW, kv_w)
        first_valid = 0
    o_a = _band_attention(q.reshape(b, t, q_w), kv3, hist_k, hist_v, bias, lw['attn_sinks'],
                          tiles['attn_batch'], first_valid).reshape(m, q_w)

    if s0 is None:
        s0_re = jnp.zeros((b, n_state), F32)
        s0_im = s0_re
    else:
        s0_re = s0[0].astype(F32).reshape(b, n_state)
        s0_im = s0[1].astype(F32).reshape(b, n_state)
    u_tb = u.reshape(b, t, ssm_w).transpose(1, 0, 2)
    o_s_tb, s_re, s_im = _s5_layer(u_tb, s0_re, s0_im, lw['s5'], tiles['s5_steps'])
    o_s = o_s_tb.reshape(t, b, ssm_w).transpose(1, 0, 2).reshape(m, ssm_w)

    o_m = _memory_attention(qm, mem_k, mem_v, b, t, tiles['memattn_rows'])

    x1 = _mix_residual(x2d, h, o_a, o_s, o_m, lw['w_in_bf'], lw['gate_col0'], lw['w_oa'],
                       lw['w_out_bf'], lw['norm_post_mix'], tiles['mix_rows'], tiles['mix_cols'])

    if conv_prev is None:
        conv_prev8 = jnp.zeros((b, SUBLANES, d_ff), F32)
    else:
        conv_prev8 = jnp.pad(conv_prev.astype(F32), ((0, 0), (SUBLANES - (CONV_W - 1), 0), (0, 0)))
    x2, tails = _conv_ffn(x1.reshape(b, t, d), lw['norm_pre_ffn'], lw['w_up'], lw['conv_w'], lw['conv_b'],
                          lw['w_down'], lw['norm_post_ffn'], conv_prev8,
                          tiles['ffn_batch'], tiles['ffn_rows'], FFN_COLS, tiles['ffn_sub_steps'])
    conv_new = tails[:, -1, SUBLANES - (CONV_W - 1):, :]

    n_kv = kv_w // HEAD_DIM
    k_new = jnp.concatenate([hist_k, kv3[:, :, :kv_w]], axis=1)[:, -WINDOW:].reshape(b, WINDOW, n_kv, HEAD_DIM)
    v_new = jnp.concatenate([hist_v, kv3[:, :, kv_w:]], axis=1)[:, -WINDOW:].reshape(b, WINDOW, n_kv, HEAD_DIM)
    return x2, k_new, v_new, s_re, s_im, conv_new


def kernel(x_prompt, x_sample, cache_attn_k, cache_attn_v, cache_mem_k, cache_mem_v, state_ssm_re, state_ssm_im, state_conv, mem_prompt, rel_bias_table, norm_pre_mix, norm_post_mix, norm_pre_ffn, norm_post_ffn, norm_mem, w_in, attn_sinks, ssm_a_re, ssm_a_im, ssm_log_dt, ssm_b_re, ssm_b_im, ssm_c_re, ssm_c_im, ssm_d, w_glu, w_mem_kv, w_out, w_up, conv_w, conv_b, w_down):
    depth = w_in.shape[0]
    bp, _, d = x_prompt.shape
    n_mem = mem_prompt.shape[1]
    n_q = attn_sinks.shape[1]
    n_kv, hd = cache_attn_k.shape[-2:]
    assert hd == HEAD_DIM and n_kv == N_KV and cache_attn_k.shape[2] == WINDOW
    groups, p_state = ssm_a_re.shape[1:]
    q_w, kv_w = n_q * HEAD_DIM, n_kv * HEAD_DIM
    ssm_w = ssm_d.shape[1]
    mem_w = w_mem_kv.shape[2] // 2
    proj_w = q_w + 2 * kv_w + ssm_w + mem_w
    mem_hd = mem_w // MEM_HEADS

    xp, xs = x_prompt, x_sample
    outs = [[] for _ in range(12)]
    for l in range(depth):
        bias = _rel_bias(rel_bias_table)
        lp = dict(ssm_a_re=ssm_a_re[l], ssm_a_im=ssm_a_im[l], ssm_log_dt=ssm_log_dt[l],
                  ssm_b_re=ssm_b_re[l], ssm_b_im=ssm_b_im[l], ssm_c_re=ssm_c_re[l], ssm_c_im=ssm_c_im[l],
                  ssm_d=ssm_d[l], w_glu=w_glu[l])
        w_in_bf = w_in[l].astype(BF16)
        w_out_bf = w_out[l].astype(BF16)
        grp = n_q // n_kv
        w_q = w_in_bf[:, :q_w].reshape(d, n_kv, grp, HEAD_DIM).transpose(0, 2, 1, 3).reshape(d, q_w)
        w_oa = w_out_bf[:q_w].reshape(n_kv, grp, HEAD_DIM, d).transpose(1, 0, 2, 3).reshape(q_w, d)
        u_off = q_w + 2 * kv_w
        proj_outputs = [(BF16, [(w_q, q_w, 0)]),
                        (F32, _col_blocks(w_in_bf, q_w, 2 * kv_w)),
                        (F32, _col_blocks(w_in_bf, u_off, ssm_w)),
                        (BF16, _col_blocks(w_in_bf, u_off + ssm_w, mem_w)),
                        (BF16, [])]
        lw = dict(q_w=q_w, kv_w=kv_w, ssm_w=ssm_w, mem_w=mem_w, n_state=groups * p_state,
                  norm_pre_mix=norm_pre_mix[l], norm_post_mix=norm_post_mix[l],
                  norm_pre_ffn=norm_pre_ffn[l], norm_post_ffn=norm_post_ffn[l],
                  proj_outputs=proj_outputs, w_in_bf=w_in_bf, gate_col0=proj_w, w_oa=w_oa, w_out_bf=w_out_bf,
                  attn_sinks=attn_sinks[l], s5=_s5_params(lp),
                  d_ff=w_down.shape[1], w_up=w_up[l].astype(BF16), conv_w=conv_w[l], conv_b=conv_b[l],
                  w_down=w_down[l].astype(BF16))

        w_mem_bf = w_mem_kv[l].astype(BF16)
        mk_p, mv_p = _norm_proj(mem_prompt.reshape(bp * n_mem, d), norm_mem[l],
                                [(F32, [(w_mem_bf, mem_w, 0)]), (F32, [(w_mem_bf, mem_w, 1)])],
                                _tiles(bp, n_mem)['proj_rows'])
        mk_p = mk_p.reshape(bp, n_mem, mem_w)
        mv_p = mv_p.reshape(bp, n_mem, mem_w)
        xp, k_p, v_p, sr_p, si_p, c_p = _layer(xp, None, None, None, mk_p, mv_p, bias, lw)

        bs = xs.shape[0]
        xs, k_s, v_s, sr_s, si_s, c_s = _layer(
            xs, (cache_attn_k[l], cache_attn_v[l]), (state_ssm_re[l], state_ssm_im[l]), state_conv[l],
            cache_mem_k[l].reshape(bs, n_mem, mem_w), cache_mem_v[l].reshape(bs, n_mem, mem_w), bias, lw)

        vals = (k_p, v_p, sr_p.reshape(bp, groups, p_state), si_p.reshape(bp, groups, p_state), c_p,
                mk_p.reshape(bp, n_mem, MEM_HEADS, mem_hd), mv_p.reshape(bp, n_mem, MEM_HEADS, mem_hd),
                k_s, v_s, sr_s.reshape(bs, groups, p_state), si_s.reshape(bs, groups, p_state), c_s)
        for acc, val in zip(outs, vals):
            acc.append(val)
    return (xp, xs) + tuple(jnp.stack(o) for o in outs)
```

```python
import functools
import math

import jax
import jax.numpy as jnp
from jax import lax
from jax.experimental import pallas as pl
from jax.experimental.pallas import tpu as pltpu

F32 = jnp.float32
BF16 = jnp.bfloat16

EPS = 1e-6
NEG_INF = -1e30
CHUNK = 64
WINDOW = 128
HEAD_DIM = 64
N_KV = 4
MAX_DISTANCE = 128
MEM_HEADS = 4
CONV_W = 3

MIB = 1024 * 1024
LANES = 128
MXU_WIDTH = 256
SUBLANES = 8
SCAN_CARRY_ELEMS = 4096


VMEM_LIMIT_MIB = dict(proj=48, attn=40, s5=48, memattn=32, mix=56, ffn=60)


def _params(semantics, call):
    return pltpu.CompilerParams(dimension_semantics=semantics, vmem_limit_bytes=VMEM_LIMIT_MIB[call] * MIB)


def _rms(x, g):
    y = x * lax.rsqrt(jnp.mean(x * x, axis=-1, keepdims=True) + EPS)
    return y * g


def _dot(a, b):
    return jnp.dot(a, b, preferred_element_type=F32)


def _dot_t(a, b):
    return lax.dot_general(a, b, (((1,), (1,)), ((), ())), preferred_element_type=F32)


def _proj_kernel(x_ref, g_ref, *refs, blocks_per_out):
    n_w = sum(blocks_per_out)
    w_refs, o_refs = refs[:n_w], refs[n_w:]
    h = _rms(x_ref[...], g_ref[...]).astype(BF16)
    first = 0
    for o_ref, count in zip(o_refs, blocks_per_out):
        if count == 0:
            o_ref[...] = h
            continue
        parts = [_dot(h, w_ref[...]) for w_ref in w_refs[first:first + count]]
        r = parts[0] if count == 1 else jnp.concatenate(parts, axis=1)
        o_ref[...] = r.astype(o_ref.dtype)
        first += count


def _col_blocks(arr, start, width):
    blk = math.gcd(start, width) if start else width
    assert blk % LANES == 0
    return [(arr, blk, start // blk + k) for k in range(width // blk)]


def _norm_proj(x2d, g, outputs, tm):
    m, d = x2d.shape
    w_args, w_specs, out_specs, out_shapes = [], [], [], []
    for dt, blocks in outputs:
        for arr, width, blk in blocks:
            assert arr.shape[0] == d and arr.shape[1] % width == 0
            w_args.append(arr)
            w_specs.append(pl.BlockSpec((d, width), lambda i, blk=blk: (0, blk)))
        total = sum(width for _, width, _ in blocks) if blocks else d
        out_specs.append(pl.BlockSpec((tm, total), lambda i: (i, 0)))
        out_shapes.append(jax.ShapeDtypeStruct((m, total), dt))
    return pl.pallas_call(
        functools.partial(_proj_kernel, blocks_per_out=tuple(len(b) for _, b in outputs)),
        grid=(m // tm,),
        in_specs=[pl.BlockSpec((tm, d), lambda i: (i, 0)),
                  pl.BlockSpec((1, d), lambda i: (0, 0))] + w_specs,
        out_specs=out_specs,
        out_shape=out_shapes,
        compiler_params=_params(("parallel",), "proj"),
        name="proj",
    )(x2d, g.reshape(1, d), *w_args)


def _bias_kernel(idx_ref, table_ref, o_ref, *, n_buckets, n_heads):
    idx = idx_ref[...]
    for h in range(n_heads):
        acc = jnp.zeros(idx.shape, F32)
        for b in range(n_buckets):
            acc = jnp.where(idx == b, table_ref[b, h], acc)
        o_ref[h] = acc


def _t5_bucket(rel, n_buckets):
    half = n_buckets // 2
    max_exact = half // 2
    n = jnp.abs(rel)
    large = max_exact + (jnp.log(jnp.maximum(n, 1).astype(F32) / max_exact)
                         / math.log(MAX_DISTANCE / max_exact) * (half - max_exact)).astype(jnp.int32)
    large = jnp.minimum(large, half - 1)
    return jnp.where(rel > 0, half, 0) + jnp.where(n < max_exact, n, large)


def _rel_bias(table):
    n_buckets, n_heads = table.shape
    lk = WINDOW + CHUNK
    rel = jnp.arange(lk)[None, :] - WINDOW - jnp.arange(CHUNK)[:, None]
    idx = _t5_bucket(rel, n_buckets).astype(jnp.int32)
    return pl.pallas_call(
        functools.partial(_bias_kernel, n_buckets=n_buckets, n_heads=n_heads),
        in_specs=[pl.BlockSpec((CHUNK, lk), lambda: (0, 0)),
                  pl.BlockSpec(memory_space=pltpu.SMEM)],
        out_specs=pl.BlockSpec((n_heads, CHUNK, lk), lambda: (0, 0, 0)),
        out_shape=jax.ShapeDtypeStruct((n_heads, CHUNK, lk), F32),
        name="bias",
    )(idx, table.astype(F32))


def _attn_kernel(q_ref, k_ref, v_ref, hk_ref, hv_ref, bias_ref, o_ref, kf_scr, vf_scr, s_scr, *, n_masked, grp):
    nb, t, _ = q_ref.shape
    kv_w = k_ref.shape[2]
    n_chunks = t // CHUNK
    lk = WINDOW + CHUNK
    lkp = bias_ref.shape[2]
    pair_w = 2 * HEAD_DIM
    scale = HEAD_DIM ** -0.5
    low_half = lax.broadcasted_iota(jnp.int32, (1, pair_w), 1) < HEAD_DIM
    ones = jnp.ones((lkp, pair_w), BF16)
    kv_pad = jnp.zeros((lkp - lk, kv_w), BF16)

    kf_scr[:, 0:WINDOW, :] = hk_ref[...].astype(BF16)
    kf_scr[:, WINDOW:, :] = k_ref[...].astype(BF16)
    vf_scr[:, 0:WINDOW, :] = hv_ref[...].astype(BF16)
    vf_scr[:, WINDOW:, :] = v_ref[...].astype(BF16)

    n_total = nb * n_chunks

    def locate(n):
        bi = n // n_chunks
        c = n % n_chunks
        return bi, c, pl.multiple_of(c * CHUNK, CHUNK)

    def score_chunk(n, slot):
        bi, c, r0 = locate(n)
        q = q_ref[bi, pl.ds(r0, CHUNK), :] * scale
        k = jnp.concatenate([kf_scr[bi, pl.ds(r0, lk), :], kv_pad], axis=0)
        variant = jnp.minimum(c, n_masked) * N_KV
        for h in range(N_KV):
            j, e = divmod(h, 2)
            qp = jnp.concatenate([q[:, g * kv_w + j * pair_w:g * kv_w + (j + 1) * pair_w]
                                  for g in range(grp)], axis=0)
            in_head = low_half if e == 0 else jnp.logical_not(low_half)
            qh = jnp.where(in_head, qp, jnp.zeros_like(qp))
            s_scr[slot, h] = _dot_t(qh, k[:, j * pair_w:(j + 1) * pair_w]) + bias_ref[variant + h]

    def chunk(n, slot):
        score_chunk(jnp.minimum(n + 1, n_total - 1), 1 - slot)
        bi, c, r0 = locate(n)
        v = jnp.concatenate([vf_scr[bi, pl.ds(r0, lk), :], kv_pad], axis=0)
        scores = [s_scr[slot, h] for h in range(N_KV)]
        probs = [jnp.exp(s - jnp.max(s, axis=-1, keepdims=True)).astype(BF16) for s in scores]
        v_ext = [jnp.concatenate([v[:, j * pair_w:(j + 1) * pair_w], ones], axis=1) for j in range(N_KV // 2)]
        sums = [_dot(p, v_ext[h // 2]) for h, p in enumerate(probs)]
        outs = [r[:, :pair_w] / r[:, pair_w:] for r in sums]
        for j in range(N_KV // 2):
            o_pair = jnp.where(low_half, outs[2 * j], outs[2 * j + 1])
            for g in range(grp):
                o_ref[bi, pl.ds(r0, CHUNK), g * kv_w + j * pair_w:g * kv_w + (j + 1) * pair_w] = (
                    o_pair[g * CHUNK:(g + 1) * CHUNK].astype(BF16))

    def chunk_pair(p, carry):
        chunk(2 * p, 0)
        chunk(2 * p + 1, 1)
        return carry

    assert n_total % 2 == 0
    score_chunk(0, 0)
    lax.fori_loop(0, n_total // 2, chunk_pair, 0, unroll=4)


def _band_attention(q3, kv3, hist_k, hist_v, bias, sinks, nb, first_valid):
    b, t, q_w = q3.shape
    kv_w = hist_k.shape[2]
    grp = q_w // HEAD_DIM // N_KV
    lk = WINDOW + CHUNK
    lkp = 2 * LANES
    n_masked = first_valid // CHUNK
    sink_col = jnp.broadcast_to(sinks.astype(F32).reshape(N_KV, grp, 1, 1), (N_KV, grp, CHUNK, 1))
    ext = jnp.concatenate([bias.reshape(N_KV, grp * CHUNK, lk), sink_col.reshape(N_KV, grp * CHUNK, 1),
                           jnp.full((N_KV, grp * CHUNK, lkp - lk - 1), NEG_INF, F32)], axis=2)
    key = jnp.arange(lkp)
    variants = [jnp.where(key < (n_masked - i) * CHUNK, NEG_INF, ext) for i in range(n_masked + 1)]
    bias_ext = jnp.concatenate(variants, axis=0)
    return pl.pallas_call(
        functools.partial(_attn_kernel, n_masked=n_masked, grp=grp),
        grid=(b // nb,),
        in_specs=[pl.BlockSpec((nb, t, q_w), lambda i: (i, 0, 0)),
                  pl.BlockSpec((nb, t, kv_w), lambda i: (i, 0, 0)),
                  pl.BlockSpec((nb, t, kv_w), lambda i: (i, 0, 1)),
                  pl.BlockSpec((nb, WINDOW, kv_w), lambda i: (i, 0, 0)),
                  pl.BlockSpec((nb, WINDOW, kv_w), lambda i: (i, 0, 0)),
                  pl.BlockSpec(bias_ext.shape, lambda i: (0, 0, 0))],
        out_specs=pl.BlockSpec((nb, t, q_w), lambda i: (i, 0, 0)),
        out_shape=jax.ShapeDtypeStruct((b, t, q_w), BF16),
        scratch_shapes=[pltpu.VMEM((nb, WINDOW + t, kv_w), BF16), pltpu.VMEM((nb, WINDOW + t, kv_w), BF16),
                        pltpu.VMEM((2, N_KV, grp * CHUNK, lkp), F32)],
        compiler_params=_params(("parallel",), "attn"),
        name="attn",
    )(q3, kv3, kv3, hist_k, hist_v, bias_ext)


def _gelu(x):
    return jax.nn.gelu(x)


def _s5_kernel(u_ref, s0re_ref, s0im_ref, are_ref, aim_ref, wbre_ref, wbim_ref, wcre_ref, wcim_ref,
               d_ref, wglu_ref, o_ref, sre_out_ref, sim_out_ref,
               sre_scr, sim_scr, st_re, st_im, y_scr, *, batch, steps, lane_chunk):
    i = pl.program_id(0)

    @pl.when(i == 0)
    def _():
        st_re[...] = s0re_ref[...]
        st_im[...] = s0im_ref[...]

    u = u_ref[...]
    ub = u.astype(BF16)
    n_in_blk, in_blk, st_blk = wbre_ref.shape
    n_out_blk, k_blk, out_blk = wcre_ref.shape
    n_state = sre_scr.shape[1]

    def project_in(r):
        blk = ub[:, r * in_blk:(r + 1) * in_blk]
        sre_scr[:, r * st_blk:(r + 1) * st_blk] = _dot(blk, wbre_ref[r])
        sim_scr[:, r * st_blk:(r + 1) * st_blk] = _dot(blk, wbim_ref[r])

    def scan(lc):
        sl = slice(lc * lane_chunk, (lc + 1) * lane_chunk)
        ar = are_ref[:, sl]
        ai = aim_ref[:, sl]
        sr = st_re[:, sl]
        si = st_im[:, sl]
        for t in range(steps):
            rows = slice(t * batch, (t + 1) * batch)
            sr, si = (ar * sr - ai * si + sre_scr[rows, sl], ar * si + ai * sr + sim_scr[rows, sl])
            sre_scr[rows, sl] = sr
            sim_scr[rows, sl] = si
        st_re[:, sl] = sr
        st_im[:, sl] = si

    def project_out(kb):
        ksl = slice(kb * k_blk, (kb + 1) * k_blk)
        y_scr[:, kb * out_blk:(kb + 1) * out_blk] = (
            _dot(sre_scr[:, ksl].astype(BF16), wcre_ref[kb])
            + _dot(sim_scr[:, ksl].astype(BF16), wcim_ref[kb]))

    chunks_per_in = st_blk // lane_chunk
    done_out = 0
    project_in(0)
    for r in range(n_in_blk):
        if r + 1 < n_in_blk:
            project_in(r + 1)
        for lc in range(r * chunks_per_in, (r + 1) * chunks_per_in):
            scan(lc)
        while done_out < n_out_blk and (done_out + 1) * k_blk <= (r + 1) * st_blk:
            project_out(done_out)
            done_out += 1
    assert done_out == n_out_blk and n_in_blk * st_blk == n_state
    y = _gelu(y_scr[...] + d_ref[...] * u)
    o_ref[...] = (y * jax.nn.sigmoid(_dot(y.astype(BF16), wglu_ref[...]))).astype(BF16)

    @pl.when(i == pl.num_programs(0) - 1)
    def _():
        sre_out_ref[...] = st_re[...]
        sim_out_ref[...] = st_im[...]


def _s5_params(lp):
    lam = lax.complex(lp['ssm_a_re'].astype(F32), lp['ssm_a_im'].astype(F32))
    dt = jnp.exp(lp['ssm_log_dt'].astype(F32))[:, None]
    a_bar = jnp.exp(lam * dt)
    b_mat = lax.complex(lp['ssm_b_re'].astype(F32), lp['ssm_b_im'].astype(F32))
    b_bar = ((a_bar - 1.0) / lam)[..., None] * b_mat
    g, p, c = b_bar.shape
    gb_in = MXU_WIDTH // c
    gb_out = LANES // c

    def b_blocks(x):
        x = x.reshape(g // gb_in, gb_in, p, c).transpose(0, 1, 3, 2)
        x = jnp.einsum('rgcp,gh->rgchp', x, jnp.eye(gb_in, dtype=F32))
        return x.reshape(g // gb_in, gb_in * c, gb_in * p).astype(BF16)

    def c_blocks(x):
        x = x.reshape(g // gb_out, gb_out, c, p).transpose(0, 1, 3, 2)
        x = jnp.einsum('kgpc,gh->kgphc', x, jnp.eye(gb_out, dtype=F32))
        return x.reshape(g // gb_out, gb_out * p, gb_out * c).astype(BF16)

    return dict(a_re=jnp.real(a_bar).reshape(1, g * p), a_im=jnp.imag(a_bar).reshape(1, g * p),
                wb_re=b_blocks(jnp.real(b_bar)), wb_im=b_blocks(jnp.imag(b_bar)),
                wc_re=c_blocks(lp['ssm_c_re'].astype(F32)), wc_im=c_blocks(-lp['ssm_c_im'].astype(F32)),
                d=lp['ssm_d'].astype(F32).reshape(1, -1), w_glu=lp['w_glu'].astype(BF16))


def _s5_layer(u_tb, s0_re, s0_im, sp, steps):
    t, b, ssm_w = u_tb.shape
    n_state = s0_re.shape[1]
    rows = steps * b
    lane_chunk = SCAN_CARRY_ELEMS // b
    a_re = jnp.broadcast_to(sp['a_re'], (b, n_state))
    a_im = jnp.broadcast_to(sp['a_im'], (b, n_state))

    def full(x):
        nd = x.ndim
        return pl.BlockSpec(x.shape, lambda i: (0,) * nd)

    consts = [s0_re, s0_im, a_re, a_im, sp['wb_re'], sp['wb_im'], sp['wc_re'], sp['wc_im'], sp['d'], sp['w_glu']]
    return pl.pallas_call(
        functools.partial(_s5_kernel, batch=b, steps=steps, lane_chunk=lane_chunk),
        grid=(t // steps,),
        in_specs=[pl.BlockSpec((rows, ssm_w), lambda i: (i, 0))] + [full(x) for x in consts],
        out_specs=[pl.BlockSpec((rows, ssm_w), lambda i: (i, 0)),
                   pl.BlockSpec((b, n_state), lambda i: (0, 0)),
                   pl.BlockSpec((b, n_state), lambda i: (0, 0))],
        out_shape=[jax.ShapeDtypeStruct((t * b, ssm_w), BF16),
                   jax.ShapeDtypeStruct((b, n_state), F32),
                   jax.ShapeDtypeStruct((b, n_state), F32)],
        scratch_shapes=[pltpu.VMEM((rows, n_state), F32), pltpu.VMEM((rows, n_state), F32),
                        pltpu.VMEM((b, n_state), F32), pltpu.VMEM((b, n_state), F32),
                        pltpu.VMEM((rows, ssm_w), F32)],
        compiler_params=_params(("arbitrary",), "s5"),
        name="s5",
    )(u_tb.reshape(t * b, ssm_w), *consts)


def _memattn_kernel(q_ref, k_ref, v_ref, o_ref, *, head_dim):
    q = q_ref[...]
    k = k_ref[...].astype(BF16)
    v = v_ref[...].astype(BF16)
    scale = head_dim ** -0.5
    ones = jnp.ones((v.shape[0], head_dim), BF16)
    heads = [slice(h * head_dim, (h + 1) * head_dim) for h in range(MEM_HEADS)]
    scores = [_dot_t(q[:, sl], k[:, sl]) * scale for sl in heads]
    probs = [jnp.exp(s - jnp.max(s, axis=-1, keepdims=True)).astype(BF16) for s in scores]
    sums = [_dot(p, jnp.concatenate([v[:, sl], ones], axis=1)) for p, sl in zip(probs, heads)]
    for r, sl in zip(sums, heads):
        o_ref[:, sl] = (r[:, :head_dim] / r[:, head_dim:]).astype(BF16)


def _memory_attention(qm, mem_k, mem_v, b, t, tq):
    n_mem, mem_w = mem_k.shape[1:]
    nt = t // tq
    return pl.pallas_call(
        functools.partial(_memattn_kernel, head_dim=mem_w // MEM_HEADS),
        grid=(b, nt),
        in_specs=[pl.BlockSpec((tq, mem_w), lambda bi, ti: (bi * nt + ti, 0)),
                  pl.BlockSpec((None, n_mem, mem_w), lambda bi, ti: (bi, 0, 0)),
                  pl.BlockSpec((None, n_mem, mem_w), lambda bi, ti: (bi, 0, 0))],
        out_specs=pl.BlockSpec((tq, mem_w), lambda bi, ti: (bi * nt + ti, 0)),
        out_shape=jax.ShapeDtypeStruct((b * t, mem_w), BF16),
        compiler_params=_params(("parallel", "parallel"), "memattn"),
        name="memattn",
    )(qm, mem_k, mem_v)


def _mix_kernel(x_ref, h_ref, oa_ref, os_ref, om_ref, wg0_ref, wg1_ref, wg2_ref,
                wa_ref, ws_ref, wm_ref, gpost_ref, o_ref, ss_scr, *, tn):
    j = pl.program_id(1)

    @pl.when(j == 0)
    def _():
        ss_scr[...] = jnp.zeros_like(ss_scr)

    h = h_ref[...]
    oa, os_, om = oa_ref[...], os_ref[...], om_ref[...]
    partial = jnp.zeros(ss_scr.shape, F32)
    for c in range(tn // MXU_WIDTH):
        sl = slice(c * MXU_WIDTH, (c + 1) * MXU_WIDTH)
        merged = (jax.nn.sigmoid(_dot(h, wg0_ref[:, sl])) * _dot(oa, wa_ref[:, sl])
                  + jax.nn.sigmoid(_dot(h, wg1_ref[:, sl])) * _dot(os_, ws_ref[:, sl])
                  + jax.nn.sigmoid(_dot(h, wg2_ref[:, sl])) * _dot(om, wm_ref[:, sl]))
        o_ref[:, pl.ds(pl.multiple_of(j * tn + c * MXU_WIDTH, MXU_WIDTH), MXU_WIDTH)] = merged
        sq = merged * merged
        partial += sum(sq[:, k * LANES:(k + 1) * LANES] for k in range(MXU_WIDTH // LANES))
    ss_scr[...] += partial

    @pl.when(j == pl.num_programs(1) - 1)
    def _():
        ss = jnp.sum(ss_scr[...], axis=-1, keepdims=True)
        inv = lax.rsqrt(ss / o_ref.shape[1] + EPS)
        o_ref[...] = x_ref[...] + (o_ref[...] * inv) * gpost_ref[...]


def _mix_residual(x2d, h, o_a, o_s, o_m, w_in_bf, gate_col0, w_oa, w_out_bf, g_post, tm, tn):
    m, d = x2d.shape
    nj = d // tn
    q_w, ssm_w, mem_w = o_a.shape[1], o_s.shape[1], o_m.shape[1]
    assert gate_col0 % tn == 0 and q_w % ssm_w == 0 and (q_w + ssm_w) % mem_w == 0
    gate_blk0 = gate_col0 // tn

    def rows(w):
        mode = dict(pipeline_mode=pl.Buffered(1)) if m == tm else {}
        return pl.BlockSpec((tm, w), lambda i, j: (i, 0), **mode)

    def gate_spec(br):
        return pl.BlockSpec((d, tn), lambda i, j: (0, gate_blk0 + br * nj + j))

    vec = pl.BlockSpec((1, d), lambda i, j: (0, 0))
    return pl.pallas_call(
        functools.partial(_mix_kernel, tn=tn),
        grid=(m // tm, nj),
        in_specs=[rows(d), rows(d), rows(q_w), rows(ssm_w), rows(mem_w),
                  gate_spec(0), gate_spec(1), gate_spec(2),
                  pl.BlockSpec((q_w, tn), lambda i, j: (0, j)),
                  pl.BlockSpec((ssm_w, tn), lambda i, j: (q_w // ssm_w, j)),
                  pl.BlockSpec((mem_w, tn), lambda i, j: ((q_w + ssm_w) // mem_w, j)), vec],
        out_specs=rows(d),
        out_shape=jax.ShapeDtypeStruct((m, d), F32),
        scratch_shapes=[pltpu.VMEM((tm, LANES), F32)],
        compiler_params=_params(("parallel", "arbitrary"), "mix"),
        name="mix",
    )(x2d, h, o_a, o_s, o_m, w_in_bf, w_in_bf, w_in_bf, w_oa, w_out_bf, w_out_bf,
      g_post.reshape(1, d))


def _ffn_kernel(x_ref, gpre_ref, *refs, n_sub, n_sub_valid):
    w_refs = refs[:3 * n_sub]
    cw_ref, cb_ref, gpost_ref, cprev_ref, o_ref, tail_ref, h_scr, a_scr, b_scr, carry_scr = refs[3 * n_sub:]
    j = pl.program_id(2)
    nb, tt, d = x_ref.shape
    tf = a_scr.shape[2]
    rows = nb * tt

    for s in range(n_sub):
        sub = j * n_sub + s
        step_cols = slice(s * tf, (s + 1) * tf)
        wua_ref, wub_ref, wd_ref = w_refs[3 * s:3 * s + 3]

        def run(h=None, sub=sub, step_cols=step_cols, wua_ref=wua_ref, wub_ref=wub_ref, wd_ref=wd_ref):
            _ffn_sub_step(sub, step_cols, x_ref, wua_ref, wub_ref, cw_ref, cb_ref, wd_ref, cprev_ref,
                          o_ref, tail_ref, h_scr, a_scr, b_scr, carry_scr, h)

        if s == 0:
            @pl.when(j == 0)
            def _(run=run):
                h = _rms(x_ref[...].reshape(rows, d), gpre_ref[...]).astype(BF16)
                h_scr[...] = h
                o_ref[...] = jnp.zeros_like(o_ref)
                run(h)

            pl.when((j > 0) & (sub < n_sub_valid))(run)
        else:
            pl.when(sub < n_sub_valid)(run)

        @pl.when(sub >= n_sub_valid)
        def _(step_cols=step_cols):
            tail_ref[:, :, step_cols] = jnp.zeros((nb, SUBLANES, tf), F32)

    @pl.when(j == pl.num_programs(2) - 1)
    def _():
        f = _rms(o_ref[...].reshape(rows, d), gpost_ref[...]).reshape(nb, tt, d)
        o_ref[...] = x_ref[...] + f


def _ffn_sub_step(sub, step_cols, x_ref, wua_ref, wub_ref, cw_ref, cb_ref, wd_ref, cprev_ref,
                  o_ref, tail_ref, h_scr, a_scr, b_scr, carry_scr, h=None):
    i = pl.program_id(1)
    nb, tt, d = x_ref.shape
    tf = a_scr.shape[2]
    h = h_scr[...] if h is None else h
    a_scr[:, 0:SUBLANES, :] = jnp.where(i == 0, cprev_ref[:, :, step_cols], carry_scr[sub])
    n_slices = tf // MXU_WIDTH
    cols = [slice(s * MXU_WIDTH, (s + 1) * MXU_WIDTH) for s in range(n_slices)]
    for sl in cols:
        a_scr[:, SUBLANES:, sl] = _dot(h, wua_ref[:, sl]).reshape(nb, tt, MXU_WIDTH)
        b_scr[:, sl] = _dot(h, wub_ref[:, sl])
    cw = cw_ref[:, step_cols]
    cb = cb_ref[:, step_cols]
    for sl in cols:
        gated = []
        for bi in range(nb):
            a = a_scr[bi, SUBLANES:, sl]
            first = jnp.concatenate([a_scr[bi, 0:SUBLANES, sl], a[0:SUBLANES]], axis=0)
            prev1 = jnp.concatenate([first[SUBLANES - 1:2 * SUBLANES - 1],
                                     pltpu.roll(a, 1, 0)[SUBLANES:]], axis=0)
            prev2 = jnp.concatenate([first[SUBLANES - 2:2 * SUBLANES - 2],
                                     pltpu.roll(a, 2, 0)[SUBLANES:]], axis=0)
            conv = (prev2 * cw[0:1, sl] + prev1 * cw[1:2, sl] + a * cw[2:3, sl]) + cb[:, sl]
            gated.append(_gelu(conv) * b_scr[bi * tt:(bi + 1) * tt, sl])
        act = (gated[0] if nb == 1 else jnp.concatenate(gated, axis=0)).astype(BF16)
        o_ref[...] += _dot(act, wd_ref[sl, :]).reshape(nb, tt, d)
    tail = a_scr[:, tt:tt + SUBLANES, :]
    carry_scr[sub] = tail
    tail_ref[:, :, step_cols] = tail


def _conv_ffn(x3d, g_pre, w_up, conv_w, conv_b, w_down, g_post, conv_prev8, nb, tt, tf, n_sub):
    b, t, d = x3d.shape
    d_ff = w_down.shape[0]
    n_valid = d_ff // tf
    step_cols = n_sub * tf
    nj = -(-n_valid // n_sub)
    d_ff_pad = nj * step_cols
    nt = t // tt
    pad = ((0, 0), (0, d_ff_pad - d_ff))
    conv_w = jnp.pad(conv_w.astype(F32), pad)
    conv_b = jnp.pad(conv_b.astype(F32).reshape(1, d_ff), pad)
    conv_prev8 = jnp.pad(conv_prev8, ((0, 0),) + pad)

    def sub_block(s):
        return lambda j: jnp.minimum(j * n_sub + s, n_valid - 1)

    w_specs = []
    for s in range(n_sub):
        blk = sub_block(s)
        w_specs += [pl.BlockSpec((d, tf), lambda bi, i, j, blk=blk: (0, blk(j))),
                    pl.BlockSpec((d, tf), lambda bi, i, j, blk=blk: (0, n_valid + blk(j))),
                    pl.BlockSpec((tf, d), lambda bi, i, j, blk=blk: (blk(j), 0))]
    mode = dict(pipeline_mode=pl.Buffered(1)) if (b // nb) * nt == 1 else {}
    x2, tails = pl.pallas_call(
        functools.partial(_ffn_kernel, n_sub=n_sub, n_sub_valid=n_valid),
        grid=(b // nb, nt, nj),
        in_specs=[pl.BlockSpec((nb, tt, d), lambda bi, i, j: (bi, i, 0), **mode),
                  pl.BlockSpec((1, d), lambda bi, i, j: (0, 0))] + w_specs + [
                  pl.BlockSpec((CONV_W, step_cols), lambda bi, i, j: (0, j)),
                  pl.BlockSpec((1, step_cols), lambda bi, i, j: (0, j)),
                  pl.BlockSpec((1, d), lambda bi, i, j: (0, 0)),
                  pl.BlockSpec((nb, SUBLANES, step_cols), lambda bi, i, j: (bi, 0, j))],
        out_specs=[pl.BlockSpec((nb, tt, d), lambda bi, i, j: (bi, i, 0), **mode),
                   pl.BlockSpec((nb, None, SUBLANES, step_cols), lambda bi, i, j: (bi, i, 0, j))],
        out_shape=[jax.ShapeDtypeStruct((b, t, d), F32),
                   jax.ShapeDtypeStruct((b, nt, SUBLANES, d_ff_pad), F32)],
        scratch_shapes=[pltpu.VMEM((nb * tt, d), BF16),
                        pltpu.VMEM((nb, tt + SUBLANES, tf), F32), pltpu.VMEM((nb * tt, tf), F32),
                        pltpu.VMEM((nj * n_sub, nb, SUBLANES, tf), F32)],
        compiler_params=_params(("arbitrary", "arbitrary", "arbitrary"), "ffn"),
        name="ffn",
    )(x3d, g_pre.reshape(1, d), *([w_up, w_up, w_down] * n_sub), conv_w, conv_b, g_post.reshape(1, d), conv_prev8)
    return x2, tails[..., :d_ff]


def _tiles(b, t):
    row_tile = 512
    tt = min(t, row_tile)
    return dict(
        proj_rows=512,
        attn_batch=max(1, min(b, 1024 // t)),
        s5_steps=min(t, 1024 // b),
        memattn_rows=tt,
        mix_rows=512, mix_cols=512,
        ffn_batch=min(b, row_tile // tt), ffn_rows=tt,
        ffn_sub_steps=2)


FFN_COLS = 2 * MXU_WIDTH


def _layer(x, attn_past, s0, conv_prev, mem_k, mem_v, bias, lw):
    b, t, d = x.shape
    m = b * t
    q_w, kv_w, ssm_w, mem_w = lw['q_w'], lw['kv_w'], lw['ssm_w'], lw['mem_w']
    n_state = lw['n_state']
    d_ff = lw['d_ff']
    tiles = _tiles(b, t)
    x2d = x.reshape(m, d)

    q, kv, u, qm, h = _norm_proj(x2d, lw['norm_pre_mix'], lw['proj_outputs'], tiles['proj_rows'])
    kv3 = kv.reshape(b, t, 2 * kv_w)
    if attn_past is None:
        hist_k = jnp.zeros((b, WINDOW, kv_w), F32)
        hist_v = hist_k
        first_valid = WINDOW
    else:
        hist_k = attn_past[0].astype(F32).reshape(b, WINDOW, kv_w)
        hist_v = attn_past[1].astype(F32).reshape(b, WINDOW, kv_w)
        first_valid = 0
    o_a = _band_attention(q.reshape(b, t, q_w), kv3, hist_k, hist_v, bias, lw['attn_sinks'],
                          tiles['attn_batch'], first_valid).reshape(m, q_w)

    if s0 is None:
        s0_re = jnp.zeros((b, n_state), F32)
        s0_im = s0_re
    else:
        s0_re = s0[0].astype(F32).reshape(b, n_state)
        s0_im = s0[1].astype(F32).reshape(b, n_state)
    u_tb = u.reshape(b, t, ssm_w).transpose(1, 0, 2)
    o_s_tb, s_re, s_im = _s5_layer(u_tb, s0_re, s0_im, lw['s5'], tiles['s5_steps'])
    o_s = o_s_tb.reshape(t, b, ssm_w).transpose(1, 0, 2).reshape(m, ssm_w)

    o_m = _memory_attention(qm, mem_k, mem_v, b, t, tiles['memattn_rows'])

    x1 = _mix_residual(x2d, h, o_a, o_s, o_m, lw['w_in_bf'], lw['gate_col0'], lw['w_oa'],
                       lw['w_out_bf'], lw['norm_post_mix'], tiles['mix_rows'], tiles['mix_cols'])

    if conv_prev is None:
        conv_prev8 = jnp.zeros((b, SUBLANES, d_ff), F32)
    else:
        conv_prev8 = jnp.pad(conv_prev.astype(F32), ((0, 0), (SUBLANES - (CONV_W - 1), 0), (0, 0)))
    x2, tails = _conv_ffn(x1.reshape(b, t, d), lw['norm_pre_ffn'], lw['w_up'], lw['conv_w'], lw['conv_b'],
                          lw['w_down'], lw['norm_post_ffn'], conv_prev8,
                          tiles['ffn_batch'], tiles['ffn_rows'], FFN_COLS, tiles['ffn_sub_steps'])
    conv_new = tails[:, -1, SUBLANES - (CONV_W - 1):, :]

    n_kv = kv_w // HEAD_DIM
    k_new = jnp.concatenate([hist_k, kv3[:, :, :kv_w]], axis=1)[:, -WINDOW:].reshape(b, WINDOW, n_kv, HEAD_DIM)
    v_new = jnp.concatenate([hist_v, kv3[:, :, kv_w:]], axis=1)[:, -WINDOW:].reshape(b, WINDOW, n_kv, HEAD_DIM)
    return x2, k_new, v_new, s_re, s_im, conv_new


def kernel(x_prompt, x_sample, cache_attn_k, cache_attn_v, cache_mem_k, cache_mem_v, state_ssm_re, state_ssm_im, state_conv, mem_prompt, rel_bias_table, norm_pre_mix, norm_post_mix, norm_pre_ffn, norm_post_ffn, norm_mem, w_in, attn_sinks, ssm_a_re, ssm_a_im, ssm_log_dt, ssm_b_re, ssm_b_im, ssm_c_re, ssm_c_im, ssm_d, w_glu, w_mem_kv, w_out, w_up, conv_w, conv_b, w_down):
    depth = w_in.shape[0]
    bp, _, d = x_prompt.shape
    n_mem = mem_prompt.shape[1]
    n_q = attn_sinks.shape[1]
    n_kv, hd = cache_attn_k.shape[-2:]
    assert hd == HEAD_DIM and n_kv == N_KV and cache_attn_k.shape[2] == WINDOW
    groups, p_state = ssm_a_re.shape[1:]
    q_w, kv_w = n_q * HEAD_DIM, n_kv * HEAD_DIM
    ssm_w = ssm_d.shape[1]
    mem_w = w_mem_kv.shape[2] // 2
    proj_w = q_w + 2 * kv_w + ssm_w + mem_w
    mem_hd = mem_w // MEM_HEADS

    xp, xs = x_prompt, x_sample
    outs = [[] for _ in range(12)]
    for l in range(depth):
        bias = _rel_bias(rel_bias_table)
        lp = dict(ssm_a_re=ssm_a_re[l], ssm_a_im=ssm_a_im[l], ssm_log_dt=ssm_log_dt[l],
                  ssm_b_re=ssm_b_re[l], ssm_b_im=ssm_b_im[l], ssm_c_re=ssm_c_re[l], ssm_c_im=ssm_c_im[l],
                  ssm_d=ssm_d[l], w_glu=w_glu[l])
        w_in_bf = w_in[l].astype(BF16)
        w_out_bf = w_out[l].astype(BF16)
        grp = n_q // n_kv
        w_q = w_in_bf[:, :q_w].reshape(d, n_kv, grp, HEAD_DIM).transpose(0, 2, 1, 3).reshape(d, q_w)
        w_oa = w_out_bf[:q_w].reshape(n_kv, grp, HEAD_DIM, d).transpose(1, 0, 2, 3).reshape(q_w, d)
        u_off = q_w + 2 * kv_w
        proj_outputs = [(BF16, [(w_q, q_w, 0)]),
                        (F32, _col_blocks(w_in_bf, q_w, 2 * kv_w)),
                        (F32, _col_blocks(w_in_bf, u_off, ssm_w)),
                        (BF16, _col_blocks(w_in_bf, u_off + ssm_w, mem_w)),
                        (BF16, [])]
        lw = dict(q_w=q_w, kv_w=kv_w, ssm_w=ssm_w, mem_w=mem_w, n_state=groups * p_state,
                  norm_pre_mix=norm_pre_mix[l], norm_post_mix=norm_post_mix[l],
                  norm_pre_ffn=norm_pre_ffn[l], norm_post_ffn=norm_post_ffn[l],
                  proj_outputs=proj_outputs, w_in_bf=w_in_bf, gate_col0=proj_w, w_oa=w_oa, w_out_bf=w_out_bf,
                  attn_sinks=attn_sinks[l], s5=_s5_params(lp),
                  d_ff=w_down.shape[1], w_up=w_up[l].astype(BF16), conv_w=conv_w[l], conv_b=conv_b[l],
                  w_down=w_down[l].astype(BF16))

        w_mem_bf = w_mem_kv[l].astype(BF16)
        mk_p, mv_p = _norm_proj(mem_prompt.reshape(bp * n_mem, d), norm_mem[l],
                                [(F32, [(w_mem_bf, mem_w, 0)]), (F32, [(w_mem_bf, mem_w, 1)])],
                                _tiles(bp, n_mem)['proj_rows'])
        mk_p = mk_p.reshape(bp, n_mem, mem_w)
        mv_p = mv_p.reshape(bp, n_mem, mem_w)
        xp, k_p, v_p, sr_p, si_p, c_p = _layer(xp, None, None, None, mk_p, mv_p, bias, lw)

        bs = xs.shape[0]
        xs, k_s, v_s, sr_s, si_s, c_s = _layer(
            xs, (cache_attn_k[l], cache_attn_v[l]), (state_ssm_re[l], state_ssm_im[l]), state_conv[l],
            cache_mem_k[l].reshape(bs, n_mem, mem_w), cache_mem_v[l].reshape(bs, n_mem, mem_w), bias, lw)

        vals = (k_p, v_p, sr_p.reshape(bp, groups, p_state), si_p.reshape(bp, groups, p_state), c_p,
                mk_p.reshape(bp, n_mem, MEM_HEADS, mem_hd), mv_p.reshape(bp, n_mem, MEM_HEADS, mem_hd),
                k_s, v_s, sr_s.reshape(bs, groups, p_state), si_s.reshape(bs, groups, p_state), c_s)
        for acc, val in zip(outs, vals):
            acc.append(val)
    return (xp, xs) + tuple(jnp.stack(o) for o in outs)
```
